```python
import math
import jax
import jax.numpy as jnp
from jax import lax
import numpy as np


D_MODEL = 2048
BATCH = 4
SEQ = 4096
DEPTH = 2

HEAD_DIM = 128
N_MIXERS = 4
GROUP_HEADS = D_MODEL // (N_MIXERS * HEAD_DIM)
GROUP_WIDTH = GROUP_HEADS * HEAD_DIM
MIX_WIDTH = N_MIXERS * GROUP_WIDTH
ROPE_THETA = 10000.0
Q_BLOCK = 128
NORM_EPS = 1e-6

DIFF_HEADS = GROUP_HEADS
DIFF_QK_DIM = HEAD_DIM // 2
DIFF_V_DIM = HEAD_DIM

GRID_W = 64
NA_HEADS = GROUP_HEADS
NA_WIN_ROWS = 8
NA_WIN_COLS = 16
NA_QCOLS = 16
NA_KV_COLS = 2 * NA_WIN_COLS

SWA_Q_HEADS = GROUP_HEADS
SWA_KV_HEADS = GROUP_HEADS // 2
SWA_WINDOW = 128

MLA_HEADS = GROUP_HEADS
MLA_Q_LORA = 512
MLA_KV_LORA = 512
MLA_NOPE = 128
MLA_ROPE = 64
MLA_V = HEAD_DIM

DIFF_COLS = 2 * DIFF_HEADS * 2 * DIFF_QK_DIM + DIFF_HEADS * DIFF_V_DIM
NA_COLS = 3 * NA_HEADS * HEAD_DIM
SWA_COLS = (SWA_Q_HEADS + 2 * SWA_KV_HEADS) * HEAD_DIM
MLA_COLS = MLA_Q_LORA + MLA_KV_LORA + MLA_ROPE
IN_COLS = DIFF_COLS + NA_COLS + SWA_COLS + MLA_COLS

FFN_DIM = 7 * D_MODEL // 2
N_EXPERTS = 8
TOP_K = 2
MOE_BLOCK = 512
N_DENSE = (DEPTH + 1) // 2
N_MOE = DEPTH // 2

PLE_DIM = 256

kernel_name = "hybrid_parallel_heads_encoder"


def rms_norm(x, g):
    x32 = x.astype(jnp.float32)
    y = x32 * lax.rsqrt(jnp.mean(x32 * x32, axis=-1, keepdims=True) + NORM_EPS)
    return (y * g.astype(jnp.float32)).astype(x.dtype)


def rope_tables(seq, dim):
    inv = ROPE_THETA ** (-jnp.arange(0, dim, 2, dtype=jnp.float32) / dim)
    ang = jnp.arange(seq, dtype=jnp.float32)[:, None] * inv[None, :]
    ang = jnp.concatenate([ang, ang], axis=-1)
    return jnp.cos(ang), jnp.sin(ang)


def apply_rope(x, cos, sin):
    x32 = x.astype(jnp.float32)
    half = x.shape[-1] // 2
    rot = jnp.concatenate([-x32[..., half:], x32[..., :half]], axis=-1)
    return (x32 * cos + rot * sin).astype(x.dtype)


def to_heads(t, n_heads):
    b, s, _ = t.shape
    return t.reshape(b, s, n_heads, -1).transpose(0, 2, 1, 3)


def from_heads(t):
    b, h, s, d = t.shape
    return t.transpose(0, 2, 1, 3).reshape(b, s, h * d)


def to_q_blocks(t):
    *lead, s, d = t.shape
    t = t.reshape(*lead, s // Q_BLOCK, Q_BLOCK, d)
    return jnp.moveaxis(t, -3, 0)


def from_q_blocks(o):
    o = jnp.moveaxis(o, 0, -3)
    *lead, nq, qb, d = o.shape
    return o.reshape(*lead, nq * qb, d)


def dense_attention(q, k, v, scale):
    def block(qb):
        s = jnp.einsum('bhqd,bhkd->bhqk', qb, k).astype(jnp.float32) * scale
        return jnp.einsum('bhqk,bhkd->bhqd', jax.nn.softmax(s, axis=-1).astype(v.dtype), v)
    return from_q_blocks(lax.map(block, to_q_blocks(q)))


def diff_attention(z, lq1, lk1, lq2, lk2, subln, lambda_init, cos, sin):
    b, s, _ = z.shape
    qk = DIFF_HEADS * 2 * DIFF_QK_DIM
    q = z[..., :qk].reshape(b, s, DIFF_HEADS, 2, DIFF_QK_DIM).transpose(0, 2, 3, 1, 4)
    k = z[..., qk:2 * qk].reshape(b, s, DIFF_HEADS, 2, DIFF_QK_DIM).transpose(0, 2, 3, 1, 4)
    v = to_heads(z[..., 2 * qk:], DIFF_HEADS)
    q = apply_rope(q, cos, sin)
    k = apply_rope(k, cos, sin)
    k1, k2 = k[:, :, 0], k[:, :, 1]
    lam = (jnp.exp(jnp.sum(lq1.astype(jnp.float32) * lk1.astype(jnp.float32)))
           - jnp.exp(jnp.sum(lq2.astype(jnp.float32) * lk2.astype(jnp.float32)))
           + lambda_init)
    scale = DIFF_QK_DIM ** -0.5

    def block(qb):
        s1 = jnp.einsum('bhqd,bhkd->bhqk', qb[:, :, 0], k1).astype(jnp.float32) * scale
        s2 = jnp.einsum('bhqd,bhkd->bhqk', qb[:, :, 1], k2).astype(jnp.float32) * scale
        w = jax.nn.softmax(s1, axis=-1) - lam * jax.nn.softmax(s2, axis=-1)
        return jnp.einsum('bhqk,bhkd->bhqd', w.astype(v.dtype), v)

    o = from_q_blocks(lax.map(block, to_q_blocks(q)))
    o = rms_norm(o, subln) * (1.0 - lambda_init)
    return from_heads(o)


def neighbourhood_attention(z, rpb):
    b, s, _ = z.shape
    w = NA_HEADS * HEAD_DIM
    q = to_heads(z[..., :w], NA_HEADS)
    k = to_heads(z[..., w:2 * w], NA_HEADS)
    v = to_heads(z[..., 2 * w:], NA_HEADS)
    rows = s // GRID_W
    kr = min(NA_WIN_ROWS, rows)
    q5 = q.reshape(b, NA_HEADS, rows, GRID_W, HEAD_DIM)
    k5 = k.reshape(b, NA_HEADS, rows, GRID_W, HEAD_DIM)
    v5 = v.reshape(b, NA_HEADS, rows, GRID_W, HEAD_DIM)
    row_start = np.clip(np.arange(rows) - kr // 2, 0, rows - kr).astype(np.int32)
    nj = GRID_W // NA_QCOLS
    blk_start = np.clip(np.arange(nj) * NA_QCOLS - NA_WIN_COLS // 2, 0, GRID_W - NA_KV_COLS)
    key_cols = blk_start[:, None] + np.arange(NA_KV_COLS)[None, :]
    q_cols = np.arange(GRID_W).reshape(nj, NA_QCOLS)
    win_start = np.clip(q_cols - NA_WIN_COLS // 2, 0, GRID_W - NA_WIN_COLS)
    col_mask = ((key_cols[:, None, :] >= win_start[..., None])
                & (key_cols[:, None, :] < win_start[..., None] + NA_WIN_COLS))
    col_idx = np.clip(key_cols[:, None, :] - q_cols[..., None],
                      -(NA_WIN_COLS - 1), NA_WIN_COLS - 1) + NA_WIN_COLS - 1
    rpb_cols = rpb[:, :, col_idx].astype(jnp.float32)
    scale = HEAD_DIM ** -0.5

    def row_step(args):
        q_row, r, rs = args
        kwin = lax.dynamic_slice_in_dim(k5, rs, kr, axis=2)
        vwin = lax.dynamic_slice_in_dim(v5, rs, kr, axis=2)
        kb = kwin[:, :, :, key_cols]
        vb = vwin[:, :, :, key_cols]
        qb = q_row.reshape(b, NA_HEADS, nj, NA_QCOLS, HEAD_DIM)
        sc = jnp.einsum('bhjqd,bhajcd->bhjqac', qb, kb).astype(jnp.float32) * scale
        row_idx = rs + jnp.arange(kr) - r + NA_WIN_ROWS - 1
        bias = jnp.take(rpb_cols, row_idx, axis=1).transpose(0, 2, 3, 1, 4)
        sc = jnp.where(col_mask[None, None, :, :, None, :], sc + bias[None], -jnp.inf)
        prob = jax.nn.softmax(sc, axis=(-2, -1))
        o = jnp.einsum('bhjqac,bhajcd->bhjqd', prob.astype(v.dtype), vb)
        return o.reshape(b, NA_HEADS, GRID_W, HEAD_DIM)

    out = lax.map(row_step, (jnp.moveaxis(q5, 2, 0), jnp.arange(rows, dtype=jnp.int32),
                             jnp.asarray(row_start)))
    out = jnp.moveaxis(out, 0, 2).reshape(b, NA_HEADS, s, HEAD_DIM)
    return from_heads(out)


def sliding_window_attention(z, sinks, cos, sin):
    b, s, _ = z.shape
    nq = SWA_Q_HEADS * HEAD_DIM
    nkv = SWA_KV_HEADS * HEAD_DIM
    q = apply_rope(to_heads(z[..., :nq], SWA_Q_HEADS), cos, sin)
    k = apply_rope(to_heads(z[..., nq:nq + nkv], SWA_KV_HEADS), cos, sin)
    v = to_heads(z[..., nq + nkv:], SWA_KV_HEADS)
    g = SWA_Q_HEADS // SWA_KV_HEADS
    nb = s // Q_BLOCK
    n_side = SWA_WINDOW // Q_BLOCK
    span = (2 * n_side + 1) * Q_BLOCK

    def band(t):
        tp = jnp.pad(t, ((0, 0), (0, 0), (SWA_WINDOW, SWA_WINDOW), (0, 0)))
        tp = tp.reshape(b, SWA_KV_HEADS, nb + 2 * n_side, Q_BLOCK, HEAD_DIM)
        return jnp.concatenate([tp[:, :, o:o + nb] for o in range(2 * n_side + 1)], axis=3)

    kb, vb = band(k), band(v)
    qb = q.reshape(b, SWA_KV_HEADS, g, nb, Q_BLOCK, HEAD_DIM)
    sc = jnp.einsum('bkgnqd,bkncd->bkgnqc', qb, kb).astype(jnp.float32) * HEAD_DIM ** -0.5
    qpos = jnp.arange(nb)[:, None, None] * Q_BLOCK + jnp.arange(Q_BLOCK)[None, :, None]
    kpos = jnp.arange(nb)[:, None, None] * Q_BLOCK - SWA_WINDOW + jnp.arange(span)[None, None, :]
    valid = (kpos >= 0) & (kpos < s) & (jnp.abs(qpos - kpos) <= SWA_WINDOW)
    sc = jnp.where(valid, sc, -jnp.inf)
    sink = jnp.broadcast_to(sinks.astype(jnp.float32).reshape(1, SWA_KV_HEADS, g, 1, 1, 1),
                            sc.shape[:-1] + (1,))
    prob = jax.nn.softmax(jnp.concatenate([sc, sink], axis=-1), axis=-1)[..., :-1]
    o = jnp.einsum('bkgnqc,bkncd->bkgnqd', prob.astype(v.dtype), vb)
    return from_heads(o.reshape(b, SWA_Q_HEADS, s, HEAD_DIM))


def latent_attention(z, q_norm, kv_norm, w_uq, w_ukv, cos, sin):
    b, s, _ = z.shape
    cq = rms_norm(z[..., :MLA_Q_LORA], q_norm)
    ckv = rms_norm(z[..., MLA_Q_LORA:MLA_Q_LORA + MLA_KV_LORA], kv_norm)
    k_rope = apply_rope(z[..., MLA_Q_LORA + MLA_KV_LORA:][:, None], cos, sin)
    q = to_heads(cq @ w_uq, MLA_HEADS)
    q = jnp.concatenate([q[..., :MLA_NOPE], apply_rope(q[..., MLA_NOPE:], cos, sin)], axis=-1)
    kv = to_heads(ckv @ w_ukv, MLA_HEADS)
    k = jnp.concatenate([kv[..., :MLA_NOPE],
                         jnp.broadcast_to(k_rope, (b, MLA_HEADS, s, MLA_ROPE))], axis=-1)
    v = kv[..., MLA_NOPE:]
    return from_heads(dense_attention(q, k, v, (MLA_NOPE + MLA_ROPE) ** -0.5))


def swiglu(x, w_gate, w_up, w_down):
    return (jax.nn.silu(x @ w_gate) * (x @ w_up)) @ w_down


def moe_swiglu(h, w_router, w_gate, w_up, w_down):
    n, d = h.shape
    logits = (h @ w_router).astype(jnp.float32)
    top_val, top_idx = lax.top_k(logits, TOP_K)
    gates = jax.nn.softmax(top_val, axis=-1)
    n_assign = n * TOP_K
    flat_e = top_idx.reshape(-1)
    flat_tok = jnp.repeat(jnp.arange(n, dtype=jnp.int32), TOP_K)
    flat_g = gates.reshape(-1)
    order = jnp.argsort(flat_e)
    e_s, tok_s, g_s = flat_e[order], flat_tok[order], flat_g[order]
    counts = jnp.bincount(flat_e, length=N_EXPERTS)
    padded = (counts + MOE_BLOCK - 1) // MOE_BLOCK * MOE_BLOCK
    grp_start = jnp.cumsum(counts) - counts
    pad_end = jnp.cumsum(padded)
    pad_start = pad_end - padded
    dest = pad_start[e_s] + jnp.arange(n_assign) - grp_start[e_s]
    n_blk = n_assign // MOE_BLOCK + N_EXPERTS
    buf_tok = jnp.full((n_blk * MOE_BLOCK,), n, dtype=jnp.int32).at[dest].set(tok_s)
    blk_exp = jnp.minimum(jnp.searchsorted(pad_end, jnp.arange(n_blk) * MOE_BLOCK, side='right'),
                          N_EXPERTS - 1)
    h_pad = jnp.concatenate([h, jnp.zeros((1, d), h.dtype)], axis=0)

    def expert_block(args):
        tok, e = args
        xb = h_pad[tok]
        return swiglu(xb, w_gate[e], w_up[e], w_down[e])

    y_buf = lax.map(expert_block, (buf_tok.reshape(n_blk, MOE_BLOCK), blk_exp))
    y = y_buf.reshape(-1, d)[dest]
    return jnp.zeros((n, d), h.dtype).at[tok_s].add(g_s[:, None].astype(h.dtype) * y)


def setup_inputs(seed: int = 0) -> dict:
    key = jax.random.key(seed)
    ks = iter(jax.random.split(key, 32))

    def nrm(shape, scale):
        return jax.random.normal(next(ks), shape, jnp.float32) * scale

    def gain(shape):
        return 1.0 + 0.05 * jax.random.normal(next(ks), shape, jnp.float32)

    return {
        'x': nrm((BATCH, SEQ, D_MODEL), 1.0),
        'p': nrm((DEPTH, BATCH, SEQ, PLE_DIM), 1.0),
        'attn_norm': gain((DEPTH, D_MODEL)),
        'w_in': nrm((DEPTH, D_MODEL, IN_COLS), D_MODEL ** -0.5),
        'diff_lq1': nrm((DEPTH, DIFF_QK_DIM), 0.1),
        'diff_lk1': nrm((DEPTH, DIFF_QK_DIM), 0.1),
        'diff_lq2': nrm((DEPTH, DIFF_QK_DIM), 0.1),
        'diff_lk2': nrm((DEPTH, DIFF_QK_DIM), 0.1),
        'diff_subln': gain((DEPTH, DIFF_V_DIM)),
        'na_rpb': nrm((DEPTH, NA_HEADS, 2 * NA_WIN_ROWS - 1, 2 * NA_WIN_COLS - 1), 0.1),
        'swa_sinks': nrm((DEPTH, SWA_Q_HEADS), 0.5),
        'mla_q_norm': gain((DEPTH, MLA_Q_LORA)),
        'mla_kv_norm': gain((DEPTH, MLA_KV_LORA)),
        'mla_w_uq': nrm((DEPTH, MLA_Q_LORA, MLA_HEADS * (MLA_NOPE + MLA_ROPE)), MLA_Q_LORA ** -0.5),
        'mla_w_ukv': nrm((DEPTH, MLA_KV_LORA, MLA_HEADS * (MLA_NOPE + MLA_V)), MLA_KV_LORA ** -0.5),
        'w_out': nrm((DEPTH, MIX_WIDTH, D_MODEL), MIX_WIDTH ** -0.5),
        'ffn_norm': gain((DEPTH, D_MODEL)),
        'dense_w_gate': nrm((N_DENSE, D_MODEL, FFN_DIM), D_MODEL ** -0.5),
        'dense_w_up': nrm((N_DENSE, D_MODEL, FFN_DIM), D_MODEL ** -0.5),
        'dense_w_down': nrm((N_DENSE, FFN_DIM, D_MODEL), FFN_DIM ** -0.5),
        'moe_router': nrm((N_MOE, D_MODEL, N_EXPERTS), D_MODEL ** -0.5),
        'moe_w_gate': nrm((N_MOE, N_EXPERTS, D_MODEL, FFN_DIM), D_MODEL ** -0.5),
        'moe_w_up': nrm((N_MOE, N_EXPERTS, D_MODEL, FFN_DIM), D_MODEL ** -0.5),
        'moe_w_down': nrm((N_MOE, N_EXPERTS, FFN_DIM, D_MODEL), FFN_DIM ** -0.5),
        'ple_norm': gain((DEPTH, D_MODEL)),
        'ple_gate': nrm((DEPTH, D_MODEL, D_MODEL), D_MODEL ** -0.5),
        'ple_proj': nrm((DEPTH, PLE_DIM, D_MODEL), PLE_DIM ** -0.5),
        'final_norm': gain((D_MODEL,)),
    }


def reference(x, p, attn_norm, w_in, diff_lq1, diff_lk1, diff_lq2, diff_lk2, diff_subln,
              na_rpb, swa_sinks, mla_q_norm, mla_kv_norm, mla_w_uq, mla_w_ukv, w_out,
              ffn_norm, dense_w_gate, dense_w_up, dense_w_down, moe_router, moe_w_gate,
              moe_w_up, moe_w_down, ple_norm, ple_gate, ple_proj, final_norm):
    b, s, _ = x.shape
    cos_d, sin_d = rope_tables(s, DIFF_QK_DIM)
    cos_h, sin_h = rope_tables(s, HEAD_DIM)
    cos_r, sin_r = rope_tables(s, MLA_ROPE)
    o0 = DIFF_COLS
    o1 = o0 + NA_COLS
    o2 = o1 + SWA_COLS
    h = x
    for i in range(DEPTH):
        z = rms_norm(h, attn_norm[i]) @ w_in[i]
        lambda_init = 0.8 - 0.6 * math.exp(-0.3 * i)
        mixed = jnp.concatenate([
            diff_attention(z[..., :o0], diff_lq1[i], diff_lk1[i], diff_lq2[i], diff_lk2[i],
                           diff_subln[i], lambda_init, cos_d, sin_d),
            neighbourhood_attention(z[..., o0:o1], na_rpb[i]),
            sliding_window_attention(z[..., o1:o2], swa_sinks[i], cos_h, sin_h),
            latent_attention(z[..., o2:], mla_q_norm[i], mla_kv_norm[i], mla_w_uq[i],
                             mla_w_ukv[i], cos_r, sin_r),
        ], axis=-1)
        h = h + mixed @ w_out[i]
        hn = rms_norm(h, ffn_norm[i])
        j = i // 2
        if i % 2 == 0:
            h = h + swiglu(hn, dense_w_gate[j], dense_w_up[j], dense_w_down[j])
        else:
            h = h + moe_swiglu(hn.reshape(b * s, -1), moe_router[j], moe_w_gate[j],
                               moe_w_up[j], moe_w_down[j]).reshape(b, s, -1)
        gate = jax.nn.sigmoid(rms_norm(h, ple_norm[i]) @ ple_gate[i])
        h = h + gate * (p[i] @ ple_proj[i])
    return rms_norm(h, final_norm)
```

```python
import functools
import math

import numpy as np
import jax
import jax.numpy as jnp
from jax import lax
from jax.experimental import pallas as pl
from jax.experimental.pallas import tpu as pltpu

F32 = jnp.float32
BF16 = jnp.bfloat16

D_MODEL = 2048
HEAD_DIM = 128
GROUP_HEADS = 4
GROUP_WIDTH = GROUP_HEADS * HEAD_DIM
ROPE_THETA = 10000.0
NORM_EPS = 1e-6
DIFF_QK_DIM = 64
GRID_W = 64
NA_WIN_ROWS = 8
NA_WIN_COLS = 16
SWA_WINDOW = 128
MLA_LORA = 512
MLA_NOPE = 128
MLA_ROPE = 64
IN_COLS = 5184
FFN_DIM = 7168
N_EXPERTS = 8
PLE_DIM = 256

LANES = 128
VMEM_LIMIT = 56 * 1024 * 1024
MASKED = -1e30

IN_TN = 512
IN_COLS_PAD = 11 * IN_TN
NA_GROUP_ROWS = 4
NA_KEY_ROWS = NA_GROUP_ROWS + NA_WIN_ROWS
SWA_TQ = 512
SWA_SPAN = SWA_TQ + 2 * SWA_WINDOW
MOE_TM = 512
GATHER_ROWS = 256


def _tile(n, pref):
    t = min(n, pref)
    assert n % t == 0, (n, t)
    return t


def _cparams(*sem):
    return pltpu.CompilerParams(dimension_semantics=sem, vmem_limit_bytes=VMEM_LIMIT)


def _rms(x, g):
    return x * lax.rsqrt(jnp.mean(x * x, axis=-1, keepdims=True) + NORM_EPS) * g


def _rmsnorm_kernel(x_ref, g_ref, o_ref):
    o_ref[...] = _rms(x_ref[...], g_ref[...]).astype(o_ref.dtype)


def _rmsnorm(x, g, tm=512):
    n, d = x.shape
    tm = _tile(n, tm)
    return pl.pallas_call(
        _rmsnorm_kernel,
        grid=(n // tm,),
        in_specs=[pl.BlockSpec((tm, d), lambda i: (i, 0)),
                  pl.BlockSpec((1, d), lambda i: (0, 0))],
        out_specs=pl.BlockSpec((tm, d), lambda i: (i, 0)),
        out_shape=jax.ShapeDtypeStruct((n, d), BF16),
        compiler_params=_cparams("parallel"),
        name="rmsnorm",
    )(x, g.reshape(1, d))


def _rope_tables(seq):
    def base(dim):
        inv = ROPE_THETA ** (-jnp.arange(0, dim, 2, dtype=F32) / dim)
        ang = jnp.arange(seq, dtype=F32)[:, None] * inv[None, :]
        ang = jnp.concatenate([ang, ang], axis=-1)
        return jnp.cos(ang), jnp.sin(ang)

    lane = np.arange(LANES)
    cos, sin = base(DIFF_QK_DIM)
    c64 = jnp.concatenate([cos, cos], axis=-1)
    s64 = jnp.concatenate([sin, sin], axis=-1)
    lo = jnp.asarray((lane % 64) < 32)
    sa64 = jnp.where(lo, -s64, 0.0)
    sb64 = jnp.where(lo, 0.0, s64)
    first = jnp.asarray(lane < 64)
    c64p, sa64p, sb64p = (jnp.where(first, t, 0.0) for t in (c64, sa64, sb64))
    cos, sin = base(HEAD_DIM)
    s128 = jnp.where(jnp.asarray(lane < 64), -sin, sin)
    return dict(c64=c64, sa64=sa64, sb64=sb64, c64p=c64p, sa64p=sa64p, sb64p=sb64p,
                c128=cos, s128=s128)


def _rope64(x, c, sa, sb):
    return x * c + pltpu.roll(x, 96, 1) * sa + pltpu.roll(x, 32, 1) * sb


def _rope128(x, c, s):
    return x * c + pltpu.roll(x, 64, 1) * s


def _inproj_kernel(x_ref, w_ref, c64, sa64, sb64, c128, s128, o_ref, acc_ref):
    j = pl.program_id(1)
    acc_ref[...] = jnp.dot(x_ref[...], w_ref[...], preferred_element_type=F32)
    groups = IN_TN // LANES

    def seg(g):
        return slice(g * LANES, (g + 1) * LANES)

    def put(g, y):
        o_ref[:, seg(g)] = y.astype(o_ref.dtype)

    @pl.when(j == 0)
    def _():
        for g in range(groups):
            put(g, _rope64(acc_ref[:, seg(g)], c64[...], sa64[...], sb64[...]) * DIFF_QK_DIM ** -0.5)

    @pl.when(j == 1)
    def _():
        for g in range(groups):
            put(g, _rope64(acc_ref[:, seg(g)], c64[...], sa64[...], sb64[...]))

    @pl.when(j == 3)
    def _():
        for g in range(groups):
            put(g, acc_ref[:, seg(g)] * HEAD_DIM ** -0.5)

    @pl.when(j == 6)
    def _():
        for g in range(groups):
            put(g, _rope128(acc_ref[:, seg(g)], c128[...], s128[...]) * HEAD_DIM ** -0.5)

    @pl.when(j == 7)
    def _():
        for g in range(groups):
            x = acc_ref[:, seg(g)]
            put(g, _rope128(x, c128[...], s128[...]) if g < 2 else x)

    plain = (j == 2) | (j == 4) | (j == 5) | (j >= 8)

    @pl.when(plain)
    def _():
        o_ref[...] = acc_ref[...].astype(o_ref.dtype)


def _inproj(xn, w, tabs, seq, tm=1024):
    n, d = xn.shape
    tm = _tile(seq, tm)
    per_seq = seq // tm
    tab_spec = pl.BlockSpec((tm, LANES), lambda i, j: (i % per_seq, 0))
    return pl.pallas_call(
        _inproj_kernel,
        grid=(n // tm, IN_COLS_PAD // IN_TN),
        in_specs=[pl.BlockSpec((tm, d), lambda i, j: (i, 0)),
                  pl.BlockSpec((d, IN_TN), lambda i, j: (0, j))] + [tab_spec] * 5,
        out_specs=pl.BlockSpec((tm, IN_TN), lambda i, j: (i, j)),
        out_shape=jax.ShapeDtypeStruct((n, IN_COLS_PAD), BF16),
        scratch_shapes=[pltpu.VMEM((tm, IN_TN), F32)],
        compiler_params=_cparams("parallel", "arbitrary"),
        name="inproj",
    )(xn, w, tabs["c64"], tabs["sa64"], tabs["sb64"], tabs["c128"], tabs["s128"])


def _mla_up_kernel(cq_ref, ckv_ref, kr_ref, gq_ref, gkv_ref, wq_ref, wkv_ref, c_ref, sa_ref, sb_ref,
                   q_ref, k_ref, v_ref):
    scale = (MLA_NOPE + MLA_ROPE) ** -0.5
    c, sa, sb = c_ref[...], sa_ref[...], sb_ref[...]
    cq = _rms(cq_ref[...].astype(F32), gq_ref[...]).astype(BF16)
    qa = jnp.dot(cq, wq_ref[...], preferred_element_type=F32)
    for h in range(GROUP_HEADS):
        base = h * 2 * LANES
        q_ref[:, base:base + LANES] = (qa[:, base:base + LANES] * scale).astype(q_ref.dtype)
        q_ref[:, base + LANES:base + 2 * LANES] = (
            _rope64(qa[:, base + LANES:base + 2 * LANES], c, sa, sb) * scale).astype(q_ref.dtype)
    ckv = _rms(ckv_ref[...].astype(F32), gkv_ref[...]).astype(BF16)
    kva = jnp.dot(ckv, wkv_ref[...], preferred_element_type=F32)
    kr = _rope64(kr_ref[...].astype(F32), c, sa, sb).astype(k_ref.dtype)
    for h in range(GROUP_HEADS):
        base = h * 2 * LANES
        k_ref[:, base:base + LANES] = kva[:, h * LANES:(h + 1) * LANES].astype(k_ref.dtype)
        k_ref[:, base + LANES:base + 2 * LANES] = kr
    v_ref[...] = kva[:, GROUP_WIDTH:].astype(v_ref.dtype)


def _mla_up(z, gq, gkv, wq, wkv, tabs, seq, tm=512):
    n = z.shape[0]
    tm = _tile(seq, tm)
    per_seq = seq // tm
    tab_spec = pl.BlockSpec((tm, LANES), lambda i: (i % per_seq, 0))
    full = lambda a: pl.BlockSpec(a.shape, lambda i: (0,) * a.ndim)
    gq = gq.reshape(1, -1)
    gkv = gkv.reshape(1, -1)
    return pl.pallas_call(
        _mla_up_kernel,
        grid=(n // tm,),
        in_specs=[pl.BlockSpec((tm, MLA_LORA), lambda i: (i, 8)),
                  pl.BlockSpec((tm, MLA_LORA), lambda i: (i, 9)),
                  pl.BlockSpec((tm, LANES), lambda i: (i, 40)),
                  full(gq), full(gkv), full(wq), full(wkv), tab_spec, tab_spec, tab_spec],
        out_specs=[pl.BlockSpec((tm, 2 * GROUP_WIDTH), lambda i: (i, 0)),
                   pl.BlockSpec((tm, 2 * GROUP_WIDTH), lambda i: (i, 0)),
                   pl.BlockSpec((tm, GROUP_WIDTH), lambda i: (i, 0))],
        out_shape=[jax.ShapeDtypeStruct((n, 2 * GROUP_WIDTH), BF16),
                   jax.ShapeDtypeStruct((n, 2 * GROUP_WIDTH), BF16),
                   jax.ShapeDtypeStruct((n, GROUP_WIDTH), BF16)],
        compiler_params=_cparams("parallel"),
        name="mla_up",
    )(z, z, z, gq, gkv, wq, wkv, tabs["c64p"], tabs["sa64p"], tabs["sb64p"])


def _qk(q, k):
    return lax.dot_general(q, k, (((1,), (1,)), ((), ())), preferred_element_type=F32)


def _online_step(s, v, m_ref, l_ref, acc_ref):
    m_prev = m_ref[...]
    m_new = jnp.maximum(m_prev, jnp.max(s, axis=-1, keepdims=True))
    alpha = jnp.exp(m_prev - m_new)
    p = jnp.exp(s - m_new)
    l_ref[...] = alpha * l_ref[...] + jnp.sum(p, axis=-1, keepdims=True)
    acc_ref[...] = alpha * acc_ref[...] + jnp.dot(p.astype(v.dtype), v, preferred_element_type=F32)
    m_ref[...] = m_new


def _flash_kernel(q_ref, k_ref, v_ref, o_ref, m_ref, l_ref, acc_ref, *, tk):
    q = q_ref[...]
    m_ref[...] = jnp.full(m_ref.shape, MASKED, F32)
    l_ref[...] = jnp.zeros(l_ref.shape, F32)
    acc_ref[...] = jnp.zeros(acc_ref.shape, F32)

    def body(t, carry):
        off = pl.multiple_of(t * tk, tk)
        _online_step(_qk(q, k_ref[pl.ds(off, tk), :]), v_ref[pl.ds(off, tk), :], m_ref, l_ref, acc_ref)
        return carry

    lax.fori_loop(0, k_ref.shape[0] // tk, body, 0)
    o_ref[...] = (acc_ref[...] / l_ref[...]).astype(o_ref.dtype)


def _mla_attention(q, k, v, batch, seq, tq=512, tk=512):
    n = q.shape[0]
    tq, tk = _tile(seq, tq), _tile(seq, tk)
    nq = seq // tq
    dq = 2 * LANES
    return pl.pallas_call(
        functools.partial(_flash_kernel, tk=tk),
        grid=(batch, GROUP_HEADS, nq),
        in_specs=[pl.BlockSpec((tq, dq), lambda b, h, i: (b * nq + i, h)),
                  pl.BlockSpec((seq, dq), lambda b, h, i: (b, h)),
                  pl.BlockSpec((seq, LANES), lambda b, h, i: (b, h))],
        out_specs=pl.BlockSpec((tq, LANES), lambda b, h, i: (b * nq + i, h)),
        out_shape=jax.ShapeDtypeStruct((n, GROUP_WIDTH), BF16),
        scratch_shapes=[pltpu.VMEM((tq, 1), F32), pltpu.VMEM((tq, 1), F32), pltpu.VMEM((tq, LANES), F32)],
        compiler_params=_cparams("parallel", "parallel", "arbitrary"),
        name="mla_attention",
    )(q, k, v)


def _diff_kernel(lq1_ref, lk1_ref, lq2_ref, lk2_ref, g_ref, q_ref, k_ref, v_ref, o_ref,
                 m1, l1, a1, m2, l2, a2, *, tk, lambda_init):
    q = q_ref[...]
    lane = lax.broadcasted_iota(jnp.int32, q.shape, 1)
    q1 = jnp.where(lane < DIFF_QK_DIM, q, jnp.zeros_like(q))
    q2 = jnp.where(lane >= DIFF_QK_DIM, q, jnp.zeros_like(q))
    for m, l, a in ((m1, l1, a1), (m2, l2, a2)):
        m[...] = jnp.full(m.shape, MASKED, F32)
        l[...] = jnp.zeros(l.shape, F32)
        a[...] = jnp.zeros(a.shape, F32)

    def body(t, carry):
        off = pl.multiple_of(t * tk, tk)
        k = k_ref[pl.ds(off, tk), :]
        v = v_ref[pl.ds(off, tk), :]
        _online_step(_qk(q1, k), v, m1, l1, a1)
        _online_step(_qk(q2, k), v, m2, l2, a2)
        return carry

    lax.fori_loop(0, k_ref.shape[0] // tk, body, 0)
    lam = (jnp.exp(jnp.sum(lq1_ref[...] * lk1_ref[...], axis=-1, keepdims=True))
           - jnp.exp(jnp.sum(lq2_ref[...] * lk2_ref[...], axis=-1, keepdims=True)) + lambda_init)
    o = a1[...] / l1[...] - lam * (a2[...] / l2[...])
    o_ref[...] = (_rms(o, g_ref[...]) * (1.0 - lambda_init)).astype(o_ref.dtype)


def _diff_attention(z, lq1, lk1, lq2, lk2, subln, lambda_init, batch, seq, tq=512, tk=512):
    n = z.shape[0]
    tq, tk = _tile(seq, tq), _tile(seq, tk)
    nq = seq // tq
    vec = lambda a: a.reshape(1, -1)
    small = lambda a: pl.BlockSpec(a.shape, lambda b, h, i: (0, 0))
    params = [vec(a) for a in (lq1, lk1, lq2, lk2, subln)]
    vm = lambda w: pltpu.VMEM((tq, w), F32)
    return pl.pallas_call(
        functools.partial(_diff_kernel, tk=tk, lambda_init=lambda_init),
        grid=(batch, GROUP_HEADS, nq),
        in_specs=[small(a) for a in params] + [
            pl.BlockSpec((tq, LANES), lambda b, h, i: (b * nq + i, h)),
            pl.BlockSpec((seq, LANES), lambda b, h, i: (b, 4 + h)),
            pl.BlockSpec((seq, LANES), lambda b, h, i: (b, 8 + h))],
        out_specs=pl.BlockSpec((tq, LANES), lambda b, h, i: (b * nq + i, h)),
        out_shape=jax.ShapeDtypeStruct((n, GROUP_WIDTH), BF16),
        scratch_shapes=[vm(1), vm(1), vm(LANES), vm(1), vm(1), vm(LANES)],
        compiler_params=_cparams("parallel", "parallel", "arbitrary"),
        name="diff_attention",
    )(*params, z, z, z)


def _na_bias_tables(rpb, rows):
    groups = rows // NA_GROUP_ROWS
    assert groups >= 3
    qc = np.arange(GRID_W)
    kc = np.arange(GRID_W)
    ws = np.clip(qc - NA_WIN_COLS // 2, 0, GRID_W - NA_WIN_COLS)
    col_ok = (kc[None, :] >= ws[:, None]) & (kc[None, :] < ws[:, None] + NA_WIN_COLS)
    cidx = np.clip(kc[None, :] - qc[:, None], -(NA_WIN_COLS - 1), NA_WIN_COLS - 1) + NA_WIN_COLS - 1
    tabs = []
    for m in (0, 1, groups - 1):
        kstart = np.clip(NA_GROUP_ROWS * m - NA_WIN_ROWS // 2, 0, rows - NA_KEY_ROWS)
        qr = NA_GROUP_ROWS * m + np.arange(NA_GROUP_ROWS)
        kr = kstart + np.arange(NA_KEY_ROWS)
        rs = np.clip(qr - NA_WIN_ROWS // 2, 0, rows - NA_WIN_ROWS)
        row_ok = (kr[None, :] >= rs[:, None]) & (kr[None, :] < rs[:, None] + NA_WIN_ROWS)
        ridx = np.clip(kr[None, :] - qr[:, None] + NA_WIN_ROWS - 1, 0, 2 * NA_WIN_ROWS - 2)
        bias = rpb[:, ridx[:, None, :, None], cidx[None, :, None, :]]
        ok = row_ok[:, None, :, None] & col_ok[None, :, None, :]
        tab = jnp.where(jnp.asarray(ok)[None], bias.astype(F32), MASKED)
        tabs.append(tab.reshape(rpb.shape[0], NA_GROUP_ROWS * GRID_W, NA_KEY_ROWS * GRID_W))
    return jnp.stack(tabs)


def _na_kernel(bias_ref, q_ref, k_ref, v_ref, o_ref, *, rows):
    groups = rows // NA_GROUP_ROWS
    tq = NA_GROUP_ROWS * GRID_W
    span = NA_KEY_ROWS * GRID_W

    def body(m, carry):
        cls = jnp.where(m == 0, 0, jnp.where(m == groups - 1, 2, 1))
        krow = jnp.clip(NA_GROUP_ROWS * m - NA_WIN_ROWS // 2, 0, rows - NA_KEY_ROWS)
        koff = pl.multiple_of(krow * GRID_W, GRID_W)
        qoff = pl.multiple_of(m * tq, tq)
        s = _qk(q_ref[pl.ds(qoff, tq), :], k_ref[pl.ds(koff, span), :]) + bias_ref[cls, 0]
        p = jnp.exp(s - jnp.max(s, axis=-1, keepdims=True))
        l = jnp.sum(p, axis=-1, keepdims=True)
        v = v_ref[pl.ds(koff, span), :]
        o = jnp.dot(p.astype(v.dtype), v, preferred_element_type=F32) / l
        o_ref[pl.ds(qoff, tq), :] = o.astype(o_ref.dtype)
        return carry

    lax.fori_loop(0, groups, body, 0)


def _na_attention(z, rpb, batch, seq):
    n = z.shape[0]
    rows = seq // GRID_W
    bias = _na_bias_tables(rpb, rows)
    tq, span = NA_GROUP_ROWS * GRID_W, NA_KEY_ROWS * GRID_W
    return pl.pallas_call(
        functools.partial(_na_kernel, rows=rows),
        grid=(batch, GROUP_HEADS),
        in_specs=[pl.BlockSpec((3, 1, tq, span), lambda b, h: (0, h, 0, 0)),
                  pl.BlockSpec((seq, LANES), lambda b, h: (b, 12 + h)),
                  pl.BlockSpec((seq, LANES), lambda b, h: (b, 16 + h)),
                  pl.BlockSpec((seq, LANES), lambda b, h: (b, 20 + h))],
        out_specs=pl.BlockSpec((seq, LANES), lambda b, h: (b, h)),
        out_shape=jax.ShapeDtypeStruct((n, GROUP_WIDTH), BF16),
        compiler_params=_cparams("parallel", "parallel"),
        name="na_attention",
    )(bias, z, z, z)


def _swa_kernel(sink_ref, q_ref, k_ref, v_ref, o_ref, *, seq):
    h = pl.program_id(1)
    q0 = pl.program_id(2) * SWA_TQ
    koff = pl.multiple_of(jnp.clip(q0 - SWA_WINDOW, 0, seq - SWA_SPAN), SWA_WINDOW)
    s = _qk(q_ref[...], k_ref[pl.ds(koff, SWA_SPAN), :])
    qpos = q0 + lax.broadcasted_iota(jnp.int32, s.shape, 0)
    kpos = koff + lax.broadcasted_iota(jnp.int32, s.shape, 1)
    s = jnp.where(jnp.abs(qpos - kpos) <= SWA_WINDOW, s, MASKED)
    sink = sink_ref[h]
    m = jnp.maximum(jnp.max(s, axis=-1, keepdims=True), sink)
    p = jnp.exp(s - m)
    l = jnp.sum(p, axis=-1, keepdims=True) + jnp.exp(sink - m)
    v = v_ref[pl.ds(koff, SWA_SPAN), :]
    o_ref[...] = (jnp.dot(p.astype(v.dtype), v, preferred_element_type=F32) / l).astype(o_ref.dtype)


def _swa_attention(z, sinks, batch, seq):
    n = z.shape[0]
    assert seq % SWA_TQ == 0 and seq >= SWA_SPAN
    nq = seq // SWA_TQ
    return pl.pallas_call(
        functools.partial(_swa_kernel, seq=seq),
        grid=(batch, GROUP_HEADS, nq),
        in_specs=[pl.BlockSpec(memory_space=pltpu.SMEM),
                  pl.BlockSpec((SWA_TQ, LANES), lambda b, h, i: (b * nq + i, 24 + h)),
                  pl.BlockSpec((seq, LANES), lambda b, h, i: (b, 28 + h // 2)),
                  pl.BlockSpec((seq, LANES), lambda b, h, i: (b, 30 + h // 2))],
        out_specs=pl.BlockSpec((SWA_TQ, LANES), lambda b, h, i: (b * nq + i, h)),
        out_shape=jax.ShapeDtypeStruct((n, GROUP_WIDTH), BF16),
        compiler_params=_cparams("parallel", "parallel", "arbitrary"),
        name="swa_attention",
    )(sinks, z, z, z)


def _dot_f32(x, w):
    xh = x.astype(BF16)
    xl = (x - xh.astype(F32)).astype(BF16)
    wh = w.astype(BF16)
    wl = (w - wh.astype(F32)).astype(BF16)
    d = lambda a, b: jnp.dot(a, b, preferred_element_type=F32)
    return d(xh, wh) + (d(xh, wl) + d(xl, wh))


def _top2(logits):
    lane = lax.broadcasted_iota(jnp.int32, logits.shape, 1)
    l1 = jnp.where(lane < N_EXPERTS, logits, MASKED)
    m1 = jnp.max(l1, axis=-1, keepdims=True)
    i1 = jnp.min(jnp.where(l1 == m1, lane, LANES), axis=-1, keepdims=True)
    l2 = jnp.where(lane == i1, MASKED, l1)
    m2 = jnp.max(l2, axis=-1, keepdims=True)
    i2 = jnp.min(jnp.where(l2 == m2, lane, LANES), axis=-1, keepdims=True)
    e2 = jnp.exp(m2 - m1)
    den = 1.0 + e2
    out = jnp.where(lane == 0, i1.astype(F32), 0.0)
    out = jnp.where(lane == 1, i2.astype(F32), out)
    out = jnp.where(lane == 2, 1.0 / den, out)
    return jnp.where(lane == 3, e2 / den, out)


def _outproj_kernel(*refs, routed):
    if routed:
        md, mn, ms, ml, w_ref, h_ref, g_ref, wr_ref, h1_ref, hn_ref, route_ref = refs
    else:
        md, mn, ms, ml, w_ref, h_ref, g_ref, h1_ref, hn_ref = refs
    acc = h_ref[...]
    for k, m in enumerate((md, mn, ms, ml)):
        acc = acc + jnp.dot(m[...], w_ref[k * GROUP_WIDTH:(k + 1) * GROUP_WIDTH, :],
                            preferred_element_type=F32)
    h1_ref[...] = acc
    hn = _rms(acc, g_ref[...])
    hn_ref[...] = hn.astype(hn_ref.dtype)
    if routed:
        route_ref[...] = _top2(_dot_f32(hn, wr_ref[...]))


def _outproj(mixed, w, h, g, w_router=None, tm=256):
    n, d = h.shape
    tm = _tile(n, tm)
    routed = w_router is not None
    row = lambda width: pl.BlockSpec((tm, width), lambda i: (i, 0))
    full = lambda a: pl.BlockSpec(a.shape, lambda i: (0, 0))
    g = g.reshape(1, d)
    args = list(mixed) + [w, h, g]
    in_specs = [row(GROUP_WIDTH)] * 4 + [full(w), row(d), full(g)]
    out_specs = [row(d), row(d)]
    out_shape = [jax.ShapeDtypeStruct((n, d), F32), jax.ShapeDtypeStruct((n, d), F32 if routed else BF16)]
    if routed:
        args.append(w_router)
        in_specs.append(full(w_router))
        out_specs.append(row(LANES))
        out_shape.append(jax.ShapeDtypeStruct((n, LANES), F32))
    return pl.pallas_call(
        functools.partial(_outproj_kernel, routed=routed),
        grid=(n // tm,),
        in_specs=in_specs, out_specs=out_specs, out_shape=out_shape,
        compiler_params=_cparams("parallel"),
        name="outproj_routed" if routed else "outproj",
    )(*args)


def _swiglu_step(x, wg_ref, wu_ref, wd_ref, o_ref, f):
    g = jnp.dot(x, wg_ref[...], preferred_element_type=F32)
    u = jnp.dot(x, wu_ref[...], preferred_element_type=F32)
    a = (g * jax.nn.sigmoid(g) * u).astype(BF16)
    y = jnp.dot(a, wd_ref[...], preferred_element_type=F32)

    @pl.when(f == 0)
    def _():
        o_ref[...] = y

    @pl.when(f > 0)
    def _():
        o_ref[...] += y


def _ffn_kernel(x_ref, wg_ref, wu_ref, wd_ref, o_ref):
    _swiglu_step(x_ref[...], wg_ref, wu_ref, wd_ref, o_ref, pl.program_id(1))


def _dense_ffn(x, wg, wu, wd, tm=1024, tf=512):
    n, d = x.shape
    ff = wg.shape[1]
    tm, tf = _tile(n, tm), _tile(ff, tf)
    return pl.pallas_call(
        _ffn_kernel,
        grid=(n // tm, ff // tf),
        in_specs=[pl.BlockSpec((tm, d), lambda i, f: (i, 0)),
                  pl.BlockSpec((d, tf), lambda i, f: (0, f)),
                  pl.BlockSpec((d, tf), lambda i, f: (0, f)),
                  pl.BlockSpec((tf, d), lambda i, f: (f, 0))],
        out_specs=pl.BlockSpec((tm, d), lambda i, f: (i, 0)),
        out_shape=jax.ShapeDtypeStruct((n, d), F32),
        compiler_params=_cparams("parallel", "arbitrary"),
        name="dense_ffn",
    )(x, wg, wu, wd)


def _moe_kernel(blk_exp_ref, n_used_ref, x_ref, wg_ref, wu_ref, wd_ref, o_ref, xb_ref):
    b, f = pl.program_id(0), pl.program_id(1)
    used = b < n_used_ref[0]

    @pl.when(used & (f == 0))
    def _():
        xb_ref[...] = x_ref[...].astype(BF16)

    @pl.when(used)
    def _():
        _swiglu_step(xb_ref[...], wg_ref, wu_ref, wd_ref, o_ref, f)

    @pl.when(jnp.logical_not(used) & (f == 0))
    def _():
        o_ref[...] = jnp.zeros(o_ref.shape, o_ref.dtype)


def _moe_ffn(xs, blk_exp, n_used, wg, wu, wd, tf=512):
    n_slots, d = xs.shape
    ff = wg.shape[2]
    tf = _tile(ff, tf)
    nf = ff // tf
    n_blk = n_slots // MOE_TM

    def blk(b, nu):
        return jnp.minimum(b, nu[0] - 1)

    def col(b, f, nu):
        return jnp.where(b < nu[0], f, nf - 1)

    grid_spec = pltpu.PrefetchScalarGridSpec(
        num_scalar_prefetch=2,
        grid=(n_blk, nf),
        in_specs=[pl.BlockSpec((MOE_TM, d), lambda b, f, be, nu: (blk(b, nu), 0)),
                  pl.BlockSpec((None, d, tf), lambda b, f, be, nu: (be[blk(b, nu)], 0, col(b, f, nu))),
                  pl.BlockSpec((None, d, tf), lambda b, f, be, nu: (be[blk(b, nu)], 0, col(b, f, nu))),
                  pl.BlockSpec((None, tf, d), lambda b, f, be, nu: (be[blk(b, nu)], col(b, f, nu), 0))],
        out_specs=pl.BlockSpec((MOE_TM, d), lambda b, f, be, nu: (b, 0)),
        scratch_shapes=[pltpu.VMEM((MOE_TM, d), BF16)],
    )
    return pl.pallas_call(
        _moe_kernel,
        grid_spec=grid_spec,
        out_shape=jax.ShapeDtypeStruct((n_slots, d), F32),
        compiler_params=_cparams("arbitrary", "arbitrary"),
        name="moe_ffn",
    )(blk_exp, n_used, xs, wg, wu, wd)


def _gather_kernel(idx_ref, src_ref, o_ref, sem):
    base = pl.program_id(0) * GATHER_ROWS

    def copy(r):
        return pltpu.make_async_copy(src_ref.at[pl.ds(idx_ref[0, 0, r], 1)],
                                     o_ref.at[pl.ds(base + r, 1)], sem)

    def start(r, carry):
        copy(r).start()
        return carry

    def wait(r, carry):
        copy(r).wait()
        return carry

    lax.fori_loop(0, GATHER_ROWS, start, 0)
    lax.fori_loop(0, GATHER_ROWS, wait, 0)


def _gather_rows(src, idx):
    n_out = idx.shape[0]
    assert n_out % GATHER_ROWS == 0
    steps = n_out // GATHER_ROWS
    return pl.pallas_call(
        _gather_kernel,
        grid=(steps,),
        in_specs=[pl.BlockSpec((1, 1, GATHER_ROWS), lambda i: (i, 0, 0), memory_space=pltpu.SMEM),
                  pl.BlockSpec(memory_space=pl.ANY)],
        out_specs=pl.BlockSpec(memory_space=pl.ANY),
        out_shape=jax.ShapeDtypeStruct((n_out, src.shape[1]), src.dtype),
        scratch_shapes=[pltpu.SemaphoreType.DMA(())],
        compiler_params=_cparams("arbitrary"),
        name="gather_rows",
    )(idx.reshape(steps, 1, GATHER_ROWS), src)


def _ple_kernel(*refs, routed, last):
    if routed:
        h1_ref, y0_ref, y1_ref, route_ref, p_ref, gp_ref, wg_ref, wp_ref, gn_ref = refs[:9]
        outs = refs[9:]
        route = route_ref[...]
        h2 = h1_ref[...] + (route[:, 2:3] * y0_ref[...] + route[:, 3:4] * y1_ref[...])
    else:
        h1_ref, y_ref, p_ref, gp_ref, wg_ref, wp_ref, gn_ref = refs[:7]
        outs = refs[7:]
        h2 = h1_ref[...] + y_ref[...]
    hp = _rms(h2, gp_ref[...]).astype(BF16)
    gate = jax.nn.sigmoid(jnp.dot(hp, wg_ref[...], preferred_element_type=F32))
    pp = jnp.dot(p_ref[...].astype(BF16), wp_ref[...], preferred_element_type=F32)
    h3 = h2 + gate * pp
    if last:
        outs[0][...] = _rms(h3, gn_ref[...])
    else:
        outs[0][...] = h3
        outs[1][...] = _rms(h3, gn_ref[...]).astype(outs[1].dtype)


def _ple(h1, ys, route, p, g_ple, w_gate, w_proj, g_next, last, tm=256):
    n, d = h1.shape
    tm = _tile(n, tm)
    nt = n // tm
    routed = route is not None
    row = lambda width: pl.BlockSpec((tm, width), lambda i: (i, 0))
    full = lambda a: pl.BlockSpec(a.shape, lambda i: (0, 0))
    g_ple, g_next = g_ple.reshape(1, d), g_next.reshape(1, d)
    if routed:
        args = [h1, ys, ys, route]
        in_specs = [row(d), row(d), pl.BlockSpec((tm, d), lambda i: (i + nt, 0)), row(LANES)]
    else:
        args = [h1, ys]
        in_specs = [row(d), row(d)]
    args += [p, g_ple, w_gate, w_proj, g_next]
    in_specs += [row(p.shape[1]), full(g_ple), full(w_gate), full(w_proj), full(g_next)]
    if last:
        out_specs, out_shape = [row(d)], [jax.ShapeDtypeStruct((n, d), F32)]
    else:
        out_specs = [row(d), row(d)]
        out_shape = [jax.ShapeDtypeStruct((n, d), F32), jax.ShapeDtypeStruct((n, d), BF16)]
    return pl.pallas_call(
        functools.partial(_ple_kernel, routed=routed, last=last),
        grid=(nt,),
        in_specs=in_specs, out_specs=out_specs, out_shape=out_shape,
        compiler_params=_cparams("parallel"),
        name="ple_routed" if routed else "ple",
    )(*args)


def _dispatch(route, n):
    idx = route[:, :2].astype(jnp.int32)
    flat_e = idx.reshape(-1)
    onehot = (flat_e[:, None] == jnp.arange(N_EXPERTS, dtype=jnp.int32)[None, :]).astype(jnp.int32)
    csum = jnp.cumsum(onehot, axis=0)
    rank = jnp.sum(onehot * csum, axis=1) - 1
    counts = csum[-1]
    padded = (counts + MOE_TM - 1) // MOE_TM * MOE_TM
    pad_end = jnp.cumsum(padded)
    pad_start = pad_end - padded
    dest = pad_start[flat_e] + rank
    n_blk = (2 * n) // MOE_TM + N_EXPERTS
    flat_tok = jnp.arange(2 * n, dtype=jnp.int32) // 2
    slot_tok = jnp.zeros((n_blk * MOE_TM,), jnp.int32).at[dest].set(flat_tok)
    blk_exp = jnp.minimum(jnp.searchsorted(pad_end, jnp.arange(n_blk, dtype=jnp.int32) * MOE_TM, side="right"),
                          N_EXPERTS - 1).astype(jnp.int32)
    n_used = (pad_end[-1] // MOE_TM).astype(jnp.int32).reshape(1)
    pos = dest.reshape(n, 2).T.reshape(-1).astype(jnp.int32)
    return slot_tok, blk_exp, n_used, pos


def _mla_weights(w_uq, w_ukv):
    lora = w_uq.shape[0]
    wq = w_uq.reshape(lora, GROUP_HEADS, MLA_NOPE + MLA_ROPE)
    wq = jnp.pad(wq, ((0, 0), (0, 0), (0, 2 * LANES - MLA_NOPE - MLA_ROPE)))
    wkv = w_ukv.reshape(lora, GROUP_HEADS, 2, HEAD_DIM).transpose(0, 2, 1, 3)
    return wq.reshape(lora, -1).astype(BF16), wkv.reshape(lora, -1).astype(BF16)


def kernel(x, p, attn_norm, w_in, diff_lq1, diff_lk1, diff_lq2, diff_lk2, diff_subln, na_rpb, swa_sinks,
           mla_q_norm, mla_kv_norm, mla_w_uq, mla_w_ukv, w_out, ffn_norm, dense_w_gate, dense_w_up,
           dense_w_down, moe_router, moe_w_gate, moe_w_up, moe_w_down, ple_norm, ple_gate, ple_proj,
           final_norm):
    batch, seq, d = x.shape
    n = batch * seq
    depth = w_in.shape[0]
    tabs = _rope_tables(seq)
    h = x.reshape(n, d)
    xn = _rmsnorm(h, attn_norm[0])
    for i in range(depth):
        lambda_init = 0.8 - 0.6 * math.exp(-0.3 * i)
        w_in_i = jnp.pad(w_in[i].astype(BF16), ((0, 0), (0, IN_COLS_PAD - IN_COLS)))
        z = _inproj(xn, w_in_i, tabs, seq)
        wq, wkv = _mla_weights(mla_w_uq[i], mla_w_ukv[i])
        q_l, k_l, v_l = _mla_up(z, mla_q_norm[i], mla_kv_norm[i], wq, wkv, tabs, seq)
        mixed = (
            _diff_attention(z, diff_lq1[i], diff_lk1[i], diff_lq2[i], diff_lk2[i], diff_subln[i],
                            lambda_init, batch, seq),
            _na_attention(z, na_rpb[i], batch, seq),
            _swa_attention(z, swa_sinks[i], batch, seq),
            _mla_attention(q_l, k_l, v_l, batch, seq),
        )
        j = i // 2
        routed = i % 2 == 1
        last = i == depth - 1
        g_next = final_norm if last else attn_norm[i + 1]
        if not routed:
            h1, hn = _outproj(mixed, w_out[i].astype(BF16), h, ffn_norm[i])
            y = _dense_ffn(hn, dense_w_gate[j].astype(BF16), dense_w_up[j].astype(BF16),
                           dense_w_down[j].astype(BF16))
            route = None
        else:
            w_router = jnp.pad(moe_router[j], ((0, 0), (0, LANES - N_EXPERTS)))
            h1, hn, route = _outproj(mixed, w_out[i].astype(BF16), h, ffn_norm[i], w_router)
            slot_tok, blk_exp, n_used, pos = _dispatch(route, n)
            xs = _gather_rows(hn, slot_tok)
            ys = _moe_ffn(xs, blk_exp, n_used, moe_w_gate[j].astype(BF16), moe_w_up[j].astype(BF16),
                          moe_w_down[j].astype(BF16))
            y = _gather_rows(ys, pos)
        outs = _ple(h1, y, route, p[i].reshape(n, -1), ple_norm[i], ple_gate[i].astype(BF16),
                    ple_proj[i].astype(BF16), g_next, last)
        if last:
            return outs[0].reshape(batch, seq, d)
        h, xn = outs
```

```python
import functools
import math

import numpy as np
import jax
import jax.numpy as jnp
from jax import lax
from jax.experimental import pallas as pl
from jax.experimental.pallas import tpu as pltpu

F32 = jnp.float32
BF16 = jnp.bfloat16

D_MODEL = 2048
HEAD_DIM = 128
GROUP_HEADS = 4
GROUP_WIDTH = GROUP_HEADS * HEAD_DIM
ROPE_THETA = 10000.0
NORM_EPS = 1e-6
DIFF_QK_DIM = 64
GRID_W = 64
NA_WIN_ROWS = 8
NA_WIN_COLS = 16
SWA_WINDOW = 128
MLA_LORA = 512
MLA_NOPE = 128
MLA_ROPE = 64
IN_COLS = 5184
FFN_DIM = 7168
N_EXPERTS = 8
PLE_DIM = 256

LANES = 128
VMEM_LIMIT = 56 * 1024 * 1024
MASKED = -1e30

IN_TN = 512
IN_COLS_PAD = 11 * IN_TN
NA_GROUP_ROWS = 4
NA_KEY_ROWS = NA_GROUP_ROWS + NA_WIN_ROWS
ATT_SUB = 256
SWA_TQ = 512
SWA_SPAN = SWA_TQ + 2 * SWA_WINDOW
MOE_TM = 512
GATHER_ROWS = 256


def _tile(n, pref):
    t = min(n, pref)
    assert n % t == 0, (n, t)
    return t


def _cparams(*sem):
    return pltpu.CompilerParams(dimension_semantics=sem, vmem_limit_bytes=VMEM_LIMIT)


def _rms(x, g):
    return x * lax.rsqrt(jnp.mean(x * x, axis=-1, keepdims=True) + NORM_EPS) * g


def _rmsnorm_kernel(x_ref, g_ref, o_ref):
    o_ref[...] = _rms(x_ref[...], g_ref[...]).astype(o_ref.dtype)


def _rmsnorm(x, g, tm=512):
    n, d = x.shape
    tm = _tile(n, tm)
    return pl.pallas_call(
        _rmsnorm_kernel,
        grid=(n // tm,),
        in_specs=[pl.BlockSpec((tm, d), lambda i: (i, 0)),
                  pl.BlockSpec((1, d), lambda i: (0, 0))],
        out_specs=pl.BlockSpec((tm, d), lambda i: (i, 0)),
        out_shape=jax.ShapeDtypeStruct((n, d), BF16),
        compiler_params=_cparams("parallel"),
        name="rmsnorm",
    )(x, g.reshape(1, d))


def _rope_tables(seq):
    def base(dim):
        inv = ROPE_THETA ** (-jnp.arange(0, dim, 2, dtype=F32) / dim)
        ang = jnp.arange(seq, dtype=F32)[:, None] * inv[None, :]
        ang = jnp.concatenate([ang, ang], axis=-1)
        return jnp.cos(ang), jnp.sin(ang)

    lane = np.arange(LANES)
    cos, sin = base(DIFF_QK_DIM)
    c64 = jnp.concatenate([cos, cos], axis=-1)
    s64 = jnp.concatenate([sin, sin], axis=-1)
    lo = jnp.asarray((lane % 64) < 32)
    sa64 = jnp.where(lo, -s64, 0.0)
    sb64 = jnp.where(lo, 0.0, s64)
    first = jnp.asarray(lane < 64)
    c64p, sa64p, sb64p = (jnp.where(first, t, 0.0) for t in (c64, sa64, sb64))
    cos, sin = base(HEAD_DIM)
    s128 = jnp.where(jnp.asarray(lane < 64), -sin, sin)
    return dict(c64=c64, sa64=sa64, sb64=sb64, c64p=c64p, sa64p=sa64p, sb64p=sb64p,
                c128=cos, s128=s128)


def _rope64(x, c, sa, sb):
    return x * c + pltpu.roll(x, 96, 1) * sa + pltpu.roll(x, 32, 1) * sb


def _rope128(x, c, s):
    return x * c + pltpu.roll(x, 64, 1) * s


def _inproj_kernel(x_ref, w_ref, c64, sa64, sb64, c128, s128, o_ref, acc_ref):
    j = pl.program_id(1)
    acc_ref[...] = jnp.dot(x_ref[...], w_ref[...], preferred_element_type=F32)
    groups = IN_TN // LANES

    def seg(g):
        return slice(g * LANES, (g + 1) * LANES)

    def put(g, y):
        o_ref[:, seg(g)] = y.astype(o_ref.dtype)

    @pl.when(j == 0)
    def _():
        for g in range(groups):
            put(g, _rope64(acc_ref[:, seg(g)], c64[...], sa64[...], sb64[...]) * DIFF_QK_DIM ** -0.5)

    @pl.when(j == 1)
    def _():
        for g in range(groups):
            put(g, _rope64(acc_ref[:, seg(g)], c64[...], sa64[...], sb64[...]))

    @pl.when(j == 3)
    def _():
        for g in range(groups):
            put(g, acc_ref[:, seg(g)] * HEAD_DIM ** -0.5)

    @pl.when(j == 6)
    def _():
        for g in range(groups):
            put(g, _rope128(acc_ref[:, seg(g)], c128[...], s128[...]) * HEAD_DIM ** -0.5)

    @pl.when(j == 7)
    def _():
        for g in range(groups):
            x = acc_ref[:, seg(g)]
            put(g, _rope128(x, c128[...], s128[...]) if g < 2 else x)

    plain = (j == 2) | (j == 4) | (j == 5) | (j >= 8)

    @pl.when(plain)
    def _():
        o_ref[...] = acc_ref[...].astype(o_ref.dtype)


def _inproj(xn, w, tabs, seq, tm=1024):
    n, d = xn.shape
    tm = _tile(seq, tm)
    per_seq = seq // tm
    tab_spec = pl.BlockSpec((tm, LANES), lambda i, j: (i % per_seq, 0))
    return pl.pallas_call(
        _inproj_kernel,
        grid=(n // tm, IN_COLS_PAD // IN_TN),
        in_specs=[pl.BlockSpec((tm, d), lambda i, j: (i, 0)),
                  pl.BlockSpec((d, IN_TN), lambda i, j: (0, j))] + [tab_spec] * 5,
        out_specs=pl.BlockSpec((tm, IN_TN), lambda i, j: (i, j)),
        out_shape=jax.ShapeDtypeStruct((n, IN_COLS_PAD), BF16),
        scratch_shapes=[pltpu.VMEM((tm, IN_TN), F32)],
        compiler_params=_cparams("parallel", "arbitrary"),
        name="inproj",
    )(xn, w, tabs["c64"], tabs["sa64"], tabs["sb64"], tabs["c128"], tabs["s128"])


def _mla_up_kernel(cq_ref, ckv_ref, kr_ref, gq_ref, gkv_ref, wq_ref, wkv_ref, c_ref, sa_ref, sb_ref,
                   q_ref, k_ref, v_ref):
    scale = (MLA_NOPE + MLA_ROPE) ** -0.5
    c, sa, sb = c_ref[...], sa_ref[...], sb_ref[...]
    cq = _rms(cq_ref[...].astype(F32), gq_ref[...]).astype(BF16)
    qa = jnp.dot(cq, wq_ref[...], preferred_element_type=F32)
    for h in range(GROUP_HEADS):
        base = h * 2 * LANES
        q_ref[:, base:base + LANES] = (qa[:, base:base + LANES] * scale).astype(q_ref.dtype)
        q_ref[:, base + LANES:base + 2 * LANES] = (
            _rope64(qa[:, base + LANES:base + 2 * LANES], c, sa, sb) * scale).astype(q_ref.dtype)
    ckv = _rms(ckv_ref[...].astype(F32), gkv_ref[...]).astype(BF16)
    kva = jnp.dot(ckv, wkv_ref[...], preferred_element_type=F32)
    kr = _rope64(kr_ref[...].astype(F32), c, sa, sb).astype(k_ref.dtype)
    for h in range(GROUP_HEADS):
        base = h * 2 * LANES
        k_ref[:, base:base + LANES] = kva[:, h * LANES:(h + 1) * LANES].astype(k_ref.dtype)
        k_ref[:, base + LANES:base + 2 * LANES] = kr
    v_ref[...] = kva[:, GROUP_WIDTH:].astype(v_ref.dtype)


def _mla_up(z, gq, gkv, wq, wkv, tabs, seq, tm=512):
    n = z.shape[0]
    tm = _tile(seq, tm)
    per_seq = seq // tm
    tab_spec = pl.BlockSpec((tm, LANES), lambda i: (i % per_seq, 0))
    full = lambda a: pl.BlockSpec(a.shape, lambda i: (0,) * a.ndim)
    gq = gq.reshape(1, -1)
    gkv = gkv.reshape(1, -1)
    return pl.pallas_call(
        _mla_up_kernel,
        grid=(n // tm,),
        in_specs=[pl.BlockSpec((tm, MLA_LORA), lambda i: (i, 8)),
                  pl.BlockSpec((tm, MLA_LORA), lambda i: (i, 9)),
                  pl.BlockSpec((tm, LANES), lambda i: (i, 40)),
                  full(gq), full(gkv), full(wq), full(wkv), tab_spec, tab_spec, tab_spec],
        out_specs=[pl.BlockSpec((tm, 2 * GROUP_WIDTH), lambda i: (i, 0)),
                   pl.BlockSpec((tm, 2 * GROUP_WIDTH), lambda i: (i, 0)),
                   pl.BlockSpec((tm, GROUP_WIDTH), lambda i: (i, 0))],
        out_shape=[jax.ShapeDtypeStruct((n, 2 * GROUP_WIDTH), BF16),
                   jax.ShapeDtypeStruct((n, 2 * GROUP_WIDTH), BF16),
                   jax.ShapeDtypeStruct((n, GROUP_WIDTH), BF16)],
        compiler_params=_cparams("parallel"),
        name="mla_up",
    )(z, z, z, gq, gkv, wq, wkv, tabs["c64p"], tabs["sa64p"], tabs["sb64p"])


def _qk(q, k):
    return lax.dot_general(q, k, (((1,), (1,)), ((), ())), preferred_element_type=F32)


def _flash_core(q_ref, k_ref, v_ref, m_ref, l_ref, acc_ref, tk):
    m_ref[...] = jnp.full(m_ref.shape, MASKED, F32)
    l_ref[...] = jnp.zeros(l_ref.shape, F32)
    acc_ref[...] = jnp.zeros(acc_ref.shape, F32)

    def body(t, carry):
        off = pl.multiple_of(t * tk, tk)
        k = k_ref[pl.ds(off, tk), :]
        v = v_ref[pl.ds(off, tk), :]
        for c in range(q_ref.shape[0] // ATT_SUB):
            rows = slice(c * ATT_SUB, (c + 1) * ATT_SUB)
            s = _qk(q_ref[rows, :], k)
            m_prev = m_ref[rows, :]
            m_new = jnp.maximum(m_prev, jnp.max(s, axis=-1, keepdims=True))
            alpha = jnp.exp(m_prev - m_new)
            p = jnp.exp(s - m_new)
            l_ref[rows, :] = alpha * l_ref[rows, :] + jnp.sum(p, axis=-1, keepdims=True)
            acc_ref[rows, :] = alpha * acc_ref[rows, :] + jnp.dot(p.astype(v.dtype), v,
                                                                  preferred_element_type=F32)
            m_ref[rows, :] = m_new
        return carry

    lax.fori_loop(0, k_ref.shape[0] // tk, body, 0)


def _flash_kernel(q_ref, k_ref, v_ref, o_ref, m_ref, l_ref, acc_ref, *, tk):
    _flash_core(q_ref, k_ref, v_ref, m_ref, l_ref, acc_ref, tk)
    o_ref[...] = (acc_ref[...] / l_ref[...]).astype(o_ref.dtype)


def _mla_attention(q, k, v, batch, seq, tq=512, tk=512):
    n = q.shape[0]
    tq, tk = _tile(seq, tq), _tile(seq, tk)
    nq = seq // tq
    dq = 2 * LANES
    return pl.pallas_call(
        functools.partial(_flash_kernel, tk=tk),
        grid=(batch, GROUP_HEADS, nq),
        in_specs=[pl.BlockSpec((tq, dq), lambda b, h, i: (b * nq + i, h)),
                  pl.BlockSpec((seq, dq), lambda b, h, i: (b, h)),
                  pl.BlockSpec((seq, LANES), lambda b, h, i: (b, h))],
        out_specs=pl.BlockSpec((tq, LANES), lambda b, h, i: (b * nq + i, h)),
        out_shape=jax.ShapeDtypeStruct((n, GROUP_WIDTH), BF16),
        scratch_shapes=[pltpu.VMEM((tq, 1), F32), pltpu.VMEM((tq, 1), F32), pltpu.VMEM((tq, LANES), F32)],
        compiler_params=_cparams("parallel", "parallel", "arbitrary"),
        name="mla_attention",
    )(q, k, v)


def _diff_kernel(lq1_ref, lk1_ref, lq2_ref, lk2_ref, g_ref, q_ref, k_ref, v_ref, o_ref,
                 qq_ref, m_ref, l_ref, acc_ref, *, tk, lambda_init):
    tq = q_ref.shape[0]
    q = q_ref[...]
    lane = lax.broadcasted_iota(jnp.int32, q.shape, 1)
    qq_ref[:tq, :] = jnp.where(lane < DIFF_QK_DIM, q, jnp.zeros_like(q))
    qq_ref[tq:, :] = jnp.where(lane >= DIFF_QK_DIM, q, jnp.zeros_like(q))
    _flash_core(qq_ref, k_ref, v_ref, m_ref, l_ref, acc_ref, tk)
    lam = (jnp.exp(jnp.sum(lq1_ref[...] * lk1_ref[...], axis=-1, keepdims=True))
           - jnp.exp(jnp.sum(lq2_ref[...] * lk2_ref[...], axis=-1, keepdims=True)) + lambda_init)
    o = acc_ref[:tq, :] / l_ref[:tq, :] - lam * (acc_ref[tq:, :] / l_ref[tq:, :])
    o_ref[...] = (_rms(o, g_ref[...]) * (1.0 - lambda_init)).astype(o_ref.dtype)


def _diff_attention(z, lq1, lk1, lq2, lk2, subln, lambda_init, batch, seq, tq=512, tk=512):
    n = z.shape[0]
    tq, tk = _tile(seq, tq), _tile(seq, tk)
    nq = seq // tq
    vec = lambda a: a.reshape(1, -1)
    small = lambda a: pl.BlockSpec(a.shape, lambda b, h, i: (0, 0))
    params = [vec(a) for a in (lq1, lk1, lq2, lk2, subln)]
    vm = lambda w: pltpu.VMEM((2 * tq, w), F32)
    return pl.pallas_call(
        functools.partial(_diff_kernel, tk=tk, lambda_init=lambda_init),
        grid=(batch, GROUP_HEADS, nq),
        in_specs=[small(a) for a in params] + [
            pl.BlockSpec((tq, LANES), lambda b, h, i: (b * nq + i, h)),
            pl.BlockSpec((seq, LANES), lambda b, h, i: (b, 4 + h)),
            pl.BlockSpec((seq, LANES), lambda b, h, i: (b, 8 + h))],
        out_specs=pl.BlockSpec((tq, LANES), lambda b, h, i: (b * nq + i, h)),
        out_shape=jax.ShapeDtypeStruct((n, GROUP_WIDTH), BF16),
        scratch_shapes=[pltpu.VMEM((2 * tq, LANES), BF16), vm(1), vm(1), vm(LANES)],
        compiler_params=_cparams("parallel", "parallel", "arbitrary"),
        name="diff_attention",
    )(*params, z, z, z)


def _na_bias_tables(rpb, rows):
    groups = rows // NA_GROUP_ROWS
    assert groups >= 3
    qc = np.arange(GRID_W)
    kc = np.arange(GRID_W)
    ws = np.clip(qc - NA_WIN_COLS // 2, 0, GRID_W - NA_WIN_COLS)
    col_ok = (kc[None, :] >= ws[:, None]) & (kc[None, :] < ws[:, None] + NA_WIN_COLS)
    edge = GRID_W - NA_WIN_COLS
    rp = jnp.pad(rpb.astype(F32), ((0, 0), (0, 0), (edge, edge)))
    toep = jnp.stack([rp[:, :, GRID_W - 1 - c:2 * GRID_W - 1 - c] for c in range(GRID_W)], axis=2)
    toep = jnp.where(jnp.asarray(col_ok)[None, None], toep, MASKED)
    masked = jnp.full((rpb.shape[0], GRID_W, GRID_W), MASKED, F32)
    tabs = []
    for m in (0, 1, groups - 1):
        kstart = np.clip(NA_GROUP_ROWS * m - NA_WIN_ROWS // 2, 0, rows - NA_KEY_ROWS)
        qr = NA_GROUP_ROWS * m + np.arange(NA_GROUP_ROWS)
        kr = kstart + np.arange(NA_KEY_ROWS)
        rs = np.clip(qr - NA_WIN_ROWS // 2, 0, rows - NA_WIN_ROWS)
        row_ok = (kr[None, :] >= rs[:, None]) & (kr[None, :] < rs[:, None] + NA_WIN_ROWS)
        ridx = kr[None, :] - qr[:, None] + NA_WIN_ROWS - 1
        tabs.append(jnp.concatenate([
            jnp.concatenate([toep[:, ridx[i, a]] if row_ok[i, a] else masked
                             for a in range(NA_KEY_ROWS)], axis=-1)
            for i in range(NA_GROUP_ROWS)], axis=1))
    return jnp.stack(tabs)


def _na_kernel(bias_ref, q_ref, k_ref, v_ref, o_ref, *, rows):
    groups = rows // NA_GROUP_ROWS
    tq = NA_GROUP_ROWS * GRID_W
    span = NA_KEY_ROWS * GRID_W

    def body(m, carry):
        cls = jnp.where(m == 0, 0, jnp.where(m == groups - 1, 2, 1))
        krow = jnp.clip(NA_GROUP_ROWS * m - NA_WIN_ROWS // 2, 0, rows - NA_KEY_ROWS)
        koff = pl.multiple_of(krow * GRID_W, GRID_W)
        qoff = pl.multiple_of(m * tq, tq)
        s = _qk(q_ref[pl.ds(qoff, tq), :], k_ref[pl.ds(koff, span), :]) + bias_ref[cls, 0]
        p = jnp.exp(s - jnp.max(s, axis=-1, keepdims=True))
        l = jnp.sum(p, axis=-1, keepdims=True)
        v = v_ref[pl.ds(koff, span), :]
        o = jnp.dot(p.astype(v.dtype), v, preferred_element_type=F32) / l
        o_ref[pl.ds(qoff, tq), :] = o.astype(o_ref.dtype)
        return carry

    lax.fori_loop(0, groups, body, 0)


def _na_attention(z, rpb, batch, seq):
    n = z.shape[0]
    rows = seq // GRID_W
    bias = _na_bias_tables(rpb, rows)
    tq, span = NA_GROUP_ROWS * GRID_W, NA_KEY_ROWS * GRID_W
    return pl.pallas_call(
        functools.partial(_na_kernel, rows=rows),
        grid=(batch, GROUP_HEADS),
        in_specs=[pl.BlockSpec((3, 1, tq, span), lambda b, h: (0, h, 0, 0)),
                  pl.BlockSpec((seq, LANES), lambda b, h: (b, 12 + h)),
                  pl.BlockSpec((seq, LANES), lambda b, h: (b, 16 + h)),
                  pl.BlockSpec((seq, LANES), lambda b, h: (b, 20 + h))],
        out_specs=pl.BlockSpec((seq, LANES), lambda b, h: (b, h)),
        out_shape=jax.ShapeDtypeStruct((n, GROUP_WIDTH), BF16),
        compiler_params=_cparams("parallel", "parallel"),
        name="na_attention",
    )(bias, z, z, z)


def _swa_kernel(sink_ref, q_ref, k_ref, v_ref, o_ref, *, seq):
    h = pl.program_id(1)
    q0 = pl.program_id(2) * SWA_TQ
    koff = pl.multiple_of(jnp.clip(q0 - SWA_WINDOW, 0, seq - SWA_SPAN), SWA_WINDOW)
    s = _qk(q_ref[...], k_ref[pl.ds(koff, SWA_SPAN), :])
    qpos = q0 + lax.broadcasted_iota(jnp.int32, s.shape, 0)
    kpos = koff + lax.broadcasted_iota(jnp.int32, s.shape, 1)
    s = jnp.where(jnp.abs(qpos - kpos) <= SWA_WINDOW, s, MASKED)
    sink = sink_ref[h]
    m = jnp.maximum(jnp.max(s, axis=-1, keepdims=True), sink)
    p = jnp.exp(s - m)
    l = jnp.sum(p, axis=-1, keepdims=True) + jnp.exp(sink - m)
    v = v_ref[pl.ds(koff, SWA_SPAN), :]
    o_ref[...] = (jnp.dot(p.astype(v.dtype), v, preferred_element_type=F32) / l).astype(o_ref.dtype)


def _swa_attention(z, sinks, batch, seq):
    n = z.shape[0]
    assert seq % SWA_TQ == 0 and seq >= SWA_SPAN
    nq = seq // SWA_TQ
    return pl.pallas_call(
        functools.partial(_swa_kernel, seq=seq),
        grid=(batch, GROUP_HEADS, nq),
        in_specs=[pl.BlockSpec(memory_space=pltpu.SMEM),
                  pl.BlockSpec((SWA_TQ, LANES), lambda b, h, i: (b * nq + i, 24 + h)),
                  pl.BlockSpec((seq, LANES), lambda b, h, i: (b, 28 + h // 2)),
                  pl.BlockSpec((seq, LANES), lambda b, h, i: (b, 30 + h // 2))],
        out_specs=pl.BlockSpec((SWA_TQ, LANES), lambda b, h, i: (b * nq + i, h)),
        out_shape=jax.ShapeDtypeStruct((n, GROUP_WIDTH), BF16),
        compiler_params=_cparams("parallel", "parallel", "arbitrary"),
        name="swa_attention",
    )(sinks, z, z, z)


def _dot_f32(x, w):
    xh = x.astype(BF16)
    xl = (x - xh.astype(F32)).astype(BF16)
    wh = w.astype(BF16)
    wl = (w - wh.astype(F32)).astype(BF16)
    d = lambda a, b: jnp.dot(a, b, preferred_element_type=F32)
    return d(xh, wh) + (d(xh, wl) + d(xl, wh))


def _top2(logits):
    lane = lax.broadcasted_iota(jnp.int32, logits.shape, 1)
    l1 = jnp.where(lane < N_EXPERTS, logits, MASKED)
    m1 = jnp.max(l1, axis=-1, keepdims=True)
    i1 = jnp.min(jnp.where(l1 == m1, lane, LANES), axis=-1, keepdims=True)
    l2 = jnp.where(lane == i1, MASKED, l1)
    m2 = jnp.max(l2, axis=-1, keepdims=True)
    i2 = jnp.min(jnp.where(l2 == m2, lane, LANES), axis=-1, keepdims=True)
    e2 = jnp.exp(m2 - m1)
    den = 1.0 + e2
    out = jnp.where(lane == 0, i1.astype(F32), 0.0)
    out = jnp.where(lane == 1, i2.astype(F32), out)
    out = jnp.where(lane == 2, 1.0 / den, out)
    return jnp.where(lane == 3, e2 / den, out)


def _outproj_kernel(*refs, routed):
    if routed:
        md, mn, ms, ml, w_ref, h_ref, g_ref, wr_ref, h1_ref, hn_ref, route_ref = refs
    else:
        md, mn, ms, ml, w_ref, h_ref, g_ref, h1_ref, hn_ref = refs
    acc = h_ref[...]
    for k, m in enumerate((md, mn, ms, ml)):
        acc = acc + jnp.dot(m[...], w_ref[k * GROUP_WIDTH:(k + 1) * GROUP_WIDTH, :],
                            preferred_element_type=F32)
    h1_ref[...] = acc
    hn = _rms(acc, g_ref[...])
    hn_ref[...] = hn.astype(hn_ref.dtype)
    if routed:
        route_ref[...] = _top2(_dot_f32(hn, wr_ref[...]))


def _outproj(mixed, w, h, g, w_router=None, tm=256):
    n, d = h.shape
    tm = _tile(n, tm)
    routed = w_router is not None
    row = lambda width: pl.BlockSpec((tm, width), lambda i: (i, 0))
    full = lambda a: pl.BlockSpec(a.shape, lambda i: (0, 0))
    g = g.reshape(1, d)
    args = list(mixed) + [w, h, g]
    in_specs = [row(GROUP_WIDTH)] * 4 + [full(w), row(d), full(g)]
    out_specs = [row(d), row(d)]
    out_shape = [jax.ShapeDtypeStruct((n, d), F32), jax.ShapeDtypeStruct((n, d), F32 if routed else BF16)]
    if routed:
        args.append(w_router)
        in_specs.append(full(w_router))
        out_specs.append(row(LANES))
        out_shape.append(jax.ShapeDtypeStruct((n, LANES), F32))
    return pl.pallas_call(
        functools.partial(_outproj_kernel, routed=routed),
        grid=(n // tm,),
        in_specs=in_specs, out_specs=out_specs, out_shape=out_shape,
        compiler_params=_cparams("parallel"),
        name="outproj_routed" if routed else "outproj",
    )(*args)


def _swiglu_step(x, wg_ref, wu_ref, wd_ref, o_ref, f):
    g = jnp.dot(x, wg_ref[...], preferred_element_type=F32)
    u = jnp.dot(x, wu_ref[...], preferred_element_type=F32)
    a = (g * jax.nn.sigmoid(g) * u).astype(BF16)
    y = jnp.dot(a, wd_ref[...], preferred_element_type=F32)

    @pl.when(f == 0)
    def _():
        o_ref[...] = y

    @pl.when(f > 0)
    def _():
        o_ref[...] += y


def _ffn_kernel(x_ref, wg_ref, wu_ref, wd_ref, o_ref):
    _swiglu_step(x_ref[...], wg_ref, wu_ref, wd_ref, o_ref, pl.program_id(1))


def _dense_ffn(x, wg, wu, wd, tm=1024, tf=512):
    n, d = x.shape
    ff = wg.shape[1]
    tm, tf = _tile(n, tm), _tile(ff, tf)
    return pl.pallas_call(
        _ffn_kernel,
        grid=(n // tm, ff // tf),
        in_specs=[pl.BlockSpec((tm, d), lambda i, f: (i, 0)),
                  pl.BlockSpec((d, tf), lambda i, f: (0, f)),
                  pl.BlockSpec((d, tf), lambda i, f: (0, f)),
                  pl.BlockSpec((tf, d), lambda i, f: (f, 0))],
        out_specs=pl.BlockSpec((tm, d), lambda i, f: (i, 0)),
        out_shape=jax.ShapeDtypeStruct((n, d), F32),
        compiler_params=_cparams("parallel", "arbitrary"),
        name="dense_ffn",
    )(x, wg, wu, wd)


def _moe_kernel(blk_exp_ref, n_used_ref, x_ref, wg_ref, wu_ref, wd_ref, o_ref, xb_ref):
    b, f = pl.program_id(0), pl.program_id(1)
    used = b < n_used_ref[0]

    @pl.when(used & (f == 0))
    def _():
        xb_ref[...] = x_ref[...].astype(BF16)

    @pl.when(used)
    def _():
        _swiglu_step(xb_ref[...], wg_ref, wu_ref, wd_ref, o_ref, f)

    @pl.when(jnp.logical_not(used) & (f == 0))
    def _():
        o_ref[...] = jnp.zeros(o_ref.shape, o_ref.dtype)


def _moe_ffn(xs, blk_exp, n_used, wg, wu, wd, tf=512):
    n_slots, d = xs.shape
    ff = wg.shape[2]
    tf = _tile(ff, tf)
    nf = ff // tf
    n_blk = n_slots // MOE_TM

    def blk(b, nu):
        return jnp.minimum(b, nu[0] - 1)

    def col(b, f, nu):
        return jnp.where(b < nu[0], f, nf - 1)

    grid_spec = pltpu.PrefetchScalarGridSpec(
        num_scalar_prefetch=2,
        grid=(n_blk, nf),
        in_specs=[pl.BlockSpec((MOE_TM, d), lambda b, f, be, nu: (blk(b, nu), 0)),
                  pl.BlockSpec((None, d, tf), lambda b, f, be, nu: (be[blk(b, nu)], 0, col(b, f, nu))),
                  pl.BlockSpec((None, d, tf), lambda b, f, be, nu: (be[blk(b, nu)], 0, col(b, f, nu))),
                  pl.BlockSpec((None, tf, d), lambda b, f, be, nu: (be[blk(b, nu)], col(b, f, nu), 0))],
        out_specs=pl.BlockSpec((MOE_TM, d), lambda b, f, be, nu: (b, 0)),
        scratch_shapes=[pltpu.VMEM((MOE_TM, d), BF16)],
    )
    return pl.pallas_call(
        _moe_kernel,
        grid_spec=grid_spec,
        out_shape=jax.ShapeDtypeStruct((n_slots, d), F32),
        compiler_params=_cparams("arbitrary", "arbitrary"),
        name="moe_ffn",
    )(blk_exp, n_used, xs, wg, wu, wd)


def _gather_kernel(idx_ref, src_ref, o_ref, sem):
    base = pl.program_id(0) * GATHER_ROWS

    def copy(r):
        return pltpu.make_async_copy(src_ref.at[pl.ds(idx_ref[0, 0, r], 1)],
                                     o_ref.at[pl.ds(base + r, 1)], sem)

    def start(r, carry):
        copy(r).start()
        return carry

    def wait(r, carry):
        copy(r).wait()
        return carry

    lax.fori_loop(0, GATHER_ROWS, start, 0, unroll=8)
    lax.fori_loop(0, GATHER_ROWS, wait, 0, unroll=8)


def _gather_rows(src, idx):
    n_out = idx.shape[0]
    width = src.shape[1]
    assert n_out % GATHER_ROWS == 0 and width % LANES == 0
    steps = n_out // GATHER_ROWS
    src3 = src.reshape(src.shape[0], width // LANES, LANES)
    out3 = pl.pallas_call(
        _gather_kernel,
        grid=(steps,),
        in_specs=[pl.BlockSpec((1, 1, GATHER_ROWS), lambda i: (i, 0, 0), memory_space=pltpu.SMEM),
                  pl.BlockSpec(memory_space=pl.ANY)],
        out_specs=pl.BlockSpec(memory_space=pl.ANY),
        out_shape=jax.ShapeDtypeStruct((n_out,) + src3.shape[1:], src.dtype),
        scratch_shapes=[pltpu.SemaphoreType.DMA(())],
        compiler_params=_cparams("arbitrary"),
        name="gather_rows",
    )(idx.reshape(steps, 1, GATHER_ROWS), src3)
    return out3.reshape(n_out, width)


def _ple_kernel(*refs, routed, last):
    if routed:
        h1_ref, y0_ref, y1_ref, route_ref, p_ref, gp_ref, wg_ref, wp_ref, gn_ref = refs[:9]
        outs = refs[9:]
        route = route_ref[...]
        h2 = h1_ref[...] + (route[:, 2:3] * y0_ref[...] + route[:, 3:4] * y1_ref[...])
    else:
        h1_ref, y_ref, p_ref, gp_ref, wg_ref, wp_ref, gn_ref = refs[:7]
        outs = refs[7:]
        h2 = h1_ref[...] + y_ref[...]
    hp = _rms(h2, gp_ref[...]).astype(BF16)
    gate = jax.nn.sigmoid(jnp.dot(hp, wg_ref[...], preferred_element_type=F32))
    pp = jnp.dot(p_ref[...].astype(BF16), wp_ref[...], preferred_element_type=F32)
    h3 = h2 + gate * pp
    if last:
        outs[0][...] = _rms(h3, gn_ref[...])
    else:
        outs[0][...] = h3
        outs[1][...] = _rms(h3, gn_ref[...]).astype(outs[1].dtype)


def _ple(h1, ys, route, p, g_ple, w_gate, w_proj, g_next, last, tm=256):
    n, d = h1.shape
    tm = _tile(n, tm)
    nt = n // tm
    routed = route is not None
    row = lambda width: pl.BlockSpec((tm, width), lambda i: (i, 0))
    full = lambda a: pl.BlockSpec(a.shape, lambda i: (0, 0))
    g_ple, g_next = g_ple.reshape(1, d), g_next.reshape(1, d)
    if routed:
        args = [h1, ys, ys, route]
        in_specs = [row(d), row(d), pl.BlockSpec((tm, d), lambda i: (i + nt, 0)), row(LANES)]
    else:
        args = [h1, ys]
        in_specs = [row(d), row(d)]
    args += [p, g_ple, w_gate, w_proj, g_next]
    in_specs += [row(p.shape[1]), full(g_ple), full(w_gate), full(w_proj), full(g_next)]
    if last:
        out_specs, out_shape = [row(d)], [jax.ShapeDtypeStruct((n, d), F32)]
    else:
        out_specs = [row(d), row(d)]
        out_shape = [jax.ShapeDtypeStruct((n, d), F32), jax.ShapeDtypeStruct((n, d), BF16)]
    return pl.pallas_call(
        functools.partial(_ple_kernel, routed=routed, last=last),
        grid=(nt,),
        in_specs=in_specs, out_specs=out_specs, out_shape=out_shape,
        compiler_params=_cparams("parallel"),
        name="ple_routed" if routed else "ple",
    )(*args)


def _dispatch(route, n):
    idx = route[:, :2].astype(jnp.int32)
    flat_e = idx.reshape(-1)
    onehot = (flat_e[:, None] == jnp.arange(N_EXPERTS, dtype=jnp.int32)[None, :]).astype(jnp.int32)
    csum = jnp.cumsum(onehot, axis=0)
    rank = jnp.sum(onehot * csum, axis=1) - 1
    counts = csum[-1]
    padded = (counts + MOE_TM - 1) // MOE_TM * MOE_TM
    pad_end = jnp.cumsum(padded)
    pad_start = pad_end - padded
    dest = pad_start[flat_e] + rank
    n_blk = (2 * n) // MOE_TM + N_EXPERTS
    flat_tok = jnp.arange(2 * n, dtype=jnp.int32) // 2
    slot_tok = jnp.zeros((n_blk * MOE_TM,), jnp.int32).at[dest].set(flat_tok)
    blk_exp = jnp.minimum(jnp.searchsorted(pad_end, jnp.arange(n_blk, dtype=jnp.int32) * MOE_TM, side="right"),
                          N_EXPERTS - 1).astype(jnp.int32)
    n_used = (pad_end[-1] // MOE_TM).astype(jnp.int32).reshape(1)
    pos = dest.reshape(n, 2).T.reshape(-1).astype(jnp.int32)
    return slot_tok, blk_exp, n_used, pos


def _mla_weights(w_uq, w_ukv):
    lora = w_uq.shape[0]
    wq = w_uq.reshape(lora, GROUP_HEADS, MLA_NOPE + MLA_ROPE)
    wq = jnp.pad(wq, ((0, 0), (0, 0), (0, 2 * LANES - MLA_NOPE - MLA_ROPE)))
    wkv = w_ukv.reshape(lora, GROUP_HEADS, 2, HEAD_DIM).transpose(0, 2, 1, 3)
    return wq.reshape(lora, -1).astype(BF16), wkv.reshape(lora, -1).astype(BF16)


def kernel(x, p, attn_norm, w_in, diff_lq1, diff_lk1, diff_lq2, diff_lk2, diff_subln, na_rpb, swa_sinks,
           mla_q_norm, mla_kv_norm, mla_w_uq, mla_w_ukv, w_out, ffn_norm, dense_w_gate, dense_w_up,
           dense_w_down, moe_router, moe_w_gate, moe_w_up, moe_w_down, ple_norm, ple_gate, ple_proj,
           final_norm):
    batch, seq, d = x.shape
    n = batch * seq
    depth = w_in.shape[0]
    tabs = _rope_tables(seq)
    h = x.reshape(n, d)
    xn = _rmsnorm(h, attn_norm[0])
    for i in range(depth):
        lambda_init = 0.8 - 0.6 * math.exp(-0.3 * i)
        w_in_i = jnp.pad(w_in[i].astype(BF16), ((0, 0), (0, IN_COLS_PAD - IN_COLS)))
        z = _inproj(xn, w_in_i, tabs, seq)
        wq, wkv = _mla_weights(mla_w_uq[i], mla_w_ukv[i])
        q_l, k_l, v_l = _mla_up(z, mla_q_norm[i], mla_kv_norm[i], wq, wkv, tabs, seq)
        mixed = (
            _diff_attention(z, diff_lq1[i], diff_lk1[i], diff_lq2[i], diff_lk2[i], diff_subln[i],
                            lambda_init, batch, seq),
            _na_attention(z, na_rpb[i], batch, seq),
            _swa_attention(z, swa_sinks[i], batch, seq),
            _mla_attention(q_l, k_l, v_l, batch, seq),
        )
        j = i // 2
        routed = i % 2 == 1
        last = i == depth - 1
        g_next = final_norm if last else attn_norm[i + 1]
        if not routed:
            h1, hn = _outproj(mixed, w_out[i].astype(BF16), h, ffn_norm[i])
            y = _dense_ffn(hn, dense_w_gate[j].astype(BF16), dense_w_up[j].astype(BF16),
                           dense_w_down[j].astype(BF16))
            route = None
        else:
            w_router = jnp.pad(moe_router[j], ((0, 0), (0, LANES - N_EXPERTS)))
            h1, hn, route = _outproj(mixed, w_out[i].astype(BF16), h, ffn_norm[i], w_router)
            slot_tok, blk_exp, n_used, pos = _dispatch(route, n)
            xs = _gather_rows(hn, slot_tok)
            ys = _moe_ffn(xs, blk_exp, n_used, moe_w_gate[j].astype(BF16), moe_w_up[j].astype(BF16),
                          moe_w_down[j].astype(BF16))
            y = _gather_rows(ys, pos)
        outs = _ple(h1, y, route, p[i].reshape(n, -1), ple_norm[i], ple_gate[i].astype(BF16),
                    ple_proj[i].astype(BF16), g_next, last)
        if last:
            return outs[0].reshape(batch, seq, d)
        h, xn = outs
```

```python
import functools
import math

import numpy as np
import jax
import jax.numpy as jnp
from jax import lax
from jax.experimental import pallas as pl
from jax.experimental.pallas import tpu as pltpu

F32 = jnp.float32
BF16 = jnp.bfloat16

D_MODEL = 2048
HEAD_DIM = 128
GROUP_HEADS = 4
GROUP_WIDTH = GROUP_HEADS * HEAD_DIM
ROPE_THETA = 10000.0
NORM_EPS = 1e-6
DIFF_QK_DIM = 64
GRID_W = 64
NA_WIN_ROWS = 8
NA_WIN_COLS = 16
SWA_WINDOW = 128
MLA_LORA = 512
MLA_NOPE = 128
MLA_ROPE = 64
IN_COLS = 5184
FFN_DIM = 7168
N_EXPERTS = 8
PLE_DIM = 256

LANES = 128
VMEM_LIMIT = 56 * 1024 * 1024
MASKED = -1e30

IN_TN = 512
IN_COLS_PAD = 11 * IN_TN
NA_GROUP_ROWS = 4
NA_KEY_ROWS = NA_GROUP_ROWS + NA_WIN_ROWS
ATT_SUB = 256
SWA_TQ = 512
SWA_SPAN = SWA_TQ + 2 * SWA_WINDOW
MOE_TM = 512
GATHER_ROWS = 512


def _tile(n, pref):
    t = min(n, pref)
    assert n % t == 0, (n, t)
    return t


def _cparams(*sem):
    return pltpu.CompilerParams(dimension_semantics=sem, vmem_limit_bytes=VMEM_LIMIT)


def _rms(x, g):
    return x * lax.rsqrt(jnp.mean(x * x, axis=-1, keepdims=True) + NORM_EPS) * g


def _slab_shape(rows, width):
    return jax.ShapeDtypeStruct((rows * (width // LANES), LANES), F32)


def _slab_spec(tm, width, index_map):
    return pl.BlockSpec((tm * (width // LANES), LANES), index_map)


def _slab_chunk(ref, c, tm, width):
    return ref[pl.ds(c, tm, stride=width // LANES), :]


def _slab_store(ref, x):
    tm, width = x.shape
    for c in range(width // LANES):
        ref[pl.ds(c, tm, stride=width // LANES), :] = x[:, c * LANES:(c + 1) * LANES]


def _rmsnorm_kernel(x_ref, g_ref, o_ref):
    o_ref[...] = _rms(x_ref[...], g_ref[...]).astype(o_ref.dtype)


def _rmsnorm(x, g, tm=512):
    n, d = x.shape
    tm = _tile(n, tm)
    return pl.pallas_call(
        _rmsnorm_kernel,
        grid=(n // tm,),
        in_specs=[pl.BlockSpec((tm, d), lambda i: (i, 0)),
                  pl.BlockSpec((1, d), lambda i: (0, 0))],
        out_specs=pl.BlockSpec((tm, d), lambda i: (i, 0)),
        out_shape=jax.ShapeDtypeStruct((n, d), BF16),
        compiler_params=_cparams("parallel"),
        name="rmsnorm",
    )(x, g.reshape(1, d))


def _rope_tables(seq):
    def base(dim):
        inv = ROPE_THETA ** (-jnp.arange(0, dim, 2, dtype=F32) / dim)
        ang = jnp.arange(seq, dtype=F32)[:, None] * inv[None, :]
        ang = jnp.concatenate([ang, ang], axis=-1)
        return jnp.cos(ang), jnp.sin(ang)

    lane = np.arange(LANES)
    cos, sin = base(DIFF_QK_DIM)
    c64 = jnp.concatenate([cos, cos], axis=-1)
    s64 = jnp.concatenate([sin, sin], axis=-1)
    lo = jnp.asarray((lane % 64) < 32)
    sa64 = jnp.where(lo, -s64, 0.0)
    sb64 = jnp.where(lo, 0.0, s64)
    first = jnp.asarray(lane < 64)
    c64p, sa64p, sb64p = (jnp.where(first, t, 0.0) for t in (c64, sa64, sb64))
    cos, sin = base(HEAD_DIM)
    s128 = jnp.where(jnp.asarray(lane < 64), -sin, sin)
    return dict(c64=c64, sa64=sa64, sb64=sb64, c64p=c64p, sa64p=sa64p, sb64p=sb64p,
                c128=cos, s128=s128)


def _rope64(x, c, sa, sb):
    return x * c + pltpu.roll(x, 96, 1) * sa + pltpu.roll(x, 32, 1) * sb


def _rope128(x, c, s):
    return x * c + pltpu.roll(x, 64, 1) * s


def _inproj_kernel(x_ref, w_ref, c64, sa64, sb64, c128, s128, o_ref, acc_ref):
    j = pl.program_id(1)
    acc_ref[...] = jnp.dot(x_ref[...], w_ref[...], preferred_element_type=F32)
    groups = IN_TN // LANES

    def seg(g):
        return slice(g * LANES, (g + 1) * LANES)

    def put(g, y):
        o_ref[:, seg(g)] = y.astype(o_ref.dtype)

    @pl.when(j == 0)
    def _():
        for g in range(groups):
            put(g, _rope64(acc_ref[:, seg(g)], c64[...], sa64[...], sb64[...]) * DIFF_QK_DIM ** -0.5)

    @pl.when(j == 1)
    def _():
        for g in range(groups):
            put(g, _rope64(acc_ref[:, seg(g)], c64[...], sa64[...], sb64[...]))

    @pl.when(j == 3)
    def _():
        for g in range(groups):
            put(g, acc_ref[:, seg(g)] * HEAD_DIM ** -0.5)

    @pl.when(j == 6)
    def _():
        for g in range(groups):
            put(g, _rope128(acc_ref[:, seg(g)], c128[...], s128[...]) * HEAD_DIM ** -0.5)

    @pl.when(j == 7)
    def _():
        for g in range(groups):
            x = acc_ref[:, seg(g)]
            put(g, _rope128(x, c128[...], s128[...]) if g < 2 else x)

    plain = (j == 2) | (j == 4) | (j == 5) | (j >= 8)

    @pl.when(plain)
    def _():
        o_ref[...] = acc_ref[...].astype(o_ref.dtype)


def _inproj(xn, w, tabs, seq, tm=1024):
    n, d = xn.shape
    tm = _tile(seq, tm)
    per_seq = seq // tm
    tab_spec = pl.BlockSpec((tm, LANES), lambda i, j: (i % per_seq, 0))
    return pl.pallas_call(
        _inproj_kernel,
        grid=(n // tm, IN_COLS_PAD // IN_TN),
        in_specs=[pl.BlockSpec((tm, d), lambda i, j: (i, 0)),
                  pl.BlockSpec((d, IN_TN), lambda i, j: (0, j))] + [tab_spec] * 5,
        out_specs=pl.BlockSpec((tm, IN_TN), lambda i, j: (i, j)),
        out_shape=jax.ShapeDtypeStruct((n, IN_COLS_PAD), BF16),
        scratch_shapes=[pltpu.VMEM((tm, IN_TN), F32)],
        compiler_params=_cparams("parallel", "arbitrary"),
        name="inproj",
    )(xn, w, tabs["c64"], tabs["sa64"], tabs["sb64"], tabs["c128"], tabs["s128"])


def _mla_up_kernel(cq_ref, ckv_ref, kr_ref, gq_ref, gkv_ref, wq_ref, wkv_ref, c_ref, sa_ref, sb_ref,
                   q_ref, k_ref, v_ref):
    scale = (MLA_NOPE + MLA_ROPE) ** -0.5
    c, sa, sb = c_ref[...], sa_ref[...], sb_ref[...]
    cq = _rms(cq_ref[...].astype(F32), gq_ref[...]).astype(BF16)
    qa = jnp.dot(cq, wq_ref[...], preferred_element_type=F32)
    for h in range(GROUP_HEADS):
        base = h * 2 * LANES
        q_ref[:, base:base + LANES] = (qa[:, base:base + LANES] * scale).astype(q_ref.dtype)
        q_ref[:, base + LANES:base + 2 * LANES] = (
            _rope64(qa[:, base + LANES:base + 2 * LANES], c, sa, sb) * scale).astype(q_ref.dtype)
    ckv = _rms(ckv_ref[...].astype(F32), gkv_ref[...]).astype(BF16)
    kva = jnp.dot(ckv, wkv_ref[...], preferred_element_type=F32)
    kr = _rope64(kr_ref[...].astype(F32), c, sa, sb).astype(k_ref.dtype)
    for h in range(GROUP_HEADS):
        base = h * 2 * LANES
        k_ref[:, base:base + LANES] = kva[:, h * LANES:(h + 1) * LANES].astype(k_ref.dtype)
        k_ref[:, base + LANES:base + 2 * LANES] = kr
    v_ref[...] = kva[:, GROUP_WIDTH:].astype(v_ref.dtype)


def _mla_up(z, gq, gkv, wq, wkv, tabs, seq, tm=512):
    n = z.shape[0]
    tm = _tile(seq, tm)
    per_seq = seq // tm
    tab_spec = pl.BlockSpec((tm, LANES), lambda i: (i % per_seq, 0))
    full = lambda a: pl.BlockSpec(a.shape, lambda i: (0,) * a.ndim)
    gq = gq.reshape(1, -1)
    gkv = gkv.reshape(1, -1)
    return pl.pallas_call(
        _mla_up_kernel,
        grid=(n // tm,),
        in_specs=[pl.BlockSpec((tm, MLA_LORA), lambda i: (i, 8)),
                  pl.BlockSpec((tm, MLA_LORA), lambda i: (i, 9)),
                  pl.BlockSpec((tm, LANES), lambda i: (i, 40)),
                  full(gq), full(gkv), full(wq), full(wkv), tab_spec, tab_spec, tab_spec],
        out_specs=[pl.BlockSpec((tm, 2 * GROUP_WIDTH), lambda i: (i, 0)),
                   pl.BlockSpec((tm, 2 * GROUP_WIDTH), lambda i: (i, 0)),
                   pl.BlockSpec((tm, GROUP_WIDTH), lambda i: (i, 0))],
        out_shape=[jax.ShapeDtypeStruct((n, 2 * GROUP_WIDTH), BF16),
                   jax.ShapeDtypeStruct((n, 2 * GROUP_WIDTH), BF16),
                   jax.ShapeDtypeStruct((n, GROUP_WIDTH), BF16)],
        compiler_params=_cparams("parallel"),
        name="mla_up",
    )(z, z, z, gq, gkv, wq, wkv, tabs["c64p"], tabs["sa64p"], tabs["sb64p"])


def _qk(q, k):
    return lax.dot_general(q, k, (((1,), (1,)), ((), ())), preferred_element_type=F32)


def _flash_core(q_ref, k_ref, v_ref, m_ref, l_ref, acc_ref, tk):
    m_ref[...] = jnp.full(m_ref.shape, MASKED, F32)
    l_ref[...] = jnp.zeros(l_ref.shape, F32)
    acc_ref[...] = jnp.zeros(acc_ref.shape, F32)

    def body(t, carry):
        off = pl.multiple_of(t * tk, tk)
        k = k_ref[pl.ds(off, tk), :]
        v = v_ref[pl.ds(off, tk), :]
        for c in range(q_ref.shape[0] // ATT_SUB):
            rows = slice(c * ATT_SUB, (c + 1) * ATT_SUB)
            s = _qk(q_ref[rows, :], k)
            m_prev = m_ref[rows, :]
            m_new = jnp.maximum(m_prev, jnp.max(s, axis=-1, keepdims=True))
            alpha = jnp.exp(m_prev - m_new)
            p = jnp.exp(s - m_new)
            l_ref[rows, :] = alpha * l_ref[rows, :] + jnp.sum(p, axis=-1, keepdims=True)
            acc_ref[rows, :] = alpha * acc_ref[rows, :] + jnp.dot(p.astype(v.dtype), v,
                                                                  preferred_element_type=F32)
            m_ref[rows, :] = m_new
        return carry

    lax.fori_loop(0, k_ref.shape[0] // tk, body, 0)


def _flash_kernel(q_ref, k_ref, v_ref, o_ref, m_ref, l_ref, acc_ref, *, tk):
    _flash_core(q_ref, k_ref, v_ref, m_ref, l_ref, acc_ref, tk)
    o_ref[...] = (acc_ref[...] / l_ref[...]).astype(o_ref.dtype)


def _mla_attention(q, k, v, batch, seq, tq=512, tk=512):
    n = q.shape[0]
    tq, tk = _tile(seq, tq), _tile(seq, tk)
    nq = seq // tq
    dq = 2 * LANES
    return pl.pallas_call(
        functools.partial(_flash_kernel, tk=tk),
        grid=(batch, GROUP_HEADS, nq),
        in_specs=[pl.BlockSpec((tq, dq), lambda b, h, i: (b * nq + i, h)),
                  pl.BlockSpec((seq, dq), lambda b, h, i: (b, h)),
                  pl.BlockSpec((seq, LANES), lambda b, h, i: (b, h))],
        out_specs=pl.BlockSpec((tq, LANES), lambda b, h, i: (b * nq + i, h)),
        out_shape=jax.ShapeDtypeStruct((n, GROUP_WIDTH), BF16),
        scratch_shapes=[pltpu.VMEM((tq, 1), F32), pltpu.VMEM((tq, 1), F32), pltpu.VMEM((tq, LANES), F32)],
        compiler_params=_cparams("parallel", "parallel", "arbitrary"),
        name="mla_attention",
    )(q, k, v)


def _diff_kernel(lq1_ref, lk1_ref, lq2_ref, lk2_ref, g_ref, q_ref, k_ref, v_ref, o_ref,
                 qq_ref, m_ref, l_ref, acc_ref, *, tk, lambda_init):
    tq = q_ref.shape[0]
    q = q_ref[...]
    lane = lax.broadcasted_iota(jnp.int32, q.shape, 1)
    qq_ref[:tq, :] = jnp.where(lane < DIFF_QK_DIM, q, jnp.zeros_like(q))
    qq_ref[tq:, :] = jnp.where(lane >= DIFF_QK_DIM, q, jnp.zeros_like(q))
    _flash_core(qq_ref, k_ref, v_ref, m_ref, l_ref, acc_ref, tk)
    lam = (jnp.exp(jnp.sum(lq1_ref[...] * lk1_ref[...], axis=-1, keepdims=True))
           - jnp.exp(jnp.sum(lq2_ref[...] * lk2_ref[...], axis=-1, keepdims=True)) + lambda_init)
    o = acc_ref[:tq, :] / l_ref[:tq, :] - lam * (acc_ref[tq:, :] / l_ref[tq:, :])
    o_ref[...] = (_rms(o, g_ref[...]) * (1.0 - lambda_init)).astype(o_ref.dtype)


def _diff_attention(z, lq1, lk1, lq2, lk2, subln, lambda_init, batch, seq, tq=512, tk=512):
    n = z.shape[0]
    tq, tk = _tile(seq, tq), _tile(seq, tk)
    nq = seq // tq
    vec = lambda a: a.reshape(1, -1)
    small = lambda a: pl.BlockSpec(a.shape, lambda b, h, i: (0, 0))
    params = [vec(a) for a in (lq1, lk1, lq2, lk2, subln)]
    vm = lambda w: pltpu.VMEM((2 * tq, w), F32)
    return pl.pallas_call(
        functools.partial(_diff_kernel, tk=tk, lambda_init=lambda_init),
        grid=(batch, GROUP_HEADS, nq),
        in_specs=[small(a) for a in params] + [
            pl.BlockSpec((tq, LANES), lambda b, h, i: (b * nq + i, h)),
            pl.BlockSpec((seq, LANES), lambda b, h, i: (b, 4 + h)),
            pl.BlockSpec((seq, LANES), lambda b, h, i: (b, 8 + h))],
        out_specs=pl.BlockSpec((tq, LANES), lambda b, h, i: (b * nq + i, h)),
        out_shape=jax.ShapeDtypeStruct((n, GROUP_WIDTH), BF16),
        scratch_shapes=[pltpu.VMEM((2 * tq, LANES), BF16), vm(1), vm(1), vm(LANES)],
        compiler_params=_cparams("parallel", "parallel", "arbitrary"),
        name="diff_attention",
    )(*params, z, z, z)


def _na_bias_tables(rpb, rows):
    groups = rows // NA_GROUP_ROWS
    assert groups >= 3
    qc = np.arange(GRID_W)
    kc = np.arange(GRID_W)
    ws = np.clip(qc - NA_WIN_COLS // 2, 0, GRID_W - NA_WIN_COLS)
    col_ok = (kc[None, :] >= ws[:, None]) & (kc[None, :] < ws[:, None] + NA_WIN_COLS)
    edge = GRID_W - NA_WIN_COLS
    rp = jnp.pad(rpb.astype(F32), ((0, 0), (0, 0), (edge, edge)))
    toep = jnp.stack([rp[:, :, GRID_W - 1 - c:2 * GRID_W - 1 - c] for c in range(GRID_W)], axis=2)
    toep = jnp.where(jnp.asarray(col_ok)[None, None], toep, MASKED)
    masked = jnp.full((rpb.shape[0], GRID_W, GRID_W), MASKED, F32)
    tabs = []
    for m in (0, 1, groups - 1):
        kstart = np.clip(NA_GROUP_ROWS * m - NA_WIN_ROWS // 2, 0, rows - NA_KEY_ROWS)
        qr = NA_GROUP_ROWS * m + np.arange(NA_GROUP_ROWS)
        kr = kstart + np.arange(NA_KEY_ROWS)
        rs = np.clip(qr - NA_WIN_ROWS // 2, 0, rows - NA_WIN_ROWS)
        row_ok = (kr[None, :] >= rs[:, None]) & (kr[None, :] < rs[:, None] + NA_WIN_ROWS)
        ridx = kr[None, :] - qr[:, None] + NA_WIN_ROWS - 1
        tabs.append(jnp.concatenate([
            jnp.concatenate([toep[:, ridx[i, a]] if row_ok[i, a] else masked
                             for a in range(NA_KEY_ROWS)], axis=-1)
            for i in range(NA_GROUP_ROWS)], axis=1))
    return jnp.stack(tabs)


def _na_kernel(bias_ref, q_ref, k_ref, v_ref, o_ref, *, rows):
    groups = rows // NA_GROUP_ROWS
    tq = NA_GROUP_ROWS * GRID_W
    span = NA_KEY_ROWS * GRID_W

    def body(m, carry):
        cls = jnp.where(m == 0, 0, jnp.where(m == groups - 1, 2, 1))
        krow = jnp.clip(NA_GROUP_ROWS * m - NA_WIN_ROWS // 2, 0, rows - NA_KEY_ROWS)
        koff = pl.multiple_of(krow * GRID_W, GRID_W)
        qoff = pl.multiple_of(m * tq, tq)
        s = _qk(q_ref[pl.ds(qoff, tq), :], k_ref[pl.ds(koff, span), :]) + bias_ref[cls, 0]
        p = jnp.exp(s - jnp.max(s, axis=-1, keepdims=True))
        l = jnp.sum(p, axis=-1, keepdims=True)
        v = v_ref[pl.ds(koff, span), :]
        o = jnp.dot(p.astype(v.dtype), v, preferred_element_type=F32) / l
        o_ref[pl.ds(qoff, tq), :] = o.astype(o_ref.dtype)
        return carry

    lax.fori_loop(0, groups, body, 0)


def _na_attention(z, rpb, batch, seq):
    n = z.shape[0]
    rows = seq // GRID_W
    bias = _na_bias_tables(rpb, rows)
    tq, span = NA_GROUP_ROWS * GRID_W, NA_KEY_ROWS * GRID_W
    return pl.pallas_call(
        functools.partial(_na_kernel, rows=rows),
        grid=(batch, GROUP_HEADS),
        in_specs=[pl.BlockSpec((3, 1, tq, span), lambda b, h: (0, h, 0, 0)),
                  pl.BlockSpec((seq, LANES), lambda b, h: (b, 12 + h)),
                  pl.BlockSpec((seq, LANES), lambda b, h: (b, 16 + h)),
                  pl.BlockSpec((seq, LANES), lambda b, h: (b, 20 + h))],
        out_specs=pl.BlockSpec((seq, LANES), lambda b, h: (b, h)),
        out_shape=jax.ShapeDtypeStruct((n, GROUP_WIDTH), BF16),
        compiler_params=_cparams("parallel", "parallel"),
        name="na_attention",
    )(bias, z, z, z)


def _swa_kernel(sink_ref, q_ref, k_ref, v_ref, o_ref, *, seq):
    h = pl.program_id(1)
    q0 = pl.program_id(2) * SWA_TQ
    koff = pl.multiple_of(jnp.clip(q0 - SWA_WINDOW, 0, seq - SWA_SPAN), SWA_WINDOW)
    s = _qk(q_ref[...], k_ref[pl.ds(koff, SWA_SPAN), :])
    qpos = q0 + lax.broadcasted_iota(jnp.int32, s.shape, 0)
    kpos = koff + lax.broadcasted_iota(jnp.int32, s.shape, 1)
    s = jnp.where(jnp.abs(qpos - kpos) <= SWA_WINDOW, s, MASKED)
    sink = sink_ref[h]
    m = jnp.maximum(jnp.max(s, axis=-1, keepdims=True), sink)
    p = jnp.exp(s - m)
    l = jnp.sum(p, axis=-1, keepdims=True) + jnp.exp(sink - m)
    v = v_ref[pl.ds(koff, SWA_SPAN), :]
    o_ref[...] = (jnp.dot(p.astype(v.dtype), v, preferred_element_type=F32) / l).astype(o_ref.dtype)


def _swa_attention(z, sinks, batch, seq):
    n = z.shape[0]
    assert seq % SWA_TQ == 0 and seq >= SWA_SPAN
    nq = seq // SWA_TQ
    return pl.pallas_call(
        functools.partial(_swa_kernel, seq=seq),
        grid=(batch, GROUP_HEADS, nq),
        in_specs=[pl.BlockSpec(memory_space=pltpu.SMEM),
                  pl.BlockSpec((SWA_TQ, LANES), lambda b, h, i: (b * nq + i, 24 + h)),
                  pl.BlockSpec((seq, LANES), lambda b, h, i: (b, 28 + h // 2)),
                  pl.BlockSpec((seq, LANES), lambda b, h, i: (b, 30 + h // 2))],
        out_specs=pl.BlockSpec((SWA_TQ, LANES), lambda b, h, i: (b * nq + i, h)),
        out_shape=jax.ShapeDtypeStruct((n, GROUP_WIDTH), BF16),
        compiler_params=_cparams("parallel", "parallel", "arbitrary"),
        name="swa_attention",
    )(sinks, z, z, z)


def _dot_f32(x, w):
    xh = x.astype(BF16)
    xl = (x - xh.astype(F32)).astype(BF16)
    wh = w.astype(BF16)
    wl = (w - wh.astype(F32)).astype(BF16)
    d = lambda a, b: jnp.dot(a, b, preferred_element_type=F32)
    return d(xh, wh) + (d(xh, wl) + d(xl, wh))


def _top2(logits):
    lane = lax.broadcasted_iota(jnp.int32, logits.shape, 1)
    l1 = jnp.where(lane < N_EXPERTS, logits, MASKED)
    m1 = jnp.max(l1, axis=-1, keepdims=True)
    i1 = jnp.min(jnp.where(l1 == m1, lane, LANES), axis=-1, keepdims=True)
    l2 = jnp.where(lane == i1, MASKED, l1)
    m2 = jnp.max(l2, axis=-1, keepdims=True)
    i2 = jnp.min(jnp.where(l2 == m2, lane, LANES), axis=-1, keepdims=True)
    e2 = jnp.exp(m2 - m1)
    den = 1.0 + e2
    out = jnp.where(lane == 0, i1.astype(F32), 0.0)
    out = jnp.where(lane == 1, i2.astype(F32), out)
    out = jnp.where(lane == 2, 1.0 / den, out)
    return jnp.where(lane == 3, e2 / den, out)


def _outproj_kernel(*refs, routed):
    if routed:
        md, mn, ms, ml, w_ref, h_ref, g_ref, wr_ref, h1_ref, hn_ref, route_ref = refs
    else:
        md, mn, ms, ml, w_ref, h_ref, g_ref, h1_ref, hn_ref = refs
    acc = h_ref[...]
    for k, m in enumerate((md, mn, ms, ml)):
        acc = acc + jnp.dot(m[...], w_ref[k * GROUP_WIDTH:(k + 1) * GROUP_WIDTH, :],
                            preferred_element_type=F32)
    h1_ref[...] = acc
    hn = _rms(acc, g_ref[...])
    if routed:
        _slab_store(hn_ref, hn)
        route_ref[...] = _top2(_dot_f32(hn, wr_ref[...]))
    else:
        hn_ref[...] = hn.astype(hn_ref.dtype)


def _outproj(mixed, w, h, g, w_router=None, tm=256):
    n, d = h.shape
    tm = _tile(n, tm)
    routed = w_router is not None
    row = lambda width: pl.BlockSpec((tm, width), lambda i: (i, 0))
    full = lambda a: pl.BlockSpec(a.shape, lambda i: (0, 0))
    g = g.reshape(1, d)
    args = list(mixed) + [w, h, g]
    in_specs = [row(GROUP_WIDTH)] * 4 + [full(w), row(d), full(g)]
    if routed:
        args.append(w_router)
        in_specs.append(full(w_router))
        out_specs = [row(d), _slab_spec(tm, d, lambda i: (i, 0)), row(LANES)]
        out_shape = [jax.ShapeDtypeStruct((n, d), F32), _slab_shape(n, d),
                     jax.ShapeDtypeStruct((n, LANES), F32)]
    else:
        out_specs = [row(d), row(d)]
        out_shape = [jax.ShapeDtypeStruct((n, d), F32), jax.ShapeDtypeStruct((n, d), BF16)]
    return pl.pallas_call(
        functools.partial(_outproj_kernel, routed=routed),
        grid=(n // tm,),
        in_specs=in_specs, out_specs=out_specs, out_shape=out_shape,
        compiler_params=_cparams("parallel"),
        name="outproj_routed" if routed else "outproj",
    )(*args)


def _swiglu_step(x, wg_ref, wu_ref, wd_ref, o_ref, f):
    g = jnp.dot(x, wg_ref[...], preferred_element_type=F32)
    u = jnp.dot(x, wu_ref[...], preferred_element_type=F32)
    a = (g * jax.nn.sigmoid(g) * u).astype(BF16)
    y = jnp.dot(a, wd_ref[...], preferred_element_type=F32)

    @pl.when(f == 0)
    def _():
        o_ref[...] = y

    @pl.when(f > 0)
    def _():
        o_ref[...] += y


def _ffn_kernel(x_ref, wg_ref, wu_ref, wd_ref, o_ref):
    _swiglu_step(x_ref[...], wg_ref, wu_ref, wd_ref, o_ref, pl.program_id(1))


def _dense_ffn(x, wg, wu, wd, tm=1024, tf=512):
    n, d = x.shape
    ff = wg.shape[1]
    tm, tf = _tile(n, tm), _tile(ff, tf)
    return pl.pallas_call(
        _ffn_kernel,
        grid=(n // tm, ff // tf),
        in_specs=[pl.BlockSpec((tm, d), lambda i, f: (i, 0)),
                  pl.BlockSpec((d, tf), lambda i, f: (0, f)),
                  pl.BlockSpec((d, tf), lambda i, f: (0, f)),
                  pl.BlockSpec((tf, d), lambda i, f: (f, 0))],
        out_specs=pl.BlockSpec((tm, d), lambda i, f: (i, 0)),
        out_shape=jax.ShapeDtypeStruct((n, d), F32),
        compiler_params=_cparams("parallel", "arbitrary"),
        name="dense_ffn",
    )(x, wg, wu, wd)


def _moe_kernel(blk_exp_ref, n_used_ref, x_ref, wg_ref, wu_ref, wd_ref, o_ref, xb_ref, acc_ref):
    b, f = pl.program_id(0), pl.program_id(1)
    tm, d = xb_ref.shape
    used = b < n_used_ref[0]
    last = f == pl.num_programs(1) - 1

    @pl.when(used & (f == 0))
    def _():
        for c in range(d // LANES):
            xb_ref[:, c * LANES:(c + 1) * LANES] = _slab_chunk(x_ref, c, tm, d).astype(BF16)

    @pl.when(used)
    def _():
        _swiglu_step(xb_ref[...], wg_ref, wu_ref, wd_ref, acc_ref, f)

    @pl.when(used & last)
    def _():
        _slab_store(o_ref, acc_ref[...])

    @pl.when(jnp.logical_not(used) & last)
    def _():
        o_ref[...] = jnp.zeros(o_ref.shape, o_ref.dtype)


def _moe_ffn(xs, blk_exp, n_used, wg, wu, wd, tf=512):
    d, ff = wg.shape[1], wg.shape[2]
    n_slots = xs.shape[0] // (d // LANES)
    tf = _tile(ff, tf)
    nf = ff // tf
    n_blk = n_slots // MOE_TM

    def blk(b, nu):
        return jnp.minimum(b, nu[0] - 1)

    def col(b, f, nu):
        return jnp.where(b < nu[0], f, nf - 1)

    grid_spec = pltpu.PrefetchScalarGridSpec(
        num_scalar_prefetch=2,
        grid=(n_blk, nf),
        in_specs=[_slab_spec(MOE_TM, d, lambda b, f, be, nu: (blk(b, nu), 0)),
                  pl.BlockSpec((None, d, tf), lambda b, f, be, nu: (be[blk(b, nu)], 0, col(b, f, nu))),
                  pl.BlockSpec((None, d, tf), lambda b, f, be, nu: (be[blk(b, nu)], 0, col(b, f, nu))),
                  pl.BlockSpec((None, tf, d), lambda b, f, be, nu: (be[blk(b, nu)], col(b, f, nu), 0))],
        out_specs=_slab_spec(MOE_TM, d, lambda b, f, be, nu: (b, 0)),
        scratch_shapes=[pltpu.VMEM((MOE_TM, d), BF16), pltpu.VMEM((MOE_TM, d), F32)],
    )
    return pl.pallas_call(
        _moe_kernel,
        grid_spec=grid_spec,
        out_shape=_slab_shape(n_slots, d),
        compiler_params=_cparams("arbitrary", "arbitrary"),
        name="moe_ffn",
    )(blk_exp, n_used, xs, wg, wu, wd)


def _gather_kernel(idx_ref, src_ref, o_ref, sem, *, chunks):
    def copy(r):
        return pltpu.make_async_copy(src_ref.at[pl.ds(idx_ref[0, 0, r] * chunks, chunks)],
                                     o_ref.at[pl.ds(r * chunks, chunks)], sem)

    def start(r, carry):
        copy(r).start()
        return carry

    def wait(r, carry):
        copy(r).wait()
        return carry

    lax.fori_loop(0, GATHER_ROWS, start, 0, unroll=8)
    lax.fori_loop(0, GATHER_ROWS, wait, 0, unroll=8)


def _gather_rows(src, idx, width):
    n_out = idx.shape[0]
    assert n_out % GATHER_ROWS == 0
    steps = n_out // GATHER_ROWS
    return pl.pallas_call(
        functools.partial(_gather_kernel, chunks=width // LANES),
        grid=(steps,),
        in_specs=[pl.BlockSpec((1, 1, GATHER_ROWS), lambda i: (i, 0, 0), memory_space=pltpu.SMEM),
                  pl.BlockSpec(memory_space=pl.ANY)],
        out_specs=_slab_spec(GATHER_ROWS, width, lambda i: (i, 0)),
        out_shape=_slab_shape(n_out, width),
        scratch_shapes=[pltpu.SemaphoreType.DMA(())],
        compiler_params=_cparams("arbitrary"),
        name="gather_rows",
    )(idx.reshape(steps, 1, GATHER_ROWS), src)


def _ple_kernel(*refs, routed, last):
    if routed:
        h1_ref, y0_ref, y1_ref, route_ref, p_ref, gp_ref, wg_ref, wp_ref, gn_ref = refs[:9]
        outs, mix_ref = refs[9:-1], refs[-1]
        tm, d = h1_ref.shape
        route = route_ref[...]
        g0, g1 = route[:, 2:3], route[:, 3:4]
        for c in range(d // LANES):
            mix_ref[:, c * LANES:(c + 1) * LANES] = (g0 * _slab_chunk(y0_ref, c, tm, d)
                                                     + g1 * _slab_chunk(y1_ref, c, tm, d))
        h2 = h1_ref[...] + mix_ref[...]
    else:
        h1_ref, y_ref, p_ref, gp_ref, wg_ref, wp_ref, gn_ref = refs[:7]
        outs = refs[7:]
        h2 = h1_ref[...] + y_ref[...]
    hp = _rms(h2, gp_ref[...]).astype(BF16)
    gate = jax.nn.sigmoid(jnp.dot(hp, wg_ref[...], preferred_element_type=F32))
    pp = jnp.dot(p_ref[...].astype(BF16), wp_ref[...], preferred_element_type=F32)
    h3 = h2 + gate * pp
    if last:
        outs[0][...] = _rms(h3, gn_ref[...])
    else:
        outs[0][...] = h3
        outs[1][...] = _rms(h3, gn_ref[...]).astype(outs[1].dtype)


def _ple(h1, ys, route, p, g_ple, w_gate, w_proj, g_next, last, tm=256):
    n, d = h1.shape
    tm = _tile(n, tm)
    nt = n // tm
    routed = route is not None
    row = lambda width: pl.BlockSpec((tm, width), lambda i: (i, 0))
    full = lambda a: pl.BlockSpec(a.shape, lambda i: (0, 0))
    g_ple, g_next = g_ple.reshape(1, d), g_next.reshape(1, d)
    if routed:
        args = [h1, ys, ys, route]
        in_specs = [row(d), _slab_spec(tm, d, lambda i: (i, 0)), _slab_spec(tm, d, lambda i: (i + nt, 0)),
                    row(LANES)]
        scratch = [pltpu.VMEM((tm, d), F32)]
    else:
        args = [h1, ys]
        in_specs = [row(d), row(d)]
        scratch = []
    args += [p, g_ple, w_gate, w_proj, g_next]
    in_specs += [row(p.shape[1]), full(g_ple), full(w_gate), full(w_proj), full(g_next)]
    if last:
        out_specs, out_shape = [row(d)], [jax.ShapeDtypeStruct((n, d), F32)]
    else:
        out_specs = [row(d), row(d)]
        out_shape = [jax.ShapeDtypeStruct((n, d), F32), jax.ShapeDtypeStruct((n, d), BF16)]
    return pl.pallas_call(
        functools.partial(_ple_kernel, routed=routed, last=last),
        grid=(nt,),
        in_specs=in_specs, out_specs=out_specs, out_shape=out_shape, scratch_shapes=scratch,
        compiler_params=_cparams("parallel"),
        name="ple_routed" if routed else "ple",
    )(*args)


def _dispatch(route, n):
    idx = route[:, :2].astype(jnp.int32)
    flat_e = idx.reshape(-1)
    onehot = (flat_e[:, None] == jnp.arange(N_EXPERTS, dtype=jnp.int32)[None, :]).astype(jnp.int32)
    csum = jnp.cumsum(onehot, axis=0)
    rank = jnp.sum(onehot * csum, axis=1) - 1
    counts = csum[-1]
    padded = (counts + MOE_TM - 1) // MOE_TM * MOE_TM
    pad_end = jnp.cumsum(padded)
    pad_start = pad_end - padded
    dest = pad_start[flat_e] + rank
    n_blk = (2 * n) // MOE_TM + N_EXPERTS
    flat_tok = jnp.arange(2 * n, dtype=jnp.int32) // 2
    slot_tok = jnp.zeros((n_blk * MOE_TM,), jnp.int32).at[dest].set(flat_tok)
    blk_exp = jnp.minimum(jnp.searchsorted(pad_end, jnp.arange(n_blk, dtype=jnp.int32) * MOE_TM, side="right"),
                          N_EXPERTS - 1).astype(jnp.int32)
    n_used = (pad_end[-1] // MOE_TM).astype(jnp.int32).reshape(1)
    pos = dest.reshape(n, 2).T.reshape(-1).astype(jnp.int32)
    return slot_tok, blk_exp, n_used, pos


def _mla_weights(w_uq, w_ukv):
    lora = w_uq.shape[0]
    wq = w_uq.reshape(lora, GROUP_HEADS, MLA_NOPE + MLA_ROPE)
    wq = jnp.pad(wq, ((0, 0), (0, 0), (0, 2 * LANES - MLA_NOPE - MLA_ROPE)))
    wkv = w_ukv.reshape(lora, GROUP_HEADS, 2, HEAD_DIM).transpose(0, 2, 1, 3)
    return wq.reshape(lora, -1).astype(BF16), wkv.reshape(lora, -1).astype(BF16)


def kernel(x, p, attn_norm, w_in, diff_lq1, diff_lk1, diff_lq2, diff_lk2, diff_subln, na_rpb, swa_sinks,
           mla_q_norm, mla_kv_norm, mla_w_uq, mla_w_ukv, w_out, ffn_norm, dense_w_gate, dense_w_up,
           dense_w_down, moe_router, moe_w_gate, moe_w_up, moe_w_down, ple_norm, ple_gate, ple_proj,
           final_norm):
    batch, seq, d = x.shape
    n = batch * seq
    depth = w_in.shape[0]
    tabs = _rope_tables(seq)
    h = x.reshape(n, d)
    xn = _rmsnorm(h, attn_norm[0])
    for i in range(depth):
        lambda_init = 0.8 - 0.6 * math.exp(-0.3 * i)
        w_in_i = jnp.pad(w_in[i].astype(BF16), ((0, 0), (0, IN_COLS_PAD - IN_COLS)))
        z = _inproj(xn, w_in_i, tabs, seq)
        wq, wkv = _mla_weights(mla_w_uq[i], mla_w_ukv[i])
        q_l, k_l, v_l = _mla_up(z, mla_q_norm[i], mla_kv_norm[i], wq, wkv, tabs, seq)
        mixed = (
            _diff_attention(z, diff_lq1[i], diff_lk1[i], diff_lq2[i], diff_lk2[i], diff_subln[i],
                            lambda_init, batch, seq),
            _na_attention(z, na_rpb[i], batch, seq),
            _swa_attention(z, swa_sinks[i], batch, seq),
            _mla_attention(q_l, k_l, v_l, batch, seq),
        )
        j = i // 2
        routed = i % 2 == 1
        last = i == depth - 1
        g_next = final_norm if last else attn_norm[i + 1]
        if not routed:
            h1, hn = _outproj(mixed, w_out[i].astype(BF16), h, ffn_norm[i])
            y = _dense_ffn(hn, dense_w_gate[j].astype(BF16), dense_w_up[j].astype(BF16),
                           dense_w_down[j].astype(BF16))
            route = None
        else:
            w_router = jnp.pad(moe_router[j], ((0, 0), (0, LANES - N_EXPERTS)))
            h1, hn, route = _outproj(mixed, w_out[i].astype(BF16), h, ffn_norm[i], w_router)
            slot_tok, blk_exp, n_used, pos = _dispatch(route, n)
            xs = _gather_rows(hn, slot_tok, d)
            ys = _moe_ffn(xs, blk_exp, n_used, moe_w_gate[j].astype(BF16), moe_w_up[j].astype(BF16),
                          moe_w_down[j].astype(BF16))
            y = _gather_rows(ys, pos, d)
        outs = _ple(h1, y, route, p[i].reshape(n, -1), ple_norm[i], ple_gate[i].astype(BF16),
                    ple_proj[i].astype(BF16), g_next, last)
        if last:
            return outs[0].reshape(batch, seq, d)
        h, xn = outs
```

```python
import functools
import math

import numpy as np
import jax
import jax.numpy as jnp
from jax import lax
from jax.experimental import pallas as pl
from jax.experimental.pallas import tpu as pltpu

F32 = jnp.float32
BF16 = jnp.bfloat16

D_MODEL = 2048
HEAD_DIM = 128
GROUP_HEADS = 4
GROUP_WIDTH = GROUP_HEADS * HEAD_DIM
ROPE_THETA = 10000.0
NORM_EPS = 1e-6
DIFF_QK_DIM = 64
GRID_W = 64
NA_WIN_ROWS = 8
NA_WIN_COLS = 16
SWA_WINDOW = 128
MLA_LORA = 512
MLA_NOPE = 128
MLA_ROPE = 64
IN_COLS = 5184
FFN_DIM = 7168
N_EXPERTS = 8
PLE_DIM = 256

LANES = 128
VMEM_LIMIT = 56 * 1024 * 1024
MASKED = -1e30

IN_TN = 512
IN_COLS_PAD = 11 * IN_TN
NA_GROUP_ROWS = 4
NA_KEY_ROWS = NA_GROUP_ROWS + NA_WIN_ROWS
LOG2E = math.log2(math.e)
SWA_TQ = 512
SWA_SPAN = SWA_TQ + 2 * SWA_WINDOW
MOE_TM = 512
GATHER_ROWS = 512


def _tile(n, pref):
    t = min(n, pref)
    assert n % t == 0, (n, t)
    return t


def _cparams(*sem):
    return pltpu.CompilerParams(dimension_semantics=sem, vmem_limit_bytes=VMEM_LIMIT)


def _rms(x, g):
    return x * lax.rsqrt(jnp.mean(x * x, axis=-1, keepdims=True) + NORM_EPS) * g


def _slab_shape(rows, width):
    return jax.ShapeDtypeStruct((rows * (width // LANES), LANES), F32)


def _slab_spec(tm, width, index_map):
    return pl.BlockSpec((tm * (width // LANES), LANES), index_map)


def _slab_chunk(ref, c, tm, width):
    return ref[pl.ds(c, tm, stride=width // LANES), :]


def _slab_store(ref, x):
    tm, width = x.shape
    for c in range(width // LANES):
        ref[pl.ds(c, tm, stride=width // LANES), :] = x[:, c * LANES:(c + 1) * LANES]


def _rmsnorm_kernel(x_ref, g_ref, o_ref):
    o_ref[...] = _rms(x_ref[...], g_ref[...]).astype(o_ref.dtype)


def _rmsnorm(x, g, tm=512):
    n, d = x.shape
    tm = _tile(n, tm)
    return pl.pallas_call(
        _rmsnorm_kernel,
        grid=(n // tm,),
        in_specs=[pl.BlockSpec((tm, d), lambda i: (i, 0)),
                  pl.BlockSpec((1, d), lambda i: (0, 0))],
        out_specs=pl.BlockSpec((tm, d), lambda i: (i, 0)),
        out_shape=jax.ShapeDtypeStruct((n, d), BF16),
        compiler_params=_cparams("parallel"),
        name="rmsnorm",
    )(x, g.reshape(1, d))


def _rope_tables(seq):
    def base(dim):
        inv = ROPE_THETA ** (-jnp.arange(0, dim, 2, dtype=F32) / dim)
        ang = jnp.arange(seq, dtype=F32)[:, None] * inv[None, :]
        ang = jnp.concatenate([ang, ang], axis=-1)
        return jnp.cos(ang), jnp.sin(ang)

    lane = np.arange(LANES)
    cos, sin = base(DIFF_QK_DIM)
    c64 = jnp.concatenate([cos, cos], axis=-1)
    s64 = jnp.concatenate([sin, sin], axis=-1)
    lo = jnp.asarray((lane % 64) < 32)
    sa64 = jnp.where(lo, -s64, 0.0)
    sb64 = jnp.where(lo, 0.0, s64)
    first = jnp.asarray(lane < 64)
    c64p, sa64p, sb64p = (jnp.where(first, t, 0.0) for t in (c64, sa64, sb64))
    cos, sin = base(HEAD_DIM)
    s128 = jnp.where(jnp.asarray(lane < 64), -sin, sin)
    return dict(c64=c64, sa64=sa64, sb64=sb64, c64p=c64p, sa64p=sa64p, sb64p=sb64p,
                c128=cos, s128=s128)


def _rope64(x, c, sa, sb):
    return x * c + pltpu.roll(x, 96, 1) * sa + pltpu.roll(x, 32, 1) * sb


def _rope128(x, c, s):
    return x * c + pltpu.roll(x, 64, 1) * s


def _inproj_kernel(x_ref, w_ref, c64, sa64, sb64, c128, s128, o_ref, acc_ref):
    j = pl.program_id(1)
    acc_ref[...] = jnp.dot(x_ref[...], w_ref[...], preferred_element_type=F32)
    groups = IN_TN // LANES

    def seg(g):
        return slice(g * LANES, (g + 1) * LANES)

    def put(g, y):
        o_ref[:, seg(g)] = y.astype(o_ref.dtype)

    @pl.when(j == 0)
    def _():
        for g in range(groups):
            put(g, _rope64(acc_ref[:, seg(g)], c64[...], sa64[...], sb64[...]) * (DIFF_QK_DIM ** -0.5 * LOG2E))

    @pl.when(j == 1)
    def _():
        for g in range(groups):
            put(g, _rope64(acc_ref[:, seg(g)], c64[...], sa64[...], sb64[...]))

    @pl.when(j == 3)
    def _():
        for g in range(groups):
            put(g, acc_ref[:, seg(g)] * HEAD_DIM ** -0.5)

    @pl.when(j == 6)
    def _():
        for g in range(groups):
            put(g, _rope128(acc_ref[:, seg(g)], c128[...], s128[...]) * HEAD_DIM ** -0.5)

    @pl.when(j == 7)
    def _():
        for g in range(groups):
            x = acc_ref[:, seg(g)]
            put(g, _rope128(x, c128[...], s128[...]) if g < 2 else x)

    plain = (j == 2) | (j == 4) | (j == 5) | (j >= 8)

    @pl.when(plain)
    def _():
        o_ref[...] = acc_ref[...].astype(o_ref.dtype)


def _inproj(xn, w, tabs, seq, tm=1024):
    n, d = xn.shape
    tm = _tile(seq, tm)
    per_seq = seq // tm
    tab_spec = pl.BlockSpec((tm, LANES), lambda i, j: (i % per_seq, 0))
    return pl.pallas_call(
        _inproj_kernel,
        grid=(n // tm, IN_COLS_PAD // IN_TN),
        in_specs=[pl.BlockSpec((tm, d), lambda i, j: (i, 0)),
                  pl.BlockSpec((d, IN_TN), lambda i, j: (0, j))] + [tab_spec] * 5,
        out_specs=pl.BlockSpec((tm, IN_TN), lambda i, j: (i, j)),
        out_shape=jax.ShapeDtypeStruct((n, IN_COLS_PAD), BF16),
        scratch_shapes=[pltpu.VMEM((tm, IN_TN), F32)],
        compiler_params=_cparams("parallel", "arbitrary"),
        name="inproj",
    )(xn, w, tabs["c64"], tabs["sa64"], tabs["sb64"], tabs["c128"], tabs["s128"])


def _mla_up_kernel(cq_ref, ckv_ref, kr_ref, gq_ref, gkv_ref, wq_ref, wkv_ref, c_ref, sa_ref, sb_ref,
                   q_ref, k_ref, v_ref):
    scale = (MLA_NOPE + MLA_ROPE) ** -0.5 * LOG2E
    c, sa, sb = c_ref[...], sa_ref[...], sb_ref[...]
    cq = _rms(cq_ref[...].astype(F32), gq_ref[...]).astype(BF16)
    qa = jnp.dot(cq, wq_ref[...], preferred_element_type=F32)
    for h in range(GROUP_HEADS):
        base = h * 2 * LANES
        q_ref[:, base:base + LANES] = (qa[:, base:base + LANES] * scale).astype(q_ref.dtype)
        q_ref[:, base + LANES:base + 2 * LANES] = (
            _rope64(qa[:, base + LANES:base + 2 * LANES], c, sa, sb) * scale).astype(q_ref.dtype)
    ckv = _rms(ckv_ref[...].astype(F32), gkv_ref[...]).astype(BF16)
    kva = jnp.dot(ckv, wkv_ref[...], preferred_element_type=F32)
    kr = _rope64(kr_ref[...].astype(F32), c, sa, sb).astype(k_ref.dtype)
    for h in range(GROUP_HEADS):
        base = h * 2 * LANES
        k_ref[:, base:base + LANES] = kva[:, h * LANES:(h + 1) * LANES].astype(k_ref.dtype)
        k_ref[:, base + LANES:base + 2 * LANES] = kr
    v_ref[...] = kva[:, GROUP_WIDTH:].astype(v_ref.dtype)


def _mla_up(z, gq, gkv, wq, wkv, tabs, seq, tm=512):
    n = z.shape[0]
    tm = _tile(seq, tm)
    per_seq = seq // tm
    tab_spec = pl.BlockSpec((tm, LANES), lambda i: (i % per_seq, 0))
    full = lambda a: pl.BlockSpec(a.shape, lambda i: (0,) * a.ndim)
    gq = gq.reshape(1, -1)
    gkv = gkv.reshape(1, -1)
    return pl.pallas_call(
        _mla_up_kernel,
        grid=(n // tm,),
        in_specs=[pl.BlockSpec((tm, MLA_LORA), lambda i: (i, 8)),
                  pl.BlockSpec((tm, MLA_LORA), lambda i: (i, 9)),
                  pl.BlockSpec((tm, LANES), lambda i: (i, 40)),
                  full(gq), full(gkv), full(wq), full(wkv), tab_spec, tab_spec, tab_spec],
        out_specs=[pl.BlockSpec((tm, 2 * GROUP_WIDTH), lambda i: (i, 0)),
                   pl.BlockSpec((tm, 2 * GROUP_WIDTH), lambda i: (i, 0)),
                   pl.BlockSpec((tm, GROUP_WIDTH), lambda i: (i, 0))],
        out_shape=[jax.ShapeDtypeStruct((n, 2 * GROUP_WIDTH), BF16),
                   jax.ShapeDtypeStruct((n, 2 * GROUP_WIDTH), BF16),
                   jax.ShapeDtypeStruct((n, GROUP_WIDTH), BF16)],
        compiler_params=_cparams("parallel"),
        name="mla_up",
    )(z, z, z, gq, gkv, wq, wkv, tabs["c64p"], tabs["sa64p"], tabs["sb64p"])


def _qk(q, k):
    return lax.dot_general(q, k, (((1,), (1,)), ((), ())), preferred_element_type=F32)


def _flash_core(q_ref, k_ref, v_ref, vt_ref, s_ref, m_ref, l_ref, acc_ref, tk):
    nt = k_ref.shape[0] // tk
    assert nt % 2 == 0

    @pl.when(pl.program_id(2) == 0)
    def _():
        vt_ref[...] = v_ref[...].T

    m_ref[...] = jnp.full(m_ref.shape, MASKED, F32)
    l_ref[...] = jnp.zeros(l_ref.shape, F32)
    acc_ref[...] = jnp.zeros(acc_ref.shape, F32)

    def scores(t, slot):
        off = pl.multiple_of(jnp.minimum(t, nt - 1) * tk, tk)
        s_ref[slot] = _qk(k_ref[pl.ds(off, tk), :], q_ref[...])

    def consume(t, slot):
        off = pl.multiple_of(t * tk, tk)
        s = s_ref[slot]
        m_prev = m_ref[...]
        m_new = jnp.maximum(m_prev, jnp.max(s, axis=0, keepdims=True))
        alpha = jnp.exp2(m_prev - m_new)
        p = jnp.exp2(s - m_new)
        l_ref[...] = alpha * l_ref[...] + jnp.sum(p, axis=0, keepdims=True)
        acc_ref[...] = alpha * acc_ref[...] + jnp.dot(vt_ref[:, pl.ds(off, tk)], p.astype(vt_ref.dtype),
                                                      preferred_element_type=F32)
        m_ref[...] = m_new

    scores(0, 0)

    def body(j, carry):
        scores(2 * j + 1, 1)
        consume(2 * j, 0)
        scores(2 * j + 2, 0)
        consume(2 * j + 1, 1)
        return carry

    lax.fori_loop(0, nt // 2, body, 0)


def _flash_scratch(seq, tk, queries):
    return [pltpu.VMEM((LANES, seq), BF16), pltpu.VMEM((2, tk, queries), F32),
            pltpu.VMEM((1, queries), F32), pltpu.VMEM((1, queries), F32), pltpu.VMEM((LANES, queries), F32)]


def _flash_kernel(q_ref, k_ref, v_ref, o_ref, vt_ref, s_ref, m_ref, l_ref, acc_ref, *, tk):
    _flash_core(q_ref, k_ref, v_ref, vt_ref, s_ref, m_ref, l_ref, acc_ref, tk)
    o_ref[...] = (acc_ref[...] / l_ref[...]).T.astype(o_ref.dtype)


def _mla_attention(q, k, v, batch, seq, tq=512, tk=512):
    n = q.shape[0]
    tq, tk = _tile(seq, tq), _tile(seq, tk)
    nq = seq // tq
    dq = 2 * LANES
    return pl.pallas_call(
        functools.partial(_flash_kernel, tk=tk),
        grid=(batch, GROUP_HEADS, nq),
        in_specs=[pl.BlockSpec((tq, dq), lambda b, h, i: (b * nq + i, h)),
                  pl.BlockSpec((seq, dq), lambda b, h, i: (b, h)),
                  pl.BlockSpec((seq, LANES), lambda b, h, i: (b, h))],
        out_specs=pl.BlockSpec((tq, LANES), lambda b, h, i: (b * nq + i, h)),
        out_shape=jax.ShapeDtypeStruct((n, GROUP_WIDTH), BF16),
        scratch_shapes=_flash_scratch(seq, tk, tq),
        compiler_params=_cparams("parallel", "parallel", "arbitrary"),
        name="mla_attention",
    )(q, k, v)


def _diff_kernel(lq1_ref, lk1_ref, lq2_ref, lk2_ref, g_ref, q_ref, k_ref, v_ref, o_ref,
                 qq_ref, vt_ref, s_ref, m_ref, l_ref, acc_ref, *, tk, lambda_init):
    tq = q_ref.shape[0]
    q = q_ref[...]
    lane = lax.broadcasted_iota(jnp.int32, q.shape, 1)
    qq_ref[:tq, :] = jnp.where(lane < DIFF_QK_DIM, q, jnp.zeros_like(q))
    qq_ref[tq:, :] = jnp.where(lane >= DIFF_QK_DIM, q, jnp.zeros_like(q))
    _flash_core(qq_ref, k_ref, v_ref, vt_ref, s_ref, m_ref, l_ref, acc_ref, tk)
    lam = (jnp.exp(jnp.sum(lq1_ref[...] * lk1_ref[...], axis=-1, keepdims=True))
           - jnp.exp(jnp.sum(lq2_ref[...] * lk2_ref[...], axis=-1, keepdims=True)) + lambda_init)
    o = (acc_ref[:, :tq] / l_ref[:, :tq] - lam * (acc_ref[:, tq:] / l_ref[:, tq:])).T
    o_ref[...] = (_rms(o, g_ref[...]) * (1.0 - lambda_init)).astype(o_ref.dtype)


def _diff_attention(z, lq1, lk1, lq2, lk2, subln, lambda_init, batch, seq, tq=512, tk=512):
    n = z.shape[0]
    tq, tk = _tile(seq, tq), _tile(seq, tk)
    nq = seq // tq
    vec = lambda a: a.reshape(1, -1)
    small = lambda a: pl.BlockSpec(a.shape, lambda b, h, i: (0, 0))
    params = [vec(a) for a in (lq1, lk1, lq2, lk2, subln)]
    return pl.pallas_call(
        functools.partial(_diff_kernel, tk=tk, lambda_init=lambda_init),
        grid=(batch, GROUP_HEADS, nq),
        in_specs=[small(a) for a in params] + [
            pl.BlockSpec((tq, LANES), lambda b, h, i: (b * nq + i, h)),
            pl.BlockSpec((seq, LANES), lambda b, h, i: (b, 4 + h)),
            pl.BlockSpec((seq, LANES), lambda b, h, i: (b, 8 + h))],
        out_specs=pl.BlockSpec((tq, LANES), lambda b, h, i: (b * nq + i, h)),
        out_shape=jax.ShapeDtypeStruct((n, GROUP_WIDTH), BF16),
        scratch_shapes=[pltpu.VMEM((2 * tq, LANES), BF16)] + _flash_scratch(seq, tk, 2 * tq),
        compiler_params=_cparams("parallel", "parallel", "arbitrary"),
        name="diff_attention",
    )(*params, z, z, z)


def _na_bias_tables(rpb, rows):
    groups = rows // NA_GROUP_ROWS
    assert groups >= 3
    qc = np.arange(GRID_W)
    kc = np.arange(GRID_W)
    ws = np.clip(qc - NA_WIN_COLS // 2, 0, GRID_W - NA_WIN_COLS)
    col_ok = (kc[None, :] >= ws[:, None]) & (kc[None, :] < ws[:, None] + NA_WIN_COLS)
    edge = GRID_W - NA_WIN_COLS
    rp = jnp.pad(rpb.astype(F32), ((0, 0), (0, 0), (edge, edge)))
    toep = jnp.stack([rp[:, :, GRID_W - 1 - c:2 * GRID_W - 1 - c] for c in range(GRID_W)], axis=2)
    toep = jnp.where(jnp.asarray(col_ok)[None, None], toep, MASKED)
    masked = jnp.full((rpb.shape[0], GRID_W, GRID_W), MASKED, F32)
    tabs = []
    for m in (0, 1, groups - 1):
        kstart = np.clip(NA_GROUP_ROWS * m - NA_WIN_ROWS // 2, 0, rows - NA_KEY_ROWS)
        qr = NA_GROUP_ROWS * m + np.arange(NA_GROUP_ROWS)
        kr = kstart + np.arange(NA_KEY_ROWS)
        rs = np.clip(qr - NA_WIN_ROWS // 2, 0, rows - NA_WIN_ROWS)
        row_ok = (kr[None, :] >= rs[:, None]) & (kr[None, :] < rs[:, None] + NA_WIN_ROWS)
        ridx = kr[None, :] - qr[:, None] + NA_WIN_ROWS - 1
        tabs.append(jnp.concatenate([
            jnp.concatenate([toep[:, ridx[i, a]] if row_ok[i, a] else masked
                             for a in range(NA_KEY_ROWS)], axis=-1)
            for i in range(NA_GROUP_ROWS)], axis=1))
    return jnp.stack(tabs)


def _na_kernel(bias_ref, q_ref, k_ref, v_ref, o_ref, *, rows):
    groups = rows // NA_GROUP_ROWS
    tq = NA_GROUP_ROWS * GRID_W
    span = NA_KEY_ROWS * GRID_W

    def body(m, carry):
        cls = jnp.where(m == 0, 0, jnp.where(m == groups - 1, 2, 1))
        krow = jnp.clip(NA_GROUP_ROWS * m - NA_WIN_ROWS // 2, 0, rows - NA_KEY_ROWS)
        koff = pl.multiple_of(krow * GRID_W, GRID_W)
        qoff = pl.multiple_of(m * tq, tq)
        s = _qk(q_ref[pl.ds(qoff, tq), :], k_ref[pl.ds(koff, span), :]) + bias_ref[cls, 0]
        p = jnp.exp(s - jnp.max(s, axis=-1, keepdims=True))
        l = jnp.sum(p, axis=-1, keepdims=True)
        v = v_ref[pl.ds(koff, span), :]
        o = jnp.dot(p.astype(v.dtype), v, preferred_element_type=F32) / l
        o_ref[pl.ds(qoff, tq), :] = o.astype(o_ref.dtype)
        return carry

    lax.fori_loop(0, groups, body, 0)


def _na_attention(z, rpb, batch, seq):
    n = z.shape[0]
    rows = seq // GRID_W
    bias = _na_bias_tables(rpb, rows)
    tq, span = NA_GROUP_ROWS * GRID_W, NA_KEY_ROWS * GRID_W
    return pl.pallas_call(
        functools.partial(_na_kernel, rows=rows),
        grid=(batch, GROUP_HEADS),
        in_specs=[pl.BlockSpec((3, 1, tq, span), lambda b, h: (0, h, 0, 0)),
                  pl.BlockSpec((seq, LANES), lambda b, h: (b, 12 + h)),
                  pl.BlockSpec((seq, LANES), lambda b, h: (b, 16 + h)),
                  pl.BlockSpec((seq, LANES), lambda b, h: (b, 20 + h))],
        out_specs=pl.BlockSpec((seq, LANES), lambda b, h: (b, h)),
        out_shape=jax.ShapeDtypeStruct((n, GROUP_WIDTH), BF16),
        compiler_params=_cparams("parallel", "parallel"),
        name="na_attention",
    )(bias, z, z, z)


def _swa_kernel(sink_ref, q_ref, k_ref, v_ref, o_ref, *, seq):
    h = pl.program_id(1)
    q0 = pl.program_id(2) * SWA_TQ
    koff = pl.multiple_of(jnp.clip(q0 - SWA_WINDOW, 0, seq - SWA_SPAN), SWA_WINDOW)
    s = _qk(q_ref[...], k_ref[pl.ds(koff, SWA_SPAN), :])
    qpos = q0 + lax.broadcasted_iota(jnp.int32, s.shape, 0)
    kpos = koff + lax.broadcasted_iota(jnp.int32, s.shape, 1)
    s = jnp.where(jnp.abs(qpos - kpos) <= SWA_WINDOW, s, MASKED)
    sink = sink_ref[h]
    m = jnp.maximum(jnp.max(s, axis=-1, keepdims=True), sink)
    p = jnp.exp(s - m)
    l = jnp.sum(p, axis=-1, keepdims=True) + jnp.exp(sink - m)
    v = v_ref[pl.ds(koff, SWA_SPAN), :]
    o_ref[...] = (jnp.dot(p.astype(v.dtype), v, preferred_element_type=F32) / l).astype(o_ref.dtype)


def _swa_attention(z, sinks, batch, seq):
    n = z.shape[0]
    assert seq % SWA_TQ == 0 and seq >= SWA_SPAN
    nq = seq // SWA_TQ
    return pl.pallas_call(
        functools.partial(_swa_kernel, seq=seq),
        grid=(batch, GROUP_HEADS, nq),
        in_specs=[pl.BlockSpec(memory_space=pltpu.SMEM),
                  pl.BlockSpec((SWA_TQ, LANES), lambda b, h, i: (b * nq + i, 24 + h)),
                  pl.BlockSpec((seq, LANES), lambda b, h, i: (b, 28 + h // 2)),
                  pl.BlockSpec((seq, LANES), lambda b, h, i: (b, 30 + h // 2))],
        out_specs=pl.BlockSpec((SWA_TQ, LANES), lambda b, h, i: (b * nq + i, h)),
        out_shape=jax.ShapeDtypeStruct((n, GROUP_WIDTH), BF16),
        compiler_params=_cparams("parallel", "parallel", "arbitrary"),
        name="swa_attention",
    )(sinks, z, z, z)


def _dot_f32(x, w):
    xh = x.astype(BF16)
    xl = (x - xh.astype(F32)).astype(BF16)
    wh = w.astype(BF16)
    wl = (w - wh.astype(F32)).astype(BF16)
    d = lambda a, b: jnp.dot(a, b, preferred_element_type=F32)
    return d(xh, wh) + (d(xh, wl) + d(xl, wh))


def _top2(logits):
    lane = lax.broadcasted_iota(jnp.int32, logits.shape, 1)
    l1 = jnp.where(lane < N_EXPERTS, logits, MASKED)
    m1 = jnp.max(l1, axis=-1, keepdims=True)
    i1 = jnp.min(jnp.where(l1 == m1, lane, LANES), axis=-1, keepdims=True)
    l2 = jnp.where(lane == i1, MASKED, l1)
    m2 = jnp.max(l2, axis=-1, keepdims=True)
    i2 = jnp.min(jnp.where(l2 == m2, lane, LANES), axis=-1, keepdims=True)
    e2 = jnp.exp(m2 - m1)
    den = 1.0 + e2
    out = jnp.where(lane == 0, i1.astype(F32), 0.0)
    out = jnp.where(lane == 1, i2.astype(F32), out)
    out = jnp.where(lane == 2, 1.0 / den, out)
    return jnp.where(lane == 3, e2 / den, out)


def _outproj_kernel(*refs, routed):
    if routed:
        md, mn, ms, ml, w_ref, h_ref, g_ref, wr_ref, h1_ref, hn_ref, route_ref = refs
    else:
        md, mn, ms, ml, w_ref, h_ref, g_ref, h1_ref, hn_ref = refs
    acc = h_ref[...]
    for k, m in enumerate((md, mn, ms, ml)):
        acc = acc + jnp.dot(m[...], w_ref[k * GROUP_WIDTH:(k + 1) * GROUP_WIDTH, :],
                            preferred_element_type=F32)
    h1_ref[...] = acc
    hn = _rms(acc, g_ref[...])
    if routed:
        _slab_store(hn_ref, hn)
        route_ref[...] = _top2(_dot_f32(hn, wr_ref[...]))
    else:
        hn_ref[...] = hn.astype(hn_ref.dtype)


def _outproj(mixed, w, h, g, w_router=None, tm=256):
    n, d = h.shape
    tm = _tile(n, tm)
    routed = w_router is not None
    row = lambda width: pl.BlockSpec((tm, width), lambda i: (i, 0))
    full = lambda a: pl.BlockSpec(a.shape, lambda i: (0, 0))
    g = g.reshape(1, d)
    args = list(mixed) + [w, h, g]
    in_specs = [row(GROUP_WIDTH)] * 4 + [full(w), row(d), full(g)]
    if routed:
        args.append(w_router)
        in_specs.append(full(w_router))
        out_specs = [row(d), _slab_spec(tm, d, lambda i: (i, 0)), row(LANES)]
        out_shape = [jax.ShapeDtypeStruct((n, d), F32), _slab_shape(n, d),
                     jax.ShapeDtypeStruct((n, LANES), F32)]
    else:
        out_specs = [row(d), row(d)]
        out_shape = [jax.ShapeDtypeStruct((n, d), F32), jax.ShapeDtypeStruct((n, d), BF16)]
    return pl.pallas_call(
        functools.partial(_outproj_kernel, routed=routed),
        grid=(n // tm,),
        in_specs=in_specs, out_specs=out_specs, out_shape=out_shape,
        compiler_params=_cparams("parallel"),
        name="outproj_routed" if routed else "outproj",
    )(*args)


def _swiglu_step(x, wg_ref, wu_ref, wd_ref, o_ref, f):
    @pl.when(f == 0)
    def _():
        o_ref[...] = jnp.zeros(o_ref.shape, o_ref.dtype)

    g = jnp.dot(x, wg_ref[...], preferred_element_type=F32)
    u = jnp.dot(x, wu_ref[...], preferred_element_type=F32)
    a = (g * jax.nn.sigmoid(g) * u).astype(BF16)
    o_ref[...] += jnp.dot(a, wd_ref[...], preferred_element_type=F32)


def _ffn_kernel(x_ref, wg_ref, wu_ref, wd_ref, o_ref):
    _swiglu_step(x_ref[...], wg_ref, wu_ref, wd_ref, o_ref, pl.program_id(1))


def _dense_ffn(x, wg, wu, wd, tm=1024, tf=512):
    n, d = x.shape
    ff = wg.shape[1]
    tm, tf = _tile(n, tm), _tile(ff, tf)
    return pl.pallas_call(
        _ffn_kernel,
        grid=(n // tm, ff // tf),
        in_specs=[pl.BlockSpec((tm, d), lambda i, f: (i, 0)),
                  pl.BlockSpec((d, tf), lambda i, f: (0, f)),
                  pl.BlockSpec((d, tf), lambda i, f: (0, f)),
                  pl.BlockSpec((tf, d), lambda i, f: (f, 0))],
        out_specs=pl.BlockSpec((tm, d), lambda i, f: (i, 0)),
        out_shape=jax.ShapeDtypeStruct((n, d), F32),
        compiler_params=_cparams("parallel", "arbitrary"),
        name="dense_ffn",
    )(x, wg, wu, wd)


def _moe_kernel(blk_exp_ref, n_used_ref, x_ref, wg_ref, wu_ref, wd_ref, o_ref, xb_ref, acc_ref):
    b, f = pl.program_id(0), pl.program_id(1)
    tm, d = xb_ref.shape
    used = b < n_used_ref[0]
    last = f == pl.num_programs(1) - 1

    @pl.when(used & (f == 0))
    def _():
        for c in range(d // LANES):
            xb_ref[:, c * LANES:(c + 1) * LANES] = _slab_chunk(x_ref, c, tm, d).astype(BF16)

    @pl.when(used)
    def _():
        _swiglu_step(xb_ref[...], wg_ref, wu_ref, wd_ref, acc_ref, f)

    @pl.when(used & last)
    def _():
        _slab_store(o_ref, acc_ref[...])

    @pl.when(jnp.logical_not(used) & last)
    def _():
        o_ref[...] = jnp.zeros(o_ref.shape, o_ref.dtype)


def _moe_ffn(xs, blk_exp, n_used, wg, wu, wd, tf=512):
    d, ff = wg.shape[1], wg.shape[2]
    n_slots = xs.shape[0] // (d // LANES)
    tf = _tile(ff, tf)
    nf = ff // tf
    n_blk = n_slots // MOE_TM

    def blk(b, nu):
        return jnp.minimum(b, nu[0] - 1)

    def col(b, f, nu):
        return jnp.where(b < nu[0], f, nf - 1)

    grid_spec = pltpu.PrefetchScalarGridSpec(
        num_scalar_prefetch=2,
        grid=(n_blk, nf),
        in_specs=[_slab_spec(MOE_TM, d, lambda b, f, be, nu: (blk(b, nu), 0)),
                  pl.BlockSpec((None, d, tf), lambda b, f, be, nu: (be[blk(b, nu)], 0, col(b, f, nu))),
                  pl.BlockSpec((None, d, tf), lambda b, f, be, nu: (be[blk(b, nu)], 0, col(b, f, nu))),
                  pl.BlockSpec((None, tf, d), lambda b, f, be, nu: (be[blk(b, nu)], col(b, f, nu), 0))],
        out_specs=_slab_spec(MOE_TM, d, lambda b, f, be, nu: (b, 0)),
        scratch_shapes=[pltpu.VMEM((MOE_TM, d), BF16), pltpu.VMEM((MOE_TM, d), F32)],
    )
    return pl.pallas_call(
        _moe_kernel,
        grid_spec=grid_spec,
        out_shape=_slab_shape(n_slots, d),
        compiler_params=_cparams("arbitrary", "arbitrary"),
        name="moe_ffn",
    )(blk_exp, n_used, xs, wg, wu, wd)


def _gather_kernel(idx_ref, src_ref, o_ref, sem, *, chunks):
    def copy(r):
        return pltpu.make_async_copy(src_ref.at[pl.ds(idx_ref[0, 0, r] * chunks, chunks)],
                                     o_ref.at[pl.ds(r * chunks, chunks)], sem)

    def start(r, carry):
        copy(r).start()
        return carry

    def wait(r, carry):
        copy(r).wait()
        return carry

    lax.fori_loop(0, GATHER_ROWS, start, 0, unroll=8)
    lax.fori_loop(0, GATHER_ROWS, wait, 0, unroll=8)


def _gather_rows(src, idx, width):
    n_out = idx.shape[0]
    assert n_out % GATHER_ROWS == 0
    steps = n_out // GATHER_ROWS
    return pl.pallas_call(
        functools.partial(_gather_kernel, chunks=width // LANES),
        grid=(steps,),
        in_specs=[pl.BlockSpec((1, 1, GATHER_ROWS), lambda i: (i, 0, 0), memory_space=pltpu.SMEM),
                  pl.BlockSpec(memory_space=pl.ANY)],
        out_specs=_slab_spec(GATHER_ROWS, width, lambda i: (i, 0)),
        out_shape=_slab_shape(n_out, width),
        scratch_shapes=[pltpu.SemaphoreType.DMA(())],
        compiler_params=_cparams("arbitrary"),
        name="gather_rows",
    )(idx.reshape(steps, 1, GATHER_ROWS), src)


def _ple_kernel(*refs, routed, last):
    if routed:
        h1_ref, y0_ref, y1_ref, route_ref, p_ref, gp_ref, wg_ref, wp_ref, gn_ref = refs[:9]
        outs, mix_ref = refs[9:-1], refs[-1]
        tm, d = h1_ref.shape
        route = route_ref[...]
        g0, g1 = route[:, 2:3], route[:, 3:4]
        for c in range(d // LANES):
            mix_ref[:, c * LANES:(c + 1) * LANES] = (g0 * _slab_chunk(y0_ref, c, tm, d)
                                                     + g1 * _slab_chunk(y1_ref, c, tm, d))
        h2 = h1_ref[...] + mix_ref[...]
    else:
        h1_ref, y_ref, p_ref, gp_ref, wg_ref, wp_ref, gn_ref = refs[:7]
        outs = refs[7:]
        h2 = h1_ref[...] + y_ref[...]
    hp = _rms(h2, gp_ref[...]).astype(BF16)
    gate = jax.nn.sigmoid(jnp.dot(hp, wg_ref[...], preferred_element_type=F32))
    pp = jnp.dot(p_ref[...].astype(BF16), wp_ref[...], preferred_element_type=F32)
    h3 = h2 + gate * pp
    if last:
        outs[0][...] = _rms(h3, gn_ref[...])
    else:
        outs[0][...] = h3
        outs[1][...] = _rms(h3, gn_ref[...]).astype(outs[1].dtype)


def _ple(h1, ys, route, p, g_ple, w_gate, w_proj, g_next, last, tm=256):
    n, d = h1.shape
    tm = _tile(n, tm)
    nt = n // tm
    routed = route is not None
    row = lambda width: pl.BlockSpec((tm, width), lambda i: (i, 0))
    full = lambda a: pl.BlockSpec(a.shape, lambda i: (0, 0))
    g_ple, g_next = g_ple.reshape(1, d), g_next.reshape(1, d)
    if routed:
        args = [h1, ys, ys, route]
        in_specs = [row(d), _slab_spec(tm, d, lambda i: (i, 0)), _slab_spec(tm, d, lambda i: (i + nt, 0)),
                    row(LANES)]
        scratch = [pltpu.VMEM((tm, d), F32)]
    else:
        args = [h1, ys]
        in_specs = [row(d), row(d)]
        scratch = []
    args += [p, g_ple, w_gate, w_proj, g_next]
    in_specs += [row(p.shape[1]), full(g_ple), full(w_gate), full(w_proj), full(g_next)]
    if last:
        out_specs, out_shape = [row(d)], [jax.ShapeDtypeStruct((n, d), F32)]
    else:
        out_specs = [row(d), row(d)]
        out_shape = [jax.ShapeDtypeStruct((n, d), F32), jax.ShapeDtypeStruct((n, d), BF16)]
    return pl.pallas_call(
        functools.partial(_ple_kernel, routed=routed, last=last),
        grid=(nt,),
        in_specs=in_specs, out_specs=out_specs, out_shape=out_shape, scratch_shapes=scratch,
        compiler_params=_cparams("parallel"),
        name="ple_routed" if routed else "ple",
    )(*args)


def _dispatch(route, n):
    idx = route[:, :2].astype(jnp.int32)
    flat_e = idx.reshape(-1)
    onehot = (flat_e[:, None] == jnp.arange(N_EXPERTS, dtype=jnp.int32)[None, :]).astype(jnp.int32)
    csum = jnp.cumsum(onehot, axis=0)
    rank = jnp.sum(onehot * csum, axis=1) - 1
    counts = csum[-1]
    padded = (counts + MOE_TM - 1) // MOE_TM * MOE_TM
    pad_end = jnp.cumsum(padded)
    pad_start = pad_end - padded
    dest = pad_start[flat_e] + rank
    n_blk = (2 * n) // MOE_TM + N_EXPERTS
    flat_tok = jnp.arange(2 * n, dtype=jnp.int32) // 2
    slot_tok = jnp.zeros((n_blk * MOE_TM,), jnp.int32).at[dest].set(flat_tok)
    blk_exp = jnp.minimum(jnp.searchsorted(pad_end, jnp.arange(n_blk, dtype=jnp.int32) * MOE_TM, side="right"),
                          N_EXPERTS - 1).astype(jnp.int32)
    n_used = (pad_end[-1] // MOE_TM).astype(jnp.int32).reshape(1)
    pos = dest.reshape(n, 2).T.reshape(-1).astype(jnp.int32)
    return slot_tok, blk_exp, n_used, pos


def _mla_weights(w_uq, w_ukv):
    lora = w_uq.shape[0]
    wq = w_uq.reshape(lora, GROUP_HEADS, MLA_NOPE + MLA_ROPE)
    wq = jnp.pad(wq, ((0, 0), (0, 0), (0, 2 * LANES - MLA_NOPE - MLA_ROPE)))
    wkv = w_ukv.reshape(lora, GROUP_HEADS, 2, HEAD_DIM).transpose(0, 2, 1, 3)
    return wq.reshape(lora, -1).astype(BF16), wkv.reshape(lora, -1).astype(BF16)


def kernel(x, p, attn_norm, w_in, diff_lq1, diff_lk1, diff_lq2, diff_lk2, diff_subln, na_rpb, swa_sinks,
           mla_q_norm, mla_kv_norm, mla_w_uq, mla_w_ukv, w_out, ffn_norm, dense_w_gate, dense_w_up,
           dense_w_down, moe_router, moe_w_gate, moe_w_up, moe_w_down, ple_norm, ple_gate, ple_proj,
           final_norm):
    batch, seq, d = x.shape
    n = batch * seq
    depth = w_in.shape[0]
    tabs = _rope_tables(seq)
    h = x.reshape(n, d)
    xn = _rmsnorm(h, attn_norm[0])
    for i in range(depth):
        lambda_init = 0.8 - 0.6 * math.exp(-0.3 * i)
        w_in_i = jnp.pad(w_in[i].astype(BF16), ((0, 0), (0, IN_COLS_PAD - IN_COLS)))
        z = _inproj(xn, w_in_i, tabs, seq)
        wq, wkv = _mla_weights(mla_w_uq[i], mla_w_ukv[i])
        q_l, k_l, v_l = _mla_up(z, mla_q_norm[i], mla_kv_norm[i], wq, wkv, tabs, seq)
        mixed = (
            _diff_attention(z, diff_lq1[i], diff_lk1[i], diff_lq2[i], diff_lk2[i], diff_subln[i],
                            lambda_init, batch, seq),
            _na_attention(z, na_rpb[i], batch, seq),
            _swa_attention(z, swa_sinks[i], batch, seq),
            _mla_attention(q_l, k_l, v_l, batch, seq),
        )
        j = i // 2
        routed = i % 2 == 1
        last = i == depth - 1
        g_next = final_norm if last else attn_norm[i + 1]
        if not routed:
            h1, hn = _outproj(mixed, w_out[i].astype(BF16), h, ffn_norm[i])
            y = _dense_ffn(hn, dense_w_gate[j].astype(BF16), dense_w_up[j].astype(BF16),
                           dense_w_down[j].astype(BF16))
            route = None
        else:
            w_router = jnp.pad(moe_router[j], ((0, 0), (0, LANES - N_EXPERTS)))
            h1, hn, route = _outproj(mixed, w_out[i].astype(BF16), h, ffn_norm[i], w_router)
            slot_tok, blk_exp, n_used, pos = _dispatch(route, n)
            xs = _gather_rows(hn, slot_tok, d)
            ys = _moe_ffn(xs, blk_exp, n_used, moe_w_gate[j].astype(BF16), moe_w_up[j].astype(BF16),
                          moe_w_down[j].astype(BF16))
            y = _gather_rows(ys, pos, d)
        outs = _ple(h1, y, route, p[i].reshape(n, -1), ple_norm[i], ple_gate[i].astype(BF16),
                    ple_proj[i].astype(BF16), g_next, last)
        if last:
            return outs[0].reshape(batch, seq, d)
        h, xn = outs
```

```python
import functools
import math

import numpy as np
import jax
import jax.numpy as jnp
from jax import lax
from jax.experimental import pallas as pl
from jax.experimental.pallas import tpu as pltpu

F32 = jnp.float32
BF16 = jnp.bfloat16

D_MODEL = 2048
HEAD_DIM = 128
GROUP_HEADS = 4
GROUP_WIDTH = GROUP_HEADS * HEAD_DIM
ROPE_THETA = 10000.0
NORM_EPS = 1e-6
DIFF_QK_DIM = 64
GRID_W = 64
NA_WIN_ROWS = 8
NA_WIN_COLS = 16
SWA_WINDOW = 128
MLA_LORA = 512
MLA_NOPE = 128
MLA_ROPE = 64
IN_COLS = 5184
FFN_DIM = 7168
N_EXPERTS = 8
PLE_DIM = 256

LANES = 128
VMEM_LIMIT = 56 * 1024 * 1024
MASKED = -1e30

IN_TN = 512
IN_COLS_PAD = 11 * IN_TN
NA_GROUP_ROWS = 4
NA_KEY_ROWS = NA_GROUP_ROWS + NA_WIN_ROWS
LOG2E = math.log2(math.e)
SWA_TQ = 512
SWA_SPAN = SWA_TQ + 2 * SWA_WINDOW
MOE_TM = 512
GATHER_ROWS = 512


def _tile(n, pref):
    t = min(n, pref)
    assert n % t == 0, (n, t)
    return t


def _cparams(*sem):
    return pltpu.CompilerParams(dimension_semantics=sem, vmem_limit_bytes=VMEM_LIMIT)


def _rms(x, g):
    return x * lax.rsqrt(jnp.mean(x * x, axis=-1, keepdims=True) + NORM_EPS) * g


def _slab_shape(rows, width):
    return jax.ShapeDtypeStruct((rows * (width // LANES), LANES), F32)


def _slab_spec(tm, width, index_map):
    return pl.BlockSpec((tm * (width // LANES), LANES), index_map)


def _slab_chunk(ref, c, tm, width):
    return ref[pl.ds(c, tm, stride=width // LANES), :]


def _slab_store(ref, x):
    tm, width = x.shape
    for c in range(width // LANES):
        ref[pl.ds(c, tm, stride=width // LANES), :] = x[:, c * LANES:(c + 1) * LANES]


def _rope_tables(seq):
    def base(dim):
        inv = ROPE_THETA ** (-jnp.arange(0, dim, 2, dtype=F32) / dim)
        ang = jnp.arange(seq, dtype=F32)[:, None] * inv[None, :]
        ang = jnp.concatenate([ang, ang], axis=-1)
        return jnp.cos(ang), jnp.sin(ang)

    lane = np.arange(LANES)
    cos, sin = base(DIFF_QK_DIM)
    c64 = jnp.concatenate([cos, cos], axis=-1)
    s64 = jnp.concatenate([sin, sin], axis=-1)
    lo = jnp.asarray((lane % 64) < 32)
    sa64 = jnp.where(lo, -s64, 0.0)
    sb64 = jnp.where(lo, 0.0, s64)
    first = jnp.asarray(lane < 64)
    c64p, sa64p, sb64p = (jnp.where(first, t, 0.0) for t in (c64, sa64, sb64))
    cos, sin = base(HEAD_DIM)
    s128 = jnp.where(jnp.asarray(lane < 64), -sin, sin)
    return dict(c64=c64, sa64=sa64, sb64=sb64, c64p=c64p, sa64p=sa64p, sb64p=sb64p,
                c128=cos, s128=s128)


def _rope64(x, c, sa, sb):
    return x * c + pltpu.roll(x, 96, 1) * sa + pltpu.roll(x, 32, 1) * sb


def _rope128(x, c, s):
    return x * c + pltpu.roll(x, 64, 1) * s


def _inproj_kernel(h_ref, g_ref, w_ref, c64, sa64, sb64, c128, s128, o_ref):
    x = _rms(h_ref[...], g_ref[...]).astype(BF16)
    groups = IN_TN // LANES
    for j in range(IN_COLS_PAD // IN_TN):
        acc = jnp.dot(x, w_ref[:, j * IN_TN:(j + 1) * IN_TN], preferred_element_type=F32)
        for g in range(groups):
            a = acc[:, g * LANES:(g + 1) * LANES]
            if j == 0:
                a = _rope64(a, c64[...], sa64[...], sb64[...]) * (DIFF_QK_DIM ** -0.5 * LOG2E)
            elif j == 1:
                a = _rope64(a, c64[...], sa64[...], sb64[...])
            elif j == 3:
                a = a * HEAD_DIM ** -0.5
            elif j == 6:
                a = _rope128(a, c128[...], s128[...]) * HEAD_DIM ** -0.5
            elif j == 7 and g < 2:
                a = _rope128(a, c128[...], s128[...])
            o_ref[:, j * IN_TN + g * LANES:j * IN_TN + (g + 1) * LANES] = a.astype(o_ref.dtype)


def _inproj(h, g, w, tabs, seq, tm=512):
    n, d = h.shape
    tm = _tile(seq, tm)
    per_seq = seq // tm
    tab_spec = pl.BlockSpec((tm, LANES), lambda i: (i % per_seq, 0))
    g = g.reshape(1, d)
    return pl.pallas_call(
        _inproj_kernel,
        grid=(n // tm,),
        in_specs=[pl.BlockSpec((tm, d), lambda i: (i, 0)),
                  pl.BlockSpec(g.shape, lambda i: (0, 0)),
                  pl.BlockSpec(w.shape, lambda i: (0, 0), pipeline_mode=pl.Buffered(1))] + [tab_spec] * 5,
        out_specs=pl.BlockSpec((tm, IN_COLS_PAD), lambda i: (i, 0)),
        out_shape=jax.ShapeDtypeStruct((n, IN_COLS_PAD), BF16),
        compiler_params=_cparams("parallel"),
        name="inproj",
    )(h, g, w, tabs["c64"], tabs["sa64"], tabs["sb64"], tabs["c128"], tabs["s128"])


def _mla_up_kernel(cq_ref, ckv_ref, kr_ref, gq_ref, gkv_ref, wq_ref, wkv_ref, c_ref, sa_ref, sb_ref,
                   q_ref, k_ref, v_ref):
    scale = (MLA_NOPE + MLA_ROPE) ** -0.5 * LOG2E
    c, sa, sb = c_ref[...], sa_ref[...], sb_ref[...]
    cq = _rms(cq_ref[...].astype(F32), gq_ref[...]).astype(BF16)
    qa = jnp.dot(cq, wq_ref[...], preferred_element_type=F32)
    for h in range(GROUP_HEADS):
        base = h * 2 * LANES
        q_ref[:, base:base + LANES] = (qa[:, base:base + LANES] * scale).astype(q_ref.dtype)
        q_ref[:, base + LANES:base + 2 * LANES] = (
            _rope64(qa[:, base + LANES:base + 2 * LANES], c, sa, sb) * scale).astype(q_ref.dtype)
    ckv = _rms(ckv_ref[...].astype(F32), gkv_ref[...]).astype(BF16)
    kva = jnp.dot(ckv, wkv_ref[...], preferred_element_type=F32)
    kr = _rope64(kr_ref[...].astype(F32), c, sa, sb).astype(k_ref.dtype)
    for h in range(GROUP_HEADS):
        base = h * 2 * LANES
        k_ref[:, base:base + LANES] = kva[:, h * LANES:(h + 1) * LANES].astype(k_ref.dtype)
        k_ref[:, base + LANES:base + 2 * LANES] = kr
    v_ref[...] = kva[:, GROUP_WIDTH:].astype(v_ref.dtype)


def _mla_up(z, gq, gkv, wq, wkv, tabs, seq, tm=512):
    n = z.shape[0]
    tm = _tile(seq, tm)
    per_seq = seq // tm
    tab_spec = pl.BlockSpec((tm, LANES), lambda i: (i % per_seq, 0))
    full = lambda a: pl.BlockSpec(a.shape, lambda i: (0,) * a.ndim)
    gq = gq.reshape(1, -1)
    gkv = gkv.reshape(1, -1)
    return pl.pallas_call(
        _mla_up_kernel,
        grid=(n // tm,),
        in_specs=[pl.BlockSpec((tm, MLA_LORA), lambda i: (i, 8)),
                  pl.BlockSpec((tm, MLA_LORA), lambda i: (i, 9)),
                  pl.BlockSpec((tm, LANES), lambda i: (i, 40)),
                  full(gq), full(gkv), full(wq), full(wkv), tab_spec, tab_spec, tab_spec],
        out_specs=[pl.BlockSpec((tm, 2 * GROUP_WIDTH), lambda i: (i, 0)),
                   pl.BlockSpec((tm, 2 * GROUP_WIDTH), lambda i: (i, 0)),
                   pl.BlockSpec((tm, GROUP_WIDTH), lambda i: (i, 0))],
        out_shape=[jax.ShapeDtypeStruct((n, 2 * GROUP_WIDTH), BF16),
                   jax.ShapeDtypeStruct((n, 2 * GROUP_WIDTH), BF16),
                   jax.ShapeDtypeStruct((n, GROUP_WIDTH), BF16)],
        compiler_params=_cparams("parallel"),
        name="mla_up",
    )(z, z, z, gq, gkv, wq, wkv, tabs["c64p"], tabs["sa64p"], tabs["sb64p"])


def _qk(q, k):
    return lax.dot_general(q, k, (((1,), (1,)), ((), ())), preferred_element_type=F32)


def _flash_core(q_ref, k_ref, v_ref, vt_ref, s_ref, m_ref, l_ref, acc_ref, tk):
    nt = k_ref.shape[0] // tk
    assert nt % 2 == 0

    @pl.when(pl.program_id(2) == 0)
    def _():
        vt_ref[...] = v_ref[...].T

    m_ref[...] = jnp.full(m_ref.shape, MASKED, F32)
    l_ref[...] = jnp.zeros(l_ref.shape, F32)
    acc_ref[...] = jnp.zeros(acc_ref.shape, F32)

    def scores(t, slot):
        off = pl.multiple_of(jnp.minimum(t, nt - 1) * tk, tk)
        s_ref[slot] = _qk(k_ref[pl.ds(off, tk), :], q_ref[...])

    def consume(t, slot):
        off = pl.multiple_of(t * tk, tk)
        s = s_ref[slot]
        m_prev = m_ref[...]
        m_new = jnp.maximum(m_prev, jnp.max(s, axis=0, keepdims=True))
        alpha = jnp.exp2(m_prev - m_new)
        p = jnp.exp2(s - m_new)
        l_ref[...] = alpha * l_ref[...] + jnp.sum(p, axis=0, keepdims=True)
        acc_ref[...] = alpha * acc_ref[...] + jnp.dot(vt_ref[:, pl.ds(off, tk)], p.astype(vt_ref.dtype),
                                                      preferred_element_type=F32)
        m_ref[...] = m_new

    scores(0, 0)

    def body(j, carry):
        scores(2 * j + 1, 1)
        consume(2 * j, 0)
        scores(2 * j + 2, 0)
        consume(2 * j + 1, 1)
        return carry

    lax.fori_loop(0, nt // 2, body, 0)


def _flash_scratch(seq, tk, queries):
    return [pltpu.VMEM((LANES, seq), BF16), pltpu.VMEM((2, tk, queries), F32),
            pltpu.VMEM((1, queries), F32), pltpu.VMEM((1, queries), F32), pltpu.VMEM((LANES, queries), F32)]


def _flash_kernel(q_ref, k_ref, v_ref, o_ref, vt_ref, s_ref, m_ref, l_ref, acc_ref, *, tk):
    _flash_core(q_ref, k_ref, v_ref, vt_ref, s_ref, m_ref, l_ref, acc_ref, tk)
    o_ref[...] = (acc_ref[...] / l_ref[...]).T.astype(o_ref.dtype)


def _mla_attention(q, k, v, batch, seq, tq=512, tk=512):
    n = q.shape[0]
    tq, tk = _tile(seq, tq), _tile(seq, tk)
    nq = seq // tq
    dq = 2 * LANES
    return pl.pallas_call(
        functools.partial(_flash_kernel, tk=tk),
        grid=(batch, GROUP_HEADS, nq),
        in_specs=[pl.BlockSpec((tq, dq), lambda b, h, i: (b * nq + i, h)),
                  pl.BlockSpec((seq, dq), lambda b, h, i: (b, h)),
                  pl.BlockSpec((seq, LANES), lambda b, h, i: (b, h))],
        out_specs=pl.BlockSpec((tq, LANES), lambda b, h, i: (b * nq + i, h)),
        out_shape=jax.ShapeDtypeStruct((n, GROUP_WIDTH), BF16),
        scratch_shapes=_flash_scratch(seq, tk, tq),
        compiler_params=_cparams("parallel", "parallel", "arbitrary"),
        name="mla_attention",
    )(q, k, v)


def _diff_kernel(lq1_ref, lk1_ref, lq2_ref, lk2_ref, g_ref, q_ref, k_ref, v_ref, o_ref,
                 qq_ref, vt_ref, s_ref, m_ref, l_ref, acc_ref, *, tk, lambda_init):
    tq = q_ref.shape[0]
    q = q_ref[...]
    lane = lax.broadcasted_iota(jnp.int32, q.shape, 1)
    qq_ref[:tq, :] = jnp.where(lane < DIFF_QK_DIM, q, jnp.zeros_like(q))
    qq_ref[tq:, :] = jnp.where(lane >= DIFF_QK_DIM, q, jnp.zeros_like(q))
    _flash_core(qq_ref, k_ref, v_ref, vt_ref, s_ref, m_ref, l_ref, acc_ref, tk)
    lam = (jnp.exp(jnp.sum(lq1_ref[...] * lk1_ref[...], axis=-1, keepdims=True))
           - jnp.exp(jnp.sum(lq2_ref[...] * lk2_ref[...], axis=-1, keepdims=True)) + lambda_init)
    o = (acc_ref[:, :tq] / l_ref[:, :tq] - lam * (acc_ref[:, tq:] / l_ref[:, tq:])).T
    o_ref[...] = (_rms(o, g_ref[...]) * (1.0 - lambda_init)).astype(o_ref.dtype)


def _diff_attention(z, lq1, lk1, lq2, lk2, subln, lambda_init, batch, seq, tq=512, tk=512):
    n = z.shape[0]
    tq, tk = _tile(seq, tq), _tile(seq, tk)
    nq = seq // tq
    vec = lambda a: a.reshape(1, -1)
    small = lambda a: pl.BlockSpec(a.shape, lambda b, h, i: (0, 0))
    params = [vec(a) for a in (lq1, lk1, lq2, lk2, subln)]
    return pl.pallas_call(
        functools.partial(_diff_kernel, tk=tk, lambda_init=lambda_init),
        grid=(batch, GROUP_HEADS, nq),
        in_specs=[small(a) for a in params] + [
            pl.BlockSpec((tq, LANES), lambda b, h, i: (b * nq + i, h)),
            pl.BlockSpec((seq, LANES), lambda b, h, i: (b, 4 + h)),
            pl.BlockSpec((seq, LANES), lambda b, h, i: (b, 8 + h))],
        out_specs=pl.BlockSpec((tq, LANES), lambda b, h, i: (b * nq + i, h)),
        out_shape=jax.ShapeDtypeStruct((n, GROUP_WIDTH), BF16),
        scratch_shapes=[pltpu.VMEM((2 * tq, LANES), BF16)] + _flash_scratch(seq, tk, 2 * tq),
        compiler_params=_cparams("parallel", "parallel", "arbitrary"),
        name="diff_attention",
    )(*params, z, z, z)


def _na_bias_tables(rpb, rows):
    groups = rows // NA_GROUP_ROWS
    assert groups >= 3
    qc = np.arange(GRID_W)
    kc = np.arange(GRID_W)
    ws = np.clip(qc - NA_WIN_COLS // 2, 0, GRID_W - NA_WIN_COLS)
    col_ok = (kc[None, :] >= ws[:, None]) & (kc[None, :] < ws[:, None] + NA_WIN_COLS)
    edge = GRID_W - NA_WIN_COLS
    rp = jnp.pad(rpb.astype(F32), ((0, 0), (0, 0), (edge, edge)))
    toep = jnp.stack([rp[:, :, GRID_W - 1 - c:2 * GRID_W - 1 - c] for c in range(GRID_W)], axis=2)
    toep = jnp.where(jnp.asarray(col_ok)[None, None], toep, MASKED)
    masked = jnp.full((rpb.shape[0], GRID_W, GRID_W), MASKED, F32)
    tabs = []
    for m in (0, 1, groups - 1):
        kstart = np.clip(NA_GROUP_ROWS * m - NA_WIN_ROWS // 2, 0, rows - NA_KEY_ROWS)
        qr = NA_GROUP_ROWS * m + np.arange(NA_GROUP_ROWS)
        kr = kstart + np.arange(NA_KEY_ROWS)
        rs = np.clip(qr - NA_WIN_ROWS // 2, 0, rows - NA_WIN_ROWS)
        row_ok = (kr[None, :] >= rs[:, None]) & (kr[None, :] < rs[:, None] + NA_WIN_ROWS)
        ridx = kr[None, :] - qr[:, None] + NA_WIN_ROWS - 1
        tabs.append(jnp.concatenate([
            jnp.concatenate([toep[:, ridx[i, a]] if row_ok[i, a] else masked
                             for a in range(NA_KEY_ROWS)], axis=-1)
            for i in range(NA_GROUP_ROWS)], axis=1))
    return jnp.stack(tabs)


def _na_kernel(bias_ref, q_ref, k_ref, v_ref, o_ref, *, rows):
    groups = rows // NA_GROUP_ROWS
    tq = NA_GROUP_ROWS * GRID_W
    span = NA_KEY_ROWS * GRID_W

    def body(m, carry):
        cls = jnp.where(m == 0, 0, jnp.where(m == groups - 1, 2, 1))
        krow = jnp.clip(NA_GROUP_ROWS * m - NA_WIN_ROWS // 2, 0, rows - NA_KEY_ROWS)
        koff = pl.multiple_of(krow * GRID_W, GRID_W)
        qoff = pl.multiple_of(m * tq, tq)
        s = _qk(q_ref[pl.ds(qoff, tq), :], k_ref[pl.ds(koff, span), :]) + bias_ref[cls, 0]
        p = jnp.exp(s - jnp.max(s, axis=-1, keepdims=True))
        l = jnp.sum(p, axis=-1, keepdims=True)
        v = v_ref[pl.ds(koff, span), :]
        o = jnp.dot(p.astype(v.dtype), v, preferred_element_type=F32) / l
        o_ref[pl.ds(qoff, tq), :] = o.astype(o_ref.dtype)
        return carry

    lax.fori_loop(0, groups, body, 0)


def _na_attention(z, rpb, batch, seq):
    n = z.shape[0]
    rows = seq // GRID_W
    bias = _na_bias_tables(rpb, rows)
    tq, span = NA_GROUP_ROWS * GRID_W, NA_KEY_ROWS * GRID_W
    return pl.pallas_call(
        functools.partial(_na_kernel, rows=rows),
        grid=(batch, GROUP_HEADS),
        in_specs=[pl.BlockSpec((3, 1, tq, span), lambda b, h: (0, h, 0, 0)),
                  pl.BlockSpec((seq, LANES), lambda b, h: (b, 12 + h)),
                  pl.BlockSpec((seq, LANES), lambda b, h: (b, 16 + h)),
                  pl.BlockSpec((seq, LANES), lambda b, h: (b, 20 + h))],
        out_specs=pl.BlockSpec((seq, LANES), lambda b, h: (b, h)),
        out_shape=jax.ShapeDtypeStruct((n, GROUP_WIDTH), BF16),
        compiler_params=_cparams("parallel", "parallel"),
        name="na_attention",
    )(bias, z, z, z)


def _swa_kernel(sink_ref, q_ref, k_ref, v_ref, o_ref, *, seq):
    h = pl.program_id(1)
    q0 = pl.program_id(2) * SWA_TQ
    koff = pl.multiple_of(jnp.clip(q0 - SWA_WINDOW, 0, seq - SWA_SPAN), SWA_WINDOW)
    s = _qk(q_ref[...], k_ref[pl.ds(koff, SWA_SPAN), :])
    qpos = q0 + lax.broadcasted_iota(jnp.int32, s.shape, 0)
    kpos = koff + lax.broadcasted_iota(jnp.int32, s.shape, 1)
    s = jnp.where(jnp.abs(qpos - kpos) <= SWA_WINDOW, s, MASKED)
    sink = sink_ref[h]
    m = jnp.maximum(jnp.max(s, axis=-1, keepdims=True), sink)
    p = jnp.exp(s - m)
    l = jnp.sum(p, axis=-1, keepdims=True) + jnp.exp(sink - m)
    v = v_ref[pl.ds(koff, SWA_SPAN), :]
    o_ref[...] = (jnp.dot(p.astype(v.dtype), v, preferred_element_type=F32) / l).astype(o_ref.dtype)


def _swa_attention(z, sinks, batch, seq):
    n = z.shape[0]
    assert seq % SWA_TQ == 0 and seq >= SWA_SPAN
    nq = seq // SWA_TQ
    return pl.pallas_call(
        functools.partial(_swa_kernel, seq=seq),
        grid=(batch, GROUP_HEADS, nq),
        in_specs=[pl.BlockSpec(memory_space=pltpu.SMEM),
                  pl.BlockSpec((SWA_TQ, LANES), lambda b, h, i: (b * nq + i, 24 + h)),
                  pl.BlockSpec((seq, LANES), lambda b, h, i: (b, 28 + h // 2)),
                  pl.BlockSpec((seq, LANES), lambda b, h, i: (b, 30 + h // 2))],
        out_specs=pl.BlockSpec((SWA_TQ, LANES), lambda b, h, i: (b * nq + i, h)),
        out_shape=jax.ShapeDtypeStruct((n, GROUP_WIDTH), BF16),
        compiler_params=_cparams("parallel", "parallel", "arbitrary"),
        name="swa_attention",
    )(sinks, z, z, z)


def _dot_f32(x, w):
    xh = x.astype(BF16)
    xl = (x - xh.astype(F32)).astype(BF16)
    wh = w.astype(BF16)
    wl = (w - wh.astype(F32)).astype(BF16)
    d = lambda a, b: jnp.dot(a, b, preferred_element_type=F32)
    return d(xh, wh) + (d(xh, wl) + d(xl, wh))


def _top2(logits):
    lane = lax.broadcasted_iota(jnp.int32, logits.shape, 1)
    l1 = jnp.where(lane < N_EXPERTS, logits, MASKED)
    m1 = jnp.max(l1, axis=-1, keepdims=True)
    i1 = jnp.min(jnp.where(l1 == m1, lane, LANES), axis=-1, keepdims=True)
    l2 = jnp.where(lane == i1, MASKED, l1)
    m2 = jnp.max(l2, axis=-1, keepdims=True)
    i2 = jnp.min(jnp.where(l2 == m2, lane, LANES), axis=-1, keepdims=True)
    e2 = jnp.exp(m2 - m1)
    den = 1.0 + e2
    out = jnp.where(lane == 0, i1.astype(F32), 0.0)
    out = jnp.where(lane == 1, i2.astype(F32), out)
    out = jnp.where(lane == 2, 1.0 / den, out)
    return jnp.where(lane == 3, e2 / den, out)


def _outproj_kernel(*refs, routed):
    if routed:
        md, mn, ms, ml, w_ref, h_ref, g_ref, wr_ref, h1_ref, hn_ref, route_ref = refs
    else:
        md, mn, ms, ml, w_ref, h_ref, g_ref, h1_ref, hn_ref = refs
    acc = h_ref[...]
    for k, m in enumerate((md, mn, ms, ml)):
        acc = acc + jnp.dot(m[...], w_ref[k * GROUP_WIDTH:(k + 1) * GROUP_WIDTH, :],
                            preferred_element_type=F32)
    h1_ref[...] = acc
    hn = _rms(acc, g_ref[...])
    if routed:
        _slab_store(hn_ref, hn)
        route_ref[...] = _top2(_dot_f32(hn, wr_ref[...]))
    else:
        hn_ref[...] = hn.astype(hn_ref.dtype)


def _outproj(mixed, w, h, g, w_router=None, tm=256):
    n, d = h.shape
    tm = _tile(n, tm)
    routed = w_router is not None
    row = lambda width: pl.BlockSpec((tm, width), lambda i: (i, 0))
    full = lambda a: pl.BlockSpec(a.shape, lambda i: (0, 0))
    g = g.reshape(1, d)
    args = list(mixed) + [w, h, g]
    in_specs = [row(GROUP_WIDTH)] * 4 + [full(w), row(d), full(g)]
    if routed:
        args.append(w_router)
        in_specs.append(full(w_router))
        out_specs = [row(d), _slab_spec(tm, d, lambda i: (i, 0)), row(LANES)]
        out_shape = [jax.ShapeDtypeStruct((n, d), F32), _slab_shape(n, d),
                     jax.ShapeDtypeStruct((n, LANES), F32)]
    else:
        out_specs = [row(d), row(d)]
        out_shape = [jax.ShapeDtypeStruct((n, d), F32), jax.ShapeDtypeStruct((n, d), BF16)]
    return pl.pallas_call(
        functools.partial(_outproj_kernel, routed=routed),
        grid=(n // tm,),
        in_specs=in_specs, out_specs=out_specs, out_shape=out_shape,
        compiler_params=_cparams("parallel"),
        name="outproj_routed" if routed else "outproj",
    )(*args)


def _swiglu_step(x, wg_ref, wu_ref, wd_ref, o_ref, f):
    @pl.when(f == 0)
    def _():
        o_ref[...] = jnp.zeros(o_ref.shape, o_ref.dtype)

    g = jnp.dot(x, wg_ref[...], preferred_element_type=F32)
    u = jnp.dot(x, wu_ref[...], preferred_element_type=F32)
    a = (g * jax.nn.sigmoid(g) * u).astype(BF16)
    o_ref[...] += jnp.dot(a, wd_ref[...], preferred_element_type=F32)


def _ffn_kernel(x_ref, wg_ref, wu_ref, wd_ref, o_ref):
    _swiglu_step(x_ref[...], wg_ref, wu_ref, wd_ref, o_ref, pl.program_id(1))


def _dense_ffn(x, wg, wu, wd, tm=512, tf=1024):
    n, d = x.shape
    ff = wg.shape[1]
    tm, tf = _tile(n, tm), _tile(ff, tf)
    return pl.pallas_call(
        _ffn_kernel,
        grid=(n // tm, ff // tf),
        in_specs=[pl.BlockSpec((tm, d), lambda i, f: (i, 0)),
                  pl.BlockSpec((d, tf), lambda i, f: (0, f)),
                  pl.BlockSpec((d, tf), lambda i, f: (0, f)),
                  pl.BlockSpec((tf, d), lambda i, f: (f, 0))],
        out_specs=pl.BlockSpec((tm, d), lambda i, f: (i, 0)),
        out_shape=jax.ShapeDtypeStruct((n, d), F32),
        compiler_params=_cparams("parallel", "arbitrary"),
        name="dense_ffn",
    )(x, wg, wu, wd)


def _moe_kernel(blk_exp_ref, n_used_ref, x_ref, wg_ref, wu_ref, wd_ref, o_ref, xb_ref, acc_ref):
    b, f = pl.program_id(0), pl.program_id(1)
    tm, d = xb_ref.shape
    used = b < n_used_ref[0]
    last = f == pl.num_programs(1) - 1

    @pl.when(used & (f == 0))
    def _():
        for c in range(d // LANES):
            xb_ref[:, c * LANES:(c + 1) * LANES] = _slab_chunk(x_ref, c, tm, d).astype(BF16)

    @pl.when(used)
    def _():
        _swiglu_step(xb_ref[...], wg_ref, wu_ref, wd_ref, acc_ref, f)

    @pl.when(used & last)
    def _():
        _slab_store(o_ref, acc_ref[...])

    @pl.when(jnp.logical_not(used) & last)
    def _():
        o_ref[...] = jnp.zeros(o_ref.shape, o_ref.dtype)


def _moe_ffn(xs, blk_exp, n_used, wg, wu, wd, tf=1024):
    d, ff = wg.shape[1], wg.shape[2]
    n_slots = xs.shape[0] // (d // LANES)
    tf = _tile(ff, tf)
    nf = ff // tf
    n_blk = n_slots // MOE_TM

    def blk(b, nu):
        return jnp.minimum(b, nu[0] - 1)

    def col(b, f, nu):
        return jnp.where(b < nu[0], f, nf - 1)

    grid_spec = pltpu.PrefetchScalarGridSpec(
        num_scalar_prefetch=2,
        grid=(n_blk, nf),
        in_specs=[_slab_spec(MOE_TM, d, lambda b, f, be, nu: (blk(b, nu), 0)),
                  pl.BlockSpec((None, d, tf), lambda b, f, be, nu: (be[blk(b, nu)], 0, col(b, f, nu))),
                  pl.BlockSpec((None, d, tf), lambda b, f, be, nu: (be[blk(b, nu)], 0, col(b, f, nu))),
                  pl.BlockSpec((None, tf, d), lambda b, f, be, nu: (be[blk(b, nu)], col(b, f, nu), 0))],
        out_specs=_slab_spec(MOE_TM, d, lambda b, f, be, nu: (b, 0)),
        scratch_shapes=[pltpu.VMEM((MOE_TM, d), BF16), pltpu.VMEM((MOE_TM, d), F32)],
    )
    return pl.pallas_call(
        _moe_kernel,
        grid_spec=grid_spec,
        out_shape=_slab_shape(n_slots, d),
        compiler_params=_cparams("arbitrary", "arbitrary"),
        name="moe_ffn",
    )(blk_exp, n_used, xs, wg, wu, wd)


def _gather_kernel(idx_ref, src_ref, o_ref, sem, *, chunks):
    def copy(r):
        return pltpu.make_async_copy(src_ref.at[pl.ds(idx_ref[0, 0, r] * chunks, chunks)],
                                     o_ref.at[pl.ds(r * chunks, chunks)], sem)

    def start(r, carry):
        copy(r).start()
        return carry

    def wait(r, carry):
        copy(r).wait()
        return carry

    lax.fori_loop(0, GATHER_ROWS, start, 0, unroll=8)
    lax.fori_loop(0, GATHER_ROWS, wait, 0, unroll=8)


def _gather_rows(src, idx, width):
    n_out = idx.shape[0]
    assert n_out % GATHER_ROWS == 0
    steps = n_out // GATHER_ROWS
    return pl.pallas_call(
        functools.partial(_gather_kernel, chunks=width // LANES),
        grid=(steps,),
        in_specs=[pl.BlockSpec((1, 1, GATHER_ROWS), lambda i: (i, 0, 0), memory_space=pltpu.SMEM),
                  pl.BlockSpec(memory_space=pl.ANY)],
        out_specs=_slab_spec(GATHER_ROWS, width, lambda i: (i, 0)),
        out_shape=_slab_shape(n_out, width),
        scratch_shapes=[pltpu.SemaphoreType.DMA(())],
        compiler_params=_cparams("arbitrary"),
        name="gather_rows",
    )(idx.reshape(steps, 1, GATHER_ROWS), src)


def _ple_kernel(*refs, routed, last):
    if routed:
        h1_ref, y0_ref, y1_ref, route_ref, p_ref, gp_ref, wg_ref, wp_ref, gn_ref = refs[:9]
        outs, mix_ref = refs[9:-1], refs[-1]
        tm, d = h1_ref.shape
        route = route_ref[...]
        g0, g1 = route[:, 2:3], route[:, 3:4]
        for c in range(d // LANES):
            mix_ref[:, c * LANES:(c + 1) * LANES] = (g0 * _slab_chunk(y0_ref, c, tm, d)
                                                     + g1 * _slab_chunk(y1_ref, c, tm, d))
        h2 = h1_ref[...] + mix_ref[...]
    else:
        h1_ref, y_ref, p_ref, gp_ref, wg_ref, wp_ref, gn_ref = refs[:7]
        outs = refs[7:]
        h2 = h1_ref[...] + y_ref[...]
    hp = _rms(h2, gp_ref[...]).astype(BF16)
    gate = jax.nn.sigmoid(jnp.dot(hp, wg_ref[...], preferred_element_type=F32))
    pp = jnp.dot(p_ref[...].astype(BF16), wp_ref[...], preferred_element_type=F32)
    h3 = h2 + gate * pp
    outs[0][...] = _rms(h3, gn_ref[...]) if last else h3


def _ple(h1, ys, route, p, g_ple, w_gate, w_proj, g_final, last, tm=256):
    n, d = h1.shape
    tm = _tile(n, tm)
    nt = n // tm
    routed = route is not None
    row = lambda width: pl.BlockSpec((tm, width), lambda i: (i, 0))
    full = lambda a: pl.BlockSpec(a.shape, lambda i: (0, 0))
    g_ple, g_final = g_ple.reshape(1, d), g_final.reshape(1, d)
    if routed:
        args = [h1, ys, ys, route]
        in_specs = [row(d), _slab_spec(tm, d, lambda i: (i, 0)), _slab_spec(tm, d, lambda i: (i + nt, 0)),
                    row(LANES)]
        scratch = [pltpu.VMEM((tm, d), F32)]
    else:
        args = [h1, ys]
        in_specs = [row(d), row(d)]
        scratch = []
    args += [p, g_ple, w_gate, w_proj, g_final]
    in_specs += [row(p.shape[1]), full(g_ple), full(w_gate), full(w_proj), full(g_final)]
    return pl.pallas_call(
        functools.partial(_ple_kernel, routed=routed, last=last),
        grid=(nt,),
        in_specs=in_specs, out_specs=[row(d)], out_shape=[jax.ShapeDtypeStruct((n, d), F32)],
        scratch_shapes=scratch,
        compiler_params=_cparams("parallel"),
        name="ple_routed" if routed else "ple",
    )(*args)


def _dispatch(route, n):
    idx = route[:, :2].astype(jnp.int32)
    flat_e = idx.reshape(-1)
    onehot = (flat_e[:, None] == jnp.arange(N_EXPERTS, dtype=jnp.int32)[None, :]).astype(jnp.int32)
    csum = jnp.cumsum(onehot, axis=0)
    rank = jnp.sum(onehot * csum, axis=1) - 1
    counts = csum[-1]
    padded = (counts + MOE_TM - 1) // MOE_TM * MOE_TM
    pad_end = jnp.cumsum(padded)
    pad_start = pad_end - padded
    dest = pad_start[flat_e] + rank
    n_blk = (2 * n) // MOE_TM + N_EXPERTS
    flat_tok = jnp.arange(2 * n, dtype=jnp.int32) // 2
    slot_tok = jnp.zeros((n_blk * MOE_TM,), jnp.int32).at[dest].set(flat_tok)
    blk_exp = jnp.minimum(jnp.searchsorted(pad_end, jnp.arange(n_blk, dtype=jnp.int32) * MOE_TM, side="right"),
                          N_EXPERTS - 1).astype(jnp.int32)
    n_used = (pad_end[-1] // MOE_TM).astype(jnp.int32).reshape(1)
    pos = dest.reshape(n, 2).T.reshape(-1).astype(jnp.int32)
    return slot_tok, blk_exp, n_used, pos


def _mla_weights(w_uq, w_ukv):
    lora = w_uq.shape[0]
    wq = w_uq.reshape(lora, GROUP_HEADS, MLA_NOPE + MLA_ROPE)
    wq = jnp.pad(wq, ((0, 0), (0, 0), (0, 2 * LANES - MLA_NOPE - MLA_ROPE)))
    wkv = w_ukv.reshape(lora, GROUP_HEADS, 2, HEAD_DIM).transpose(0, 2, 1, 3)
    return wq.reshape(lora, -1).astype(BF16), wkv.reshape(lora, -1).astype(BF16)


def kernel(x, p, attn_norm, w_in, diff_lq1, diff_lk1, diff_lq2, diff_lk2, diff_subln, na_rpb, swa_sinks,
           mla_q_norm, mla_kv_norm, mla_w_uq, mla_w_ukv, w_out, ffn_norm, dense_w_gate, dense_w_up,
           dense_w_down, moe_router, moe_w_gate, moe_w_up, moe_w_down, ple_norm, ple_gate, ple_proj,
           final_norm):
    batch, seq, d = x.shape
    n = batch * seq
    depth = w_in.shape[0]
    tabs = _rope_tables(seq)
    h = x.reshape(n, d)
    for i in range(depth):
        lambda_init = 0.8 - 0.6 * math.exp(-0.3 * i)
        w_in_i = jnp.pad(w_in[i].astype(BF16), ((0, 0), (0, IN_COLS_PAD - IN_COLS)))
        z = _inproj(h, attn_norm[i], w_in_i, tabs, seq)
        wq, wkv = _mla_weights(mla_w_uq[i], mla_w_ukv[i])
        q_l, k_l, v_l = _mla_up(z, mla_q_norm[i], mla_kv_norm[i], wq, wkv, tabs, seq)
        mixed = (
            _diff_attention(z, diff_lq1[i], diff_lk1[i], diff_lq2[i], diff_lk2[i], diff_subln[i],
                            lambda_init, batch, seq),
            _na_attention(z, na_rpb[i], batch, seq),
            _swa_attention(z, swa_sinks[i], batch, seq),
            _mla_attention(q_l, k_l, v_l, batch, seq),
        )
        j = i // 2
        routed = i % 2 == 1
        last = i == depth - 1
        if not routed:
            h1, hn = _outproj(mixed, w_out[i].astype(BF16), h, ffn_norm[i])
            y = _dense_ffn(hn, dense_w_gate[j].astype(BF16), dense_w_up[j].astype(BF16),
                           dense_w_down[j].astype(BF16))
            route = None
        else:
            w_router = jnp.pad(moe_router[j], ((0, 0), (0, LANES - N_EXPERTS)))
            h1, hn, route = _outproj(mixed, w_out[i].astype(BF16), h, ffn_norm[i], w_router)
            slot_tok, blk_exp, n_used, pos = _dispatch(route, n)
            xs = _gather_rows(hn, slot_tok, d)
            ys = _moe_ffn(xs, blk_exp, n_used, moe_w_gate[j].astype(BF16), moe_w_up[j].astype(BF16),
                          moe_w_down[j].astype(BF16))
            y = _gather_rows(ys, pos, d)
        (h,) = _ple(h1, y, route, p[i].reshape(n, -1), ple_norm[i], ple_gate[i].astype(BF16),
                    ple_proj[i].astype(BF16), final_norm, last)
    return h.reshape(batch, seq, d)
```

```python
import functools
import math

import numpy as np
import jax
import jax.numpy as jnp
from jax import lax
from jax.experimental import pallas as pl
from jax.experimental.pallas import tpu as pltpu

F32 = jnp.float32
BF16 = jnp.bfloat16

D_MODEL = 2048
HEAD_DIM = 128
GROUP_HEADS = 4
GROUP_WIDTH = GROUP_HEADS * HEAD_DIM
ROPE_THETA = 10000.0
NORM_EPS = 1e-6
DIFF_QK_DIM = 64
GRID_W = 64
NA_WIN_ROWS = 8
NA_WIN_COLS = 16
SWA_WINDOW = 128
MLA_LORA = 512
MLA_NOPE = 128
MLA_ROPE = 64
IN_COLS = 5184
FFN_DIM = 7168
N_EXPERTS = 8
PLE_DIM = 256

LANES = 128
VMEM_LIMIT = 56 * 1024 * 1024
MASKED = -1e30

IN_TN = 512
IN_COLS_PAD = 11 * IN_TN
NA_GROUP_ROWS = 4
NA_KEY_ROWS = NA_GROUP_ROWS + NA_WIN_ROWS
LOG2E = math.log2(math.e)
SWA_TQ = 1024
SWA_SUB = 256
SWA_SPAN = SWA_SUB + 2 * SWA_WINDOW
NA_UNROLL = 4
MOE_TM = 512


def _tile(n, pref):
    t = min(n, pref)
    assert n % t == 0, (n, t)
    return t


def _cparams(*sem):
    return pltpu.CompilerParams(dimension_semantics=sem, vmem_limit_bytes=VMEM_LIMIT)


def _rms(x, g):
    return x * lax.rsqrt(jnp.mean(x * x, axis=-1, keepdims=True) + NORM_EPS) * g


def _slab_shape(rows, width):
    return jax.ShapeDtypeStruct((rows * (width // LANES), LANES), F32)


def _slab_spec(tm, width, index_map):
    return pl.BlockSpec((tm * (width // LANES), LANES), index_map)


def _slab_chunk(ref, c, tm, width):
    return ref[pl.ds(c, tm, stride=width // LANES), :]


def _slab_store(ref, x):
    tm, width = x.shape
    for c in range(width // LANES):
        ref[pl.ds(c, tm, stride=width // LANES), :] = x[:, c * LANES:(c + 1) * LANES]


def _rope_tables(seq):
    def base(dim):
        inv = ROPE_THETA ** (-jnp.arange(0, dim, 2, dtype=F32) / dim)
        ang = jnp.arange(seq, dtype=F32)[:, None] * inv[None, :]
        ang = jnp.concatenate([ang, ang], axis=-1)
        return jnp.cos(ang), jnp.sin(ang)

    lane = np.arange(LANES)
    cos, sin = base(DIFF_QK_DIM)
    c64 = jnp.concatenate([cos, cos], axis=-1)
    s64 = jnp.concatenate([sin, sin], axis=-1)
    lo = jnp.asarray((lane % 64) < 32)
    sa64 = jnp.where(lo, -s64, 0.0)
    sb64 = jnp.where(lo, 0.0, s64)
    first = jnp.asarray(lane < 64)
    c64p, sa64p, sb64p = (jnp.where(first, t, 0.0) for t in (c64, sa64, sb64))
    cos, sin = base(HEAD_DIM)
    s128 = jnp.where(jnp.asarray(lane < 64), -sin, sin)
    return dict(c64=c64, sa64=sa64, sb64=sb64, c64p=c64p, sa64p=sa64p, sb64p=sb64p,
                c128=cos, s128=s128)


def _rope64(x, c, sa, sb):
    return x * c + pltpu.roll(x, 96, 1) * sa + pltpu.roll(x, 32, 1) * sb


def _rope128(x, c, s):
    return x * c + pltpu.roll(x, 64, 1) * s


def _inproj_kernel(h_ref, g_ref, w_ref, c64, sa64, sb64, c128, s128, o_ref):
    x = _rms(h_ref[...], g_ref[...]).astype(BF16)
    groups = IN_TN // LANES
    for j in range(IN_COLS_PAD // IN_TN):
        acc = jnp.dot(x, w_ref[:, j * IN_TN:(j + 1) * IN_TN], preferred_element_type=F32)
        for g in range(groups):
            a = acc[:, g * LANES:(g + 1) * LANES]
            if j == 0:
                a = _rope64(a, c64[...], sa64[...], sb64[...]) * (DIFF_QK_DIM ** -0.5 * LOG2E)
            elif j == 1:
                a = _rope64(a, c64[...], sa64[...], sb64[...])
            elif j == 3:
                a = a * HEAD_DIM ** -0.5
            elif j == 6:
                a = _rope128(a, c128[...], s128[...]) * HEAD_DIM ** -0.5
            elif j == 7 and g < 2:
                a = _rope128(a, c128[...], s128[...])
            o_ref[:, j * IN_TN + g * LANES:j * IN_TN + (g + 1) * LANES] = a.astype(o_ref.dtype)


def _inproj(h, g, w, tabs, seq, tm=512):
    n, d = h.shape
    tm = _tile(seq, tm)
    per_seq = seq // tm
    tab_spec = pl.BlockSpec((tm, LANES), lambda i: (i % per_seq, 0))
    g = g.reshape(1, d)
    return pl.pallas_call(
        _inproj_kernel,
        grid=(n // tm,),
        in_specs=[pl.BlockSpec((tm, d), lambda i: (i, 0)),
                  pl.BlockSpec(g.shape, lambda i: (0, 0)),
                  pl.BlockSpec(w.shape, lambda i: (0, 0), pipeline_mode=pl.Buffered(1))] + [tab_spec] * 5,
        out_specs=pl.BlockSpec((tm, IN_COLS_PAD), lambda i: (i, 0)),
        out_shape=jax.ShapeDtypeStruct((n, IN_COLS_PAD), BF16),
        compiler_params=_cparams("parallel"),
        name="inproj",
    )(h, g, w, tabs["c64"], tabs["sa64"], tabs["sb64"], tabs["c128"], tabs["s128"])


def _mla_up_kernel(cq_ref, ckv_ref, kr_ref, gq_ref, gkv_ref, wq_ref, wkv_ref, c_ref, sa_ref, sb_ref,
                   q_ref, k_ref, v_ref):
    scale = (MLA_NOPE + MLA_ROPE) ** -0.5 * LOG2E
    c, sa, sb = c_ref[...], sa_ref[...], sb_ref[...]
    cq = _rms(cq_ref[...].astype(F32), gq_ref[...]).astype(BF16)
    qa = jnp.dot(cq, wq_ref[...], preferred_element_type=F32)
    for h in range(GROUP_HEADS):
        base = h * 2 * LANES
        q_ref[:, base:base + LANES] = (qa[:, base:base + LANES] * scale).astype(q_ref.dtype)
        q_ref[:, base + LANES:base + 2 * LANES] = (
            _rope64(qa[:, base + LANES:base + 2 * LANES], c, sa, sb) * scale).astype(q_ref.dtype)
    ckv = _rms(ckv_ref[...].astype(F32), gkv_ref[...]).astype(BF16)
    kva = jnp.dot(ckv, wkv_ref[...], preferred_element_type=F32)
    kr = _rope64(kr_ref[...].astype(F32), c, sa, sb).astype(k_ref.dtype)
    for h in range(GROUP_HEADS):
        base = h * 2 * LANES
        k_ref[:, base:base + LANES] = kva[:, h * LANES:(h + 1) * LANES].astype(k_ref.dtype)
        k_ref[:, base + LANES:base + 2 * LANES] = kr
    v_ref[...] = kva[:, GROUP_WIDTH:].astype(v_ref.dtype)


def _mla_up(z, gq, gkv, wq, wkv, tabs, seq, tm=512):
    n = z.shape[0]
    tm = _tile(seq, tm)
    per_seq = seq // tm
    tab_spec = pl.BlockSpec((tm, LANES), lambda i: (i % per_seq, 0))
    full = lambda a: pl.BlockSpec(a.shape, lambda i: (0,) * a.ndim)
    gq = gq.reshape(1, -1)
    gkv = gkv.reshape(1, -1)
    return pl.pallas_call(
        _mla_up_kernel,
        grid=(n // tm,),
        in_specs=[pl.BlockSpec((tm, MLA_LORA), lambda i: (i, 8)),
                  pl.BlockSpec((tm, MLA_LORA), lambda i: (i, 9)),
                  pl.BlockSpec((tm, LANES), lambda i: (i, 40)),
                  full(gq), full(gkv), full(wq), full(wkv), tab_spec, tab_spec, tab_spec],
        out_specs=[pl.BlockSpec((tm, 2 * GROUP_WIDTH), lambda i: (i, 0)),
                   pl.BlockSpec((tm, 2 * GROUP_WIDTH), lambda i: (i, 0)),
                   pl.BlockSpec((tm, GROUP_WIDTH), lambda i: (i, 0))],
        out_shape=[jax.ShapeDtypeStruct((n, 2 * GROUP_WIDTH), BF16),
                   jax.ShapeDtypeStruct((n, 2 * GROUP_WIDTH), BF16),
                   jax.ShapeDtypeStruct((n, GROUP_WIDTH), BF16)],
        compiler_params=_cparams("parallel"),
        name="mla_up",
    )(z, z, z, gq, gkv, wq, wkv, tabs["c64p"], tabs["sa64p"], tabs["sb64p"])


def _qk(q, k):
    return lax.dot_general(q, k, (((1,), (1,)), ((), ())), preferred_element_type=F32)


def _flash_core(q_ref, k_ref, v_ref, vt_ref, s_ref, m_ref, l_ref, acc_ref, tk):
    nt = k_ref.shape[0] // tk
    assert nt % 2 == 0

    @pl.when(pl.program_id(2) == 0)
    def _():
        vt_ref[...] = v_ref[...].T

    m_ref[...] = jnp.full(m_ref.shape, MASKED, F32)
    l_ref[...] = jnp.zeros(l_ref.shape, F32)
    acc_ref[...] = jnp.zeros(acc_ref.shape, F32)

    def scores(t, slot):
        off = pl.multiple_of(jnp.minimum(t, nt - 1) * tk, tk)
        s_ref[slot] = _qk(k_ref[pl.ds(off, tk), :], q_ref[...])

    def consume(t, slot):
        off = pl.multiple_of(t * tk, tk)
        s = s_ref[slot]
        m_prev = m_ref[...]
        m_new = jnp.maximum(m_prev, jnp.max(s, axis=0, keepdims=True))
        alpha = jnp.exp2(m_prev - m_new)
        p = jnp.exp2(s - m_new)
        l_ref[...] = alpha * l_ref[...] + jnp.sum(p, axis=0, keepdims=True)
        acc_ref[...] = alpha * acc_ref[...] + jnp.dot(vt_ref[:, pl.ds(off, tk)], p.astype(vt_ref.dtype),
                                                      preferred_element_type=F32)
        m_ref[...] = m_new

    scores(0, 0)

    def body(j, carry):
        scores(2 * j + 1, 1)
        consume(2 * j, 0)
        scores(2 * j + 2, 0)
        consume(2 * j + 1, 1)
        return carry

    lax.fori_loop(0, nt // 2, body, 0)


def _flash_scratch(seq, tk, queries):
    return [pltpu.VMEM((LANES, seq), BF16), pltpu.VMEM((2, tk, queries), F32),
            pltpu.VMEM((1, queries), F32), pltpu.VMEM((1, queries), F32), pltpu.VMEM((LANES, queries), F32)]


def _flash_kernel(q_ref, k_ref, v_ref, o_ref, vt_ref, s_ref, m_ref, l_ref, acc_ref, *, tk):
    _flash_core(q_ref, k_ref, v_ref, vt_ref, s_ref, m_ref, l_ref, acc_ref, tk)
    o_ref[...] = (acc_ref[...] / l_ref[...]).T.astype(o_ref.dtype)


def _mla_attention(q, k, v, batch, seq, tq=512, tk=512):
    n = q.shape[0]
    tq, tk = _tile(seq, tq), _tile(seq, tk)
    nq = seq // tq
    dq = 2 * LANES
    return pl.pallas_call(
        functools.partial(_flash_kernel, tk=tk),
        grid=(batch, GROUP_HEADS, nq),
        in_specs=[pl.BlockSpec((tq, dq), lambda b, h, i: (b * nq + i, h)),
                  pl.BlockSpec((seq, dq), lambda b, h, i: (b, h)),
                  pl.BlockSpec((seq, LANES), lambda b, h, i: (b, h))],
        out_specs=pl.BlockSpec((tq, LANES), lambda b, h, i: (b * nq + i, h)),
        out_shape=jax.ShapeDtypeStruct((n, GROUP_WIDTH), BF16),
        scratch_shapes=_flash_scratch(seq, tk, tq),
        compiler_params=_cparams("parallel", "parallel", "arbitrary"),
        name="mla_attention",
    )(q, k, v)


def _diff_kernel(lq1_ref, lk1_ref, lq2_ref, lk2_ref, g_ref, q_ref, k_ref, v_ref, o_ref,
                 qq_ref, vt_ref, s_ref, m_ref, l_ref, acc_ref, *, tk, lambda_init):
    tq = q_ref.shape[0]
    q = q_ref[...]
    lane = lax.broadcasted_iota(jnp.int32, q.shape, 1)
    qq_ref[:tq, :] = jnp.where(lane < DIFF_QK_DIM, q, jnp.zeros_like(q))
    qq_ref[tq:, :] = jnp.where(lane >= DIFF_QK_DIM, q, jnp.zeros_like(q))
    _flash_core(qq_ref, k_ref, v_ref, vt_ref, s_ref, m_ref, l_ref, acc_ref, tk)
    lam = (jnp.exp(jnp.sum(lq1_ref[...] * lk1_ref[...], axis=-1, keepdims=True))
           - jnp.exp(jnp.sum(lq2_ref[...] * lk2_ref[...], axis=-1, keepdims=True)) + lambda_init)
    o = (acc_ref[:, :tq] / l_ref[:, :tq] - lam * (acc_ref[:, tq:] / l_ref[:, tq:])).T
    o_ref[...] = (_rms(o, g_ref[...]) * (1.0 - lambda_init)).astype(o_ref.dtype)


def _diff_attention(z, lq1, lk1, lq2, lk2, subln, lambda_init, batch, seq, tq=512, tk=512):
    n = z.shape[0]
    tq, tk = _tile(seq, tq), _tile(seq, tk)
    nq = seq // tq
    vec = lambda a: a.reshape(1, -1)
    small = lambda a: pl.BlockSpec(a.shape, lambda b, h, i: (0, 0))
    params = [vec(a) for a in (lq1, lk1, lq2, lk2, subln)]
    return pl.pallas_call(
        functools.partial(_diff_kernel, tk=tk, lambda_init=lambda_init),
        grid=(batch, GROUP_HEADS, nq),
        in_specs=[small(a) for a in params] + [
            pl.BlockSpec((tq, LANES), lambda b, h, i: (b * nq + i, h)),
            pl.BlockSpec((seq, LANES), lambda b, h, i: (b, 4 + h)),
            pl.BlockSpec((seq, LANES), lambda b, h, i: (b, 8 + h))],
        out_specs=pl.BlockSpec((tq, LANES), lambda b, h, i: (b * nq + i, h)),
        out_shape=jax.ShapeDtypeStruct((n, GROUP_WIDTH), BF16),
        scratch_shapes=[pltpu.VMEM((2 * tq, LANES), BF16)] + _flash_scratch(seq, tk, 2 * tq),
        compiler_params=_cparams("parallel", "parallel", "arbitrary"),
        name="diff_attention",
    )(*params, z, z, z)


def _na_bias_tables(rpb, rows):
    groups = rows // NA_GROUP_ROWS
    assert groups >= 3
    qc = np.arange(GRID_W)
    kc = np.arange(GRID_W)
    ws = np.clip(qc - NA_WIN_COLS // 2, 0, GRID_W - NA_WIN_COLS)
    col_ok = (kc[None, :] >= ws[:, None]) & (kc[None, :] < ws[:, None] + NA_WIN_COLS)
    edge = GRID_W - NA_WIN_COLS
    rp = jnp.pad(rpb.astype(F32), ((0, 0), (0, 0), (edge, edge)))
    toep = jnp.stack([rp[:, :, GRID_W - 1 - c:2 * GRID_W - 1 - c] for c in range(GRID_W)], axis=2)
    toep = jnp.where(jnp.asarray(col_ok)[None, None], toep, MASKED)
    masked = jnp.full((rpb.shape[0], GRID_W, GRID_W), MASKED, F32)
    tabs = []
    for m in (0, 1, groups - 1):
        kstart = np.clip(NA_GROUP_ROWS * m - NA_WIN_ROWS // 2, 0, rows - NA_KEY_ROWS)
        qr = NA_GROUP_ROWS * m + np.arange(NA_GROUP_ROWS)
        kr = kstart + np.arange(NA_KEY_ROWS)
        rs = np.clip(qr - NA_WIN_ROWS // 2, 0, rows - NA_WIN_ROWS)
        row_ok = (kr[None, :] >= rs[:, None]) & (kr[None, :] < rs[:, None] + NA_WIN_ROWS)
        ridx = kr[None, :] - qr[:, None] + NA_WIN_ROWS - 1
        tabs.append(jnp.concatenate([
            jnp.concatenate([toep[:, ridx[i, a]] if row_ok[i, a] else masked
                             for a in range(NA_KEY_ROWS)], axis=-1)
            for i in range(NA_GROUP_ROWS)], axis=1))
    return jnp.stack(tabs)


def _na_kernel(bias_ref, q_ref, k_ref, v_ref, o_ref, *, rows):
    groups = rows // NA_GROUP_ROWS
    tq = NA_GROUP_ROWS * GRID_W
    span = NA_KEY_ROWS * GRID_W

    def body(m, carry):
        cls = jnp.where(m == 0, 0, jnp.where(m == groups - 1, 2, 1))
        krow = jnp.clip(NA_GROUP_ROWS * m - NA_WIN_ROWS // 2, 0, rows - NA_KEY_ROWS)
        koff = pl.multiple_of(krow * GRID_W, GRID_W)
        qoff = pl.multiple_of(m * tq, tq)
        s = _qk(q_ref[pl.ds(qoff, tq), :], k_ref[pl.ds(koff, span), :]) + bias_ref[cls, 0]
        p = jnp.exp(s - jnp.max(s, axis=-1, keepdims=True))
        l = jnp.sum(p, axis=-1, keepdims=True)
        v = v_ref[pl.ds(koff, span), :]
        o = jnp.dot(p.astype(v.dtype), v, preferred_element_type=F32) / l
        o_ref[pl.ds(qoff, tq), :] = o.astype(o_ref.dtype)
        return carry

    lax.fori_loop(0, groups, body, 0, unroll=NA_UNROLL)


def _na_attention(z, rpb, batch, seq):
    n = z.shape[0]
    rows = seq // GRID_W
    bias = _na_bias_tables(rpb, rows)
    tq, span = NA_GROUP_ROWS * GRID_W, NA_KEY_ROWS * GRID_W
    return pl.pallas_call(
        functools.partial(_na_kernel, rows=rows),
        grid=(batch, GROUP_HEADS),
        in_specs=[pl.BlockSpec((3, 1, tq, span), lambda b, h: (0, h, 0, 0)),
                  pl.BlockSpec((seq, LANES), lambda b, h: (b, 12 + h)),
                  pl.BlockSpec((seq, LANES), lambda b, h: (b, 16 + h)),
                  pl.BlockSpec((seq, LANES), lambda b, h: (b, 20 + h))],
        out_specs=pl.BlockSpec((seq, LANES), lambda b, h: (b, h)),
        out_shape=jax.ShapeDtypeStruct((n, GROUP_WIDTH), BF16),
        compiler_params=_cparams("parallel", "parallel"),
        name="na_attention",
    )(bias, z, z, z)


def _swa_kernel(sink_ref, q_ref, k_ref, v_ref, o_ref, *, seq):
    sink = sink_ref[pl.program_id(1)]
    for c in range(SWA_TQ // SWA_SUB):
        rows = slice(c * SWA_SUB, (c + 1) * SWA_SUB)
        q0 = pl.program_id(2) * SWA_TQ + c * SWA_SUB
        koff = pl.multiple_of(jnp.clip(q0 - SWA_WINDOW, 0, seq - SWA_SPAN), SWA_WINDOW)
        s = _qk(q_ref[rows, :], k_ref[pl.ds(koff, SWA_SPAN), :])
        qpos = q0 + lax.broadcasted_iota(jnp.int32, s.shape, 0)
        kpos = koff + lax.broadcasted_iota(jnp.int32, s.shape, 1)
        s = jnp.where(jnp.abs(qpos - kpos) <= SWA_WINDOW, s, MASKED)
        m = jnp.maximum(jnp.max(s, axis=-1, keepdims=True), sink)
        p = jnp.exp(s - m)
        l = jnp.sum(p, axis=-1, keepdims=True) + jnp.exp(sink - m)
        v = v_ref[pl.ds(koff, SWA_SPAN), :]
        o_ref[rows, :] = (jnp.dot(p.astype(v.dtype), v, preferred_element_type=F32) / l).astype(o_ref.dtype)


def _swa_attention(z, sinks, batch, seq):
    n = z.shape[0]
    assert seq % SWA_TQ == 0 and seq >= SWA_SPAN
    nq = seq // SWA_TQ
    return pl.pallas_call(
        functools.partial(_swa_kernel, seq=seq),
        grid=(batch, GROUP_HEADS, nq),
        in_specs=[pl.BlockSpec(memory_space=pltpu.SMEM),
                  pl.BlockSpec((SWA_TQ, LANES), lambda b, h, i: (b * nq + i, 24 + h)),
                  pl.BlockSpec((seq, LANES), lambda b, h, i: (b, 28 + h // 2)),
                  pl.BlockSpec((seq, LANES), lambda b, h, i: (b, 30 + h // 2))],
        out_specs=pl.BlockSpec((SWA_TQ, LANES), lambda b, h, i: (b * nq + i, h)),
        out_shape=jax.ShapeDtypeStruct((n, GROUP_WIDTH), BF16),
        compiler_params=_cparams("parallel", "parallel", "arbitrary"),
        name="swa_attention",
    )(sinks, z, z, z)


def _dot_f32(x, w):
    xh = x.astype(BF16)
    xl = (x - xh.astype(F32)).astype(BF16)
    wh = w.astype(BF16)
    wl = (w - wh.astype(F32)).astype(BF16)
    d = lambda a, b: jnp.dot(a, b, preferred_element_type=F32)
    return d(xh, wh) + (d(xh, wl) + d(xl, wh))


def _top2(logits):
    lane = lax.broadcasted_iota(jnp.int32, logits.shape, 1)
    l1 = jnp.where(lane < N_EXPERTS, logits, MASKED)
    m1 = jnp.max(l1, axis=-1, keepdims=True)
    i1 = jnp.min(jnp.where(l1 == m1, lane, LANES), axis=-1, keepdims=True)
    l2 = jnp.where(lane == i1, MASKED, l1)
    m2 = jnp.max(l2, axis=-1, keepdims=True)
    i2 = jnp.min(jnp.where(l2 == m2, lane, LANES), axis=-1, keepdims=True)
    e2 = jnp.exp(m2 - m1)
    den = 1.0 + e2
    out = jnp.where(lane == 0, i1.astype(F32), 0.0)
    out = jnp.where(lane == 1, i2.astype(F32), out)
    out = jnp.where(lane == 2, 1.0 / den, out)
    return jnp.where(lane == 3, e2 / den, out)


def _outproj_kernel(*refs, routed):
    if routed:
        md, mn, ms, ml, w_ref, h_ref, g_ref, wr_ref, h1_ref, hn_ref, route_ref = refs
    else:
        md, mn, ms, ml, w_ref, h_ref, g_ref, h1_ref, hn_ref = refs
    acc = h_ref[...]
    for k, m in enumerate((md, mn, ms, ml)):
        acc = acc + jnp.dot(m[...], w_ref[k * GROUP_WIDTH:(k + 1) * GROUP_WIDTH, :],
                            preferred_element_type=F32)
    h1_ref[...] = acc
    hn = _rms(acc, g_ref[...])
    if routed:
        _slab_store(hn_ref, hn)
        route_ref[...] = _top2(_dot_f32(hn, wr_ref[...]))
    else:
        hn_ref[...] = hn.astype(hn_ref.dtype)


def _outproj(mixed, w, h, g, w_router=None, tm=256):
    n, d = h.shape
    tm = _tile(n, tm)
    routed = w_router is not None
    row = lambda width: pl.BlockSpec((tm, width), lambda i: (i, 0))
    full = lambda a: pl.BlockSpec(a.shape, lambda i: (0, 0))
    g = g.reshape(1, d)
    args = list(mixed) + [w, h, g]
    in_specs = [row(GROUP_WIDTH)] * 4 + [full(w), row(d), full(g)]
    if routed:
        args.append(w_router)
        in_specs.append(full(w_router))
        out_specs = [row(d), _slab_spec(tm, d, lambda i: (i, 0)), row(LANES)]
        out_shape = [jax.ShapeDtypeStruct((n, d), F32), _slab_shape(n, d),
                     jax.ShapeDtypeStruct((n, LANES), F32)]
    else:
        out_specs = [row(d), row(d)]
        out_shape = [jax.ShapeDtypeStruct((n, d), F32), jax.ShapeDtypeStruct((n, d), BF16)]
    return pl.pallas_call(
        functools.partial(_outproj_kernel, routed=routed),
        grid=(n // tm,),
        in_specs=in_specs, out_specs=out_specs, out_shape=out_shape,
        compiler_params=_cparams("parallel"),
        name="outproj_routed" if routed else "outproj",
    )(*args)


def _swiglu_step(x, wg_ref, wu_ref, wd_ref, o_ref, f):
    @pl.when(f == 0)
    def _():
        o_ref[...] = jnp.zeros(o_ref.shape, o_ref.dtype)

    g = jnp.dot(x, wg_ref[...], preferred_element_type=F32)
    u = jnp.dot(x, wu_ref[...], preferred_element_type=F32)
    a = (g * jax.nn.sigmoid(g) * u).astype(BF16)
    o_ref[...] += jnp.dot(a, wd_ref[...], preferred_element_type=F32)


def _ffn_kernel(x_ref, wg_ref, wu_ref, wd_ref, o_ref):
    _swiglu_step(x_ref[...], wg_ref, wu_ref, wd_ref, o_ref, pl.program_id(1))


def _dense_ffn(x, wg, wu, wd, tm=512, tf=1024):
    n, d = x.shape
    ff = wg.shape[1]
    tm, tf = _tile(n, tm), _tile(ff, tf)
    return pl.pallas_call(
        _ffn_kernel,
        grid=(n // tm, ff // tf),
        in_specs=[pl.BlockSpec((tm, d), lambda i, f: (i, 0)),
                  pl.BlockSpec((d, tf), lambda i, f: (0, f)),
                  pl.BlockSpec((d, tf), lambda i, f: (0, f)),
                  pl.BlockSpec((tf, d), lambda i, f: (f, 0))],
        out_specs=pl.BlockSpec((tm, d), lambda i, f: (i, 0)),
        out_shape=jax.ShapeDtypeStruct((n, d), F32),
        compiler_params=_cparams("parallel", "arbitrary"),
        name="dense_ffn",
    )(x, wg, wu, wd)


def _moe_kernel(blk_exp_ref, n_used_ref, tok_ref, tok_next_ref, hn_ref, wg_ref, wu_ref, wd_ref, o_ref,
                stage_ref, sem, xb_ref, acc_ref):
    b, f = pl.program_id(0), pl.program_id(1)
    tm, d = xb_ref.shape
    chunks = d // LANES
    n_used = n_used_ref[0]
    used = b < n_used
    last = f == pl.num_programs(1) - 1

    def row_copy(tok, r, slot):
        return pltpu.make_async_copy(hn_ref.at[pl.ds(tok[0, 0, r] * chunks, chunks)],
                                     stage_ref.at[slot, pl.ds(r * chunks, chunks)], sem.at[slot])

    def start_block(tok, slot):
        def go(r, carry):
            row_copy(tok, r, slot).start()
            return carry
        lax.fori_loop(0, tm, go, 0, unroll=8)

    def wait_block(tok, slot):
        def go(r, carry):
            row_copy(tok, r, slot).wait()
            return carry
        lax.fori_loop(0, tm, go, 0, unroll=8)

    @pl.when(used & (b == 0) & (f == 0))
    def _():
        start_block(tok_ref, 0)

    @pl.when(used & (f == 0))
    def _():
        slot = b % 2
        wait_block(tok_ref, slot)
        for c in range(chunks):
            xb_ref[:, c * LANES:(c + 1) * LANES] = _slab_chunk(stage_ref.at[slot], c, tm, d).astype(BF16)

    @pl.when((b + 1 < n_used) & (f == 1))
    def _():
        start_block(tok_next_ref, (b + 1) % 2)

    @pl.when(used)
    def _():
        _swiglu_step(xb_ref[...], wg_ref, wu_ref, wd_ref, acc_ref, f)

    @pl.when(used & last)
    def _():
        _slab_store(o_ref, acc_ref[...])

    @pl.when(jnp.logical_not(used) & last)
    def _():
        o_ref[...] = jnp.zeros(o_ref.shape, o_ref.dtype)


def _moe_ffn(hn, slot_tok, blk_exp, n_used, wg, wu, wd, tf=1024):
    d, ff = wg.shape[1], wg.shape[2]
    n_slots = slot_tok.shape[0]
    tf = _tile(ff, tf)
    nf = ff // tf
    assert nf >= 2
    n_blk = n_slots // MOE_TM
    tok = slot_tok.reshape(n_blk, 1, MOE_TM)
    tok_spec = lambda nxt: pl.BlockSpec((1, 1, MOE_TM), lambda b, f, be, nu: (jnp.minimum(b + nxt, n_blk - 1), 0, 0),
                                        memory_space=pltpu.SMEM)

    def blk(b, nu):
        return jnp.minimum(b, nu[0] - 1)

    def col(b, f, nu):
        return jnp.where(b < nu[0], f, nf - 1)

    grid_spec = pltpu.PrefetchScalarGridSpec(
        num_scalar_prefetch=2,
        grid=(n_blk, nf),
        in_specs=[tok_spec(0), tok_spec(1), pl.BlockSpec(memory_space=pl.ANY),
                  pl.BlockSpec((None, d, tf), lambda b, f, be, nu: (be[blk(b, nu)], 0, col(b, f, nu))),
                  pl.BlockSpec((None, d, tf), lambda b, f, be, nu: (be[blk(b, nu)], 0, col(b, f, nu))),
                  pl.BlockSpec((None, tf, d), lambda b, f, be, nu: (be[blk(b, nu)], col(b, f, nu), 0))],
        out_specs=_slab_spec(MOE_TM, d, lambda b, f, be, nu: (b, 0)),
        scratch_shapes=[pltpu.VMEM((2, MOE_TM * (d // LANES), LANES), F32), pltpu.SemaphoreType.DMA((2,)),
                        pltpu.VMEM((MOE_TM, d), BF16), pltpu.VMEM((MOE_TM, d), F32)],
    )
    return pl.pallas_call(
        _moe_kernel,
        grid_spec=grid_spec,
        out_shape=_slab_shape(n_slots, d),
        compiler_params=_cparams("arbitrary", "arbitrary"),
        name="moe_ffn",
    )(blk_exp, n_used, tok, tok, hn, wg, wu, wd)


def _ple_kernel(*refs, routed, last):
    if routed:
        h1_ref, pos_ref, pos_next_ref, ys_ref, route_ref, p_ref, gp_ref, wg_ref, wp_ref, gn_ref = refs[:10]
        outs, (stage_ref, sem, mix_ref) = refs[10:-3], refs[-3:]
        tm, d = h1_ref.shape
        chunks = d // LANES
        i = pl.program_id(0)

        def row_copy(pos, r, slot):
            return pltpu.make_async_copy(ys_ref.at[pl.ds(pos[0, 0, r] * chunks, chunks)],
                                         stage_ref.at[slot, pl.ds(r * chunks, chunks)], sem.at[slot])

        def start_rows(pos, slot):
            def go(r, carry):
                row_copy(pos, r, slot).start()
                return carry
            lax.fori_loop(0, 2 * tm, go, 0, unroll=8)

        @pl.when(i == 0)
        def _():
            start_rows(pos_ref, 0)

        @pl.when(i + 1 < pl.num_programs(0))
        def _():
            start_rows(pos_next_ref, (i + 1) % 2)

        slot = i % 2

        def wait_row(r, carry):
            row_copy(pos_ref, r, slot).wait()
            return carry
        lax.fori_loop(0, 2 * tm, wait_row, 0, unroll=8)

        route = route_ref[...]
        g0, g1 = route[:, 2:3], route[:, 3:4]
        y_ref = stage_ref.at[slot]
        for c in range(chunks):
            mix_ref[:, c * LANES:(c + 1) * LANES] = (
                g0 * y_ref[pl.ds(c, tm, stride=chunks), :]
                + g1 * y_ref[pl.ds(tm * chunks + c, tm, stride=chunks), :])
        h2 = h1_ref[...] + mix_ref[...]
    else:
        h1_ref, y_ref, p_ref, gp_ref, wg_ref, wp_ref, gn_ref = refs[:7]
        outs = refs[7:]
        h2 = h1_ref[...] + y_ref[...]
    hp = _rms(h2, gp_ref[...]).astype(BF16)
    gate = jax.nn.sigmoid(jnp.dot(hp, wg_ref[...], preferred_element_type=F32))
    pp = jnp.dot(p_ref[...].astype(BF16), wp_ref[...], preferred_element_type=F32)
    h3 = h2 + gate * pp
    outs[0][...] = _rms(h3, gn_ref[...]) if last else h3


def _ple(h1, ys, route, p, g_ple, w_gate, w_proj, g_final, last, tm=256):
    n, d = h1.shape
    tm = _tile(n, tm)
    nt = n // tm
    routed = route is not None
    row = lambda width: pl.BlockSpec((tm, width), lambda i: (i, 0))
    full = lambda a: pl.BlockSpec(a.shape, lambda i: (0, 0))
    g_ple, g_final = g_ple.reshape(1, d), g_final.reshape(1, d)
    if routed:
        ys, pos = ys
        pos = pos.reshape(nt, tm, 2).transpose(0, 2, 1).reshape(nt, 1, 2 * tm)
        pos_spec = lambda nxt: pl.BlockSpec((1, 1, 2 * tm), lambda i: (jnp.minimum(i + nxt, nt - 1), 0, 0),
                                            memory_space=pltpu.SMEM)
        args = [h1, pos, pos, ys, route]
        in_specs = [row(d), pos_spec(0), pos_spec(1), pl.BlockSpec(memory_space=pl.ANY), row(LANES)]
        scratch = [pltpu.VMEM((2, 2 * tm * (d // LANES), LANES), F32), pltpu.SemaphoreType.DMA((2,)),
                   pltpu.VMEM((tm, d), F32)]
    else:
        args = [h1, ys]
        in_specs = [row(d), row(d)]
        scratch = []
    args += [p, g_ple, w_gate, w_proj, g_final]
    in_specs += [row(p.shape[1]), full(g_ple), full(w_gate), full(w_proj), full(g_final)]
    return pl.pallas_call(
        functools.partial(_ple_kernel, routed=routed, last=last),
        grid=(nt,),
        in_specs=in_specs, out_specs=[row(d)], out_shape=[jax.ShapeDtypeStruct((n, d), F32)],
        scratch_shapes=scratch,
        compiler_params=_cparams("arbitrary"),
        name="ple_routed" if routed else "ple",
    )(*args)


def _dispatch(route, n):
    idx = route[:, :2].astype(jnp.int32)
    flat_e = idx.reshape(-1)
    onehot = (flat_e[:, None] == jnp.arange(N_EXPERTS, dtype=jnp.int32)[None, :]).astype(jnp.int32)
    csum = jnp.cumsum(onehot, axis=0)
    rank = jnp.sum(onehot * csum, axis=1) - 1
    counts = csum[-1]
    padded = (counts + MOE_TM - 1) // MOE_TM * MOE_TM
    pad_end = jnp.cumsum(padded)
    pad_start = pad_end - padded
    dest = pad_start[flat_e] + rank
    n_blk = (2 * n) // MOE_TM + N_EXPERTS
    flat_tok = jnp.arange(2 * n, dtype=jnp.int32) // 2
    slot_tok = jnp.zeros((n_blk * MOE_TM,), jnp.int32).at[dest].set(flat_tok)
    blk_exp = jnp.minimum(jnp.searchsorted(pad_end, jnp.arange(n_blk, dtype=jnp.int32) * MOE_TM, side="right"),
                          N_EXPERTS - 1).astype(jnp.int32)
    n_used = (pad_end[-1] // MOE_TM).astype(jnp.int32).reshape(1)
    return slot_tok, blk_exp, n_used, dest.reshape(n, 2).astype(jnp.int32)


def _mla_weights(w_uq, w_ukv):
    lora = w_uq.shape[0]
    wq = w_uq.reshape(lora, GROUP_HEADS, MLA_NOPE + MLA_ROPE)
    wq = jnp.pad(wq, ((0, 0), (0, 0), (0, 2 * LANES - MLA_NOPE - MLA_ROPE)))
    wkv = w_ukv.reshape(lora, GROUP_HEADS, 2, HEAD_DIM).transpose(0, 2, 1, 3)
    return wq.reshape(lora, -1).astype(BF16), wkv.reshape(lora, -1).astype(BF16)


def kernel(x, p, attn_norm, w_in, diff_lq1, diff_lk1, diff_lq2, diff_lk2, diff_subln, na_rpb, swa_sinks,
           mla_q_norm, mla_kv_norm, mla_w_uq, mla_w_ukv, w_out, ffn_norm, dense_w_gate, dense_w_up,
           dense_w_down, moe_router, moe_w_gate, moe_w_up, moe_w_down, ple_norm, ple_gate, ple_proj,
           final_norm):
    batch, seq, d = x.shape
    n = batch * seq
    depth = w_in.shape[0]
    tabs = _rope_tables(seq)
    h = x.reshape(n, d)
    for i in range(depth):
        lambda_init = 0.8 - 0.6 * math.exp(-0.3 * i)
        w_in_i = jnp.pad(w_in[i].astype(BF16), ((0, 0), (0, IN_COLS_PAD - IN_COLS)))
        z = _inproj(h, attn_norm[i], w_in_i, tabs, seq)
        wq, wkv = _mla_weights(mla_w_uq[i], mla_w_ukv[i])
        q_l, k_l, v_l = _mla_up(z, mla_q_norm[i], mla_kv_norm[i], wq, wkv, tabs, seq)
        mixed = (
            _diff_attention(z, diff_lq1[i], diff_lk1[i], diff_lq2[i], diff_lk2[i], diff_subln[i],
                            lambda_init, batch, seq),
            _na_attention(z, na_rpb[i], batch, seq),
            _swa_attention(z, swa_sinks[i], batch, seq),
            _mla_attention(q_l, k_l, v_l, batch, seq),
        )
        j = i // 2
        routed = i % 2 == 1
        last = i == depth - 1
        if not routed:
            h1, hn = _outproj(mixed, w_out[i].astype(BF16), h, ffn_norm[i])
            y = _dense_ffn(hn, dense_w_gate[j].astype(BF16), dense_w_up[j].astype(BF16),
                           dense_w_down[j].astype(BF16))
            route = None
        else:
            w_router = jnp.pad(moe_router[j], ((0, 0), (0, LANES - N_EXPERTS)))
            h1, hn, route = _outproj(mixed, w_out[i].astype(BF16), h, ffn_norm[i], w_router)
            slot_tok, blk_exp, n_used, pos = _dispatch(route, n)
            ys = _moe_ffn(hn, slot_tok, blk_exp, n_used, moe_w_gate[j].astype(BF16),
                          moe_w_up[j].astype(BF16), moe_w_down[j].astype(BF16))
            y = (ys, pos)
        (h,) = _ple(h1, y, route, p[i].reshape(n, -1), ple_norm[i], ple_gate[i].astype(BF16),
                    ple_proj[i].astype(BF16), final_norm, last)
    return h.reshape(batch, seq, d)
```

```python
import functools
import math

import numpy as np
import jax
import jax.numpy as jnp
from jax import lax
from jax.experimental import pallas as pl
from jax.experimental.pallas import tpu as pltpu

F32 = jnp.float32
BF16 = jnp.bfloat16

HEAD_DIM = 128
GROUP_HEADS = 4
GROUP_WIDTH = GROUP_HEADS * HEAD_DIM
ROPE_THETA = 10000.0
NORM_EPS = 1e-6
DIFF_QK_DIM = 64
GRID_W = 64
NA_WIN_ROWS = 8
NA_WIN_COLS = 16
SWA_WINDOW = 128
MLA_LORA = 512
MLA_NOPE = 128
MLA_ROPE = 64
IN_COLS = 5184
N_EXPERTS = 8

LANES = 128
VMEM_LIMIT = 56 * 1024 * 1024
MASKED = -1e30
LOG2E = math.log2(math.e)

IN_TN = 512
IN_COLS_PAD = -(-IN_COLS // LANES) * LANES
NA_GROUP_ROWS = 4
NA_KEY_ROWS = NA_GROUP_ROWS + NA_WIN_ROWS
NA_UNROLL = 4
SWA_TQ = 1024
SWA_SUB = 256
SWA_SPAN = SWA_SUB + 2 * SWA_WINDOW
MOE_TM = 512


def _tile(n, pref):
    t = min(n, pref)
    assert n % t == 0, (n, t)
    return t


def _cparams(*sem):
    return pltpu.CompilerParams(dimension_semantics=sem, vmem_limit_bytes=VMEM_LIMIT)


def _rms(x, g):
    return x * lax.rsqrt(jnp.mean(x * x, axis=-1, keepdims=True) + NORM_EPS) * g


def _slab_shape(rows, width):
    return jax.ShapeDtypeStruct((rows * (width // LANES), LANES), F32)


def _slab_spec(tm, width, index_map):
    return pl.BlockSpec((tm * (width // LANES), LANES), index_map)


def _slab_chunk(ref, c, tm, width):
    return ref[pl.ds(c, tm, stride=width // LANES), :]


def _slab_store(ref, x):
    tm, width = x.shape
    for c in range(width // LANES):
        ref[pl.ds(c, tm, stride=width // LANES), :] = x[:, c * LANES:(c + 1) * LANES]


def _rope_tables(seq):
    def base(dim):
        inv = ROPE_THETA ** (-jnp.arange(0, dim, 2, dtype=F32) / dim)
        ang = jnp.arange(seq, dtype=F32)[:, None] * inv[None, :]
        ang = jnp.concatenate([ang, ang], axis=-1)
        return jnp.cos(ang), jnp.sin(ang)

    lane = np.arange(LANES)
    cos, sin = base(DIFF_QK_DIM)
    c64 = jnp.concatenate([cos, cos], axis=-1)
    s64 = jnp.concatenate([sin, sin], axis=-1)
    lo = jnp.asarray((lane % 64) < 32)
    sa64 = jnp.where(lo, -s64, 0.0)
    sb64 = jnp.where(lo, 0.0, s64)
    first = jnp.asarray(lane < 64)
    c64p, sa64p, sb64p = (jnp.where(first, t, 0.0) for t in (c64, sa64, sb64))
    cos, sin = base(HEAD_DIM)
    s128 = jnp.where(jnp.asarray(lane < 64), -sin, sin)
    return dict(c64=c64, sa64=sa64, sb64=sb64, c64p=c64p, sa64p=sa64p, sb64p=sb64p,
                c128=cos, s128=s128)


def _rope64(x, c, sa, sb):
    return x * c + pltpu.roll(x, 96, 1) * sa + pltpu.roll(x, 32, 1) * sb


def _rope128(x, c, s):
    return x * c + pltpu.roll(x, 64, 1) * s


def _inproj_kernel(h_ref, g_ref, w_ref, c64, sa64, sb64, c128, s128, o_ref):
    x = _rms(h_ref[...], g_ref[...]).astype(BF16)
    for j in range(-(-IN_COLS_PAD // IN_TN)):
        width = min(IN_TN, IN_COLS_PAD - j * IN_TN)
        acc = jnp.dot(x, w_ref[:, j * IN_TN:j * IN_TN + width], preferred_element_type=F32)
        for g in range(width // LANES):
            a = acc[:, g * LANES:(g + 1) * LANES]
            if j == 0:
                a = _rope64(a, c64[...], sa64[...], sb64[...]) * (DIFF_QK_DIM ** -0.5 * LOG2E)
            elif j == 1:
                a = _rope64(a, c64[...], sa64[...], sb64[...])
            elif j == 3:
                a = a * HEAD_DIM ** -0.5
            elif j == 6:
                a = _rope128(a, c128[...], s128[...]) * HEAD_DIM ** -0.5
            elif j == 7 and g < 2:
                a = _rope128(a, c128[...], s128[...])
            o_ref[:, j * IN_TN + g * LANES:j * IN_TN + (g + 1) * LANES] = a.astype(o_ref.dtype)


def _inproj(h, g, w, tabs, seq, tm=512):
    n, d = h.shape
    tm = _tile(seq, tm)
    per_seq = seq // tm
    tab_spec = pl.BlockSpec((tm, LANES), lambda i: (i % per_seq, 0))
    g = g.reshape(1, d)
    return pl.pallas_call(
        _inproj_kernel,
        grid=(n // tm,),
        in_specs=[pl.BlockSpec((tm, d), lambda i: (i, 0)),
                  pl.BlockSpec(g.shape, lambda i: (0, 0)),
                  pl.BlockSpec(w.shape, lambda i: (0, 0), pipeline_mode=pl.Buffered(1))] + [tab_spec] * 5,
        out_specs=pl.BlockSpec((tm, IN_COLS_PAD), lambda i: (i, 0)),
        out_shape=jax.ShapeDtypeStruct((n, IN_COLS_PAD), BF16),
        compiler_params=_cparams("parallel"),
        name="inproj",
    )(h, g, w, tabs["c64"], tabs["sa64"], tabs["sb64"], tabs["c128"], tabs["s128"])


def _mla_up_kernel(cq_ref, ckv_ref, kr_ref, gq_ref, gkv_ref, wq_ref, wkv_ref, c_ref, sa_ref, sb_ref,
                   q_ref, k_ref, v_ref):
    scale = (MLA_NOPE + MLA_ROPE) ** -0.5 * LOG2E
    c, sa, sb = c_ref[...], sa_ref[...], sb_ref[...]
    cq = _rms(cq_ref[...].astype(F32), gq_ref[...]).astype(BF16)
    qa = jnp.dot(cq, wq_ref[...], preferred_element_type=F32)
    for h in range(GROUP_HEADS):
        base = h * 2 * LANES
        q_ref[:, base:base + LANES] = (qa[:, base:base + LANES] * scale).astype(q_ref.dtype)
        q_ref[:, base + LANES:base + 2 * LANES] = (
            _rope64(qa[:, base + LANES:base + 2 * LANES], c, sa, sb) * scale).astype(q_ref.dtype)
    ckv = _rms(ckv_ref[...].astype(F32), gkv_ref[...]).astype(BF16)
    kva = jnp.dot(ckv, wkv_ref[...], preferred_element_type=F32)
    kr = _rope64(kr_ref[...].astype(F32), c, sa, sb).astype(k_ref.dtype)
    for h in range(GROUP_HEADS):
        base = h * 2 * LANES
        k_ref[:, base:base + LANES] = kva[:, h * LANES:(h + 1) * LANES].astype(k_ref.dtype)
        k_ref[:, base + LANES:base + 2 * LANES] = kr
    v_ref[...] = kva[:, GROUP_WIDTH:].astype(v_ref.dtype)


def _mla_up(z, gq, gkv, wq, wkv, tabs, seq, tm=512):
    n = z.shape[0]
    tm = _tile(seq, tm)
    per_seq = seq // tm
    tab_spec = pl.BlockSpec((tm, LANES), lambda i: (i % per_seq, 0))
    full = lambda a: pl.BlockSpec(a.shape, lambda i: (0,) * a.ndim)
    gq = gq.reshape(1, -1)
    gkv = gkv.reshape(1, -1)
    return pl.pallas_call(
        _mla_up_kernel,
        grid=(n // tm,),
        in_specs=[pl.BlockSpec((tm, MLA_LORA), lambda i: (i, 8)),
                  pl.BlockSpec((tm, MLA_LORA), lambda i: (i, 9)),
                  pl.BlockSpec((tm, LANES), lambda i: (i, 40)),
                  full(gq), full(gkv), full(wq), full(wkv), tab_spec, tab_spec, tab_spec],
        out_specs=[pl.BlockSpec((tm, 2 * GROUP_WIDTH), lambda i: (i, 0)),
                   pl.BlockSpec((tm, 2 * GROUP_WIDTH), lambda i: (i, 0)),
                   pl.BlockSpec((tm, GROUP_WIDTH), lambda i: (i, 0))],
        out_shape=[jax.ShapeDtypeStruct((n, 2 * GROUP_WIDTH), BF16),
                   jax.ShapeDtypeStruct((n, 2 * GROUP_WIDTH), BF16),
                   jax.ShapeDtypeStruct((n, GROUP_WIDTH), BF16)],
        compiler_params=_cparams("parallel"),
        name="mla_up",
    )(z, z, z, gq, gkv, wq, wkv, tabs["c64p"], tabs["sa64p"], tabs["sb64p"])


def _qk(q, k):
    return lax.dot_general(q, k, (((1,), (1,)), ((), ())), preferred_element_type=F32)


def _flash_core(q_ref, k_ref, v_ref, vt_ref, s_ref, m_ref, l_ref, acc_ref, tk):
    nt = k_ref.shape[0] // tk
    assert nt % 2 == 0

    @pl.when(pl.program_id(2) == 0)
    def _():
        vt_ref[...] = v_ref[...].T

    m_ref[...] = jnp.full(m_ref.shape, MASKED, F32)
    l_ref[...] = jnp.zeros(l_ref.shape, F32)
    acc_ref[...] = jnp.zeros(acc_ref.shape, F32)

    def scores(t, slot):
        off = pl.multiple_of(jnp.minimum(t, nt - 1) * tk, tk)
        s_ref[slot] = _qk(k_ref[pl.ds(off, tk), :], q_ref[...])

    def consume(t, slot):
        off = pl.multiple_of(t * tk, tk)
        s = s_ref[slot]
        m_prev = m_ref[...]
        m_new = jnp.maximum(m_prev, jnp.max(s, axis=0, keepdims=True))
        alpha = jnp.exp2(m_prev - m_new)
        p = jnp.exp2(s - m_new)
        l_ref[...] = alpha * l_ref[...] + jnp.sum(p, axis=0, keepdims=True)
        acc_ref[...] = alpha * acc_ref[...] + jnp.dot(vt_ref[:, pl.ds(off, tk)], p.astype(vt_ref.dtype),
                                                      preferred_element_type=F32)
        m_ref[...] = m_new

    scores(0, 0)

    def body(j, carry):
        scores(2 * j + 1, 1)
        consume(2 * j, 0)
        scores(2 * j + 2, 0)
        consume(2 * j + 1, 1)
        return carry

    lax.fori_loop(0, nt // 2, body, 0)


def _flash_scratch(seq, tk, queries):
    return [pltpu.VMEM((LANES, seq), BF16), pltpu.VMEM((2, tk, queries), F32),
            pltpu.VMEM((1, queries), F32), pltpu.VMEM((1, queries), F32), pltpu.VMEM((LANES, queries), F32)]


def _flash_kernel(q_ref, k_ref, v_ref, o_ref, vt_ref, s_ref, m_ref, l_ref, acc_ref, *, tk):
    _flash_core(q_ref, k_ref, v_ref, vt_ref, s_ref, m_ref, l_ref, acc_ref, tk)
    o_ref[...] = (acc_ref[...] / l_ref[...]).T.astype(o_ref.dtype)


def _mla_attention(q, k, v, batch, seq, tq=1024, tk=512):
    n = q.shape[0]
    tq, tk = _tile(seq, tq), _tile(seq, tk)
    nq = seq // tq
    dq = 2 * LANES
    return pl.pallas_call(
        functools.partial(_flash_kernel, tk=tk),
        grid=(batch, GROUP_HEADS, nq),
        in_specs=[pl.BlockSpec((tq, dq), lambda b, h, i: (b * nq + i, h)),
                  pl.BlockSpec((seq, dq), lambda b, h, i: (b, h)),
                  pl.BlockSpec((seq, LANES), lambda b, h, i: (b, h))],
        out_specs=pl.BlockSpec((tq, LANES), lambda b, h, i: (b * nq + i, h)),
        out_shape=jax.ShapeDtypeStruct((n, GROUP_WIDTH), BF16),
        scratch_shapes=_flash_scratch(seq, tk, tq),
        compiler_params=_cparams("parallel", "parallel", "arbitrary"),
        name="mla_attention",
    )(q, k, v)


def _diff_kernel(lq1_ref, lk1_ref, lq2_ref, lk2_ref, g_ref, q_ref, k_ref, v_ref, o_ref,
                 qq_ref, vt_ref, s_ref, m_ref, l_ref, acc_ref, *, tk, lambda_init):
    tq = q_ref.shape[0]
    q = q_ref[...]
    lane = lax.broadcasted_iota(jnp.int32, q.shape, 1)
    qq_ref[:tq, :] = jnp.where(lane < DIFF_QK_DIM, q, jnp.zeros_like(q))
    qq_ref[tq:, :] = jnp.where(lane >= DIFF_QK_DIM, q, jnp.zeros_like(q))
    _flash_core(qq_ref, k_ref, v_ref, vt_ref, s_ref, m_ref, l_ref, acc_ref, tk)
    lam = (jnp.exp(jnp.sum(lq1_ref[...] * lk1_ref[...], axis=-1, keepdims=True))
           - jnp.exp(jnp.sum(lq2_ref[...] * lk2_ref[...], axis=-1, keepdims=True)) + lambda_init)
    o = (acc_ref[:, :tq] / l_ref[:, :tq] - lam * (acc_ref[:, tq:] / l_ref[:, tq:])).T
    o_ref[...] = (_rms(o, g_ref[...]) * (1.0 - lambda_init)).astype(o_ref.dtype)


def _diff_attention(z, lq1, lk1, lq2, lk2, subln, lambda_init, batch, seq, tq=512, tk=512):
    n = z.shape[0]
    tq, tk = _tile(seq, tq), _tile(seq, tk)
    nq = seq // tq
    vec = lambda a: a.reshape(1, -1)
    small = lambda a: pl.BlockSpec(a.shape, lambda b, h, i: (0, 0))
    params = [vec(a) for a in (lq1, lk1, lq2, lk2, subln)]
    return pl.pallas_call(
        functools.partial(_diff_kernel, tk=tk, lambda_init=lambda_init),
        grid=(batch, GROUP_HEADS, nq),
        in_specs=[small(a) for a in params] + [
            pl.BlockSpec((tq, LANES), lambda b, h, i: (b * nq + i, h)),
            pl.BlockSpec((seq, LANES), lambda b, h, i: (b, 4 + h)),
            pl.BlockSpec((seq, LANES), lambda b, h, i: (b, 8 + h))],
        out_specs=pl.BlockSpec((tq, LANES), lambda b, h, i: (b * nq + i, h)),
        out_shape=jax.ShapeDtypeStruct((n, GROUP_WIDTH), BF16),
        scratch_shapes=[pltpu.VMEM((2 * tq, LANES), BF16)] + _flash_scratch(seq, tk, 2 * tq),
        compiler_params=_cparams("parallel", "parallel", "arbitrary"),
        name="diff_attention",
    )(*params, z, z, z)


def _na_bias_tables(rpb, rows):
    groups = rows // NA_GROUP_ROWS
    assert groups >= 3
    qc = np.arange(GRID_W)
    kc = np.arange(GRID_W)
    ws = np.clip(qc - NA_WIN_COLS // 2, 0, GRID_W - NA_WIN_COLS)
    col_ok = (kc[None, :] >= ws[:, None]) & (kc[None, :] < ws[:, None] + NA_WIN_COLS)
    edge = GRID_W - NA_WIN_COLS
    rp = jnp.pad(rpb.astype(F32), ((0, 0), (0, 0), (edge, edge)))
    toep = jnp.stack([rp[:, :, GRID_W - 1 - c:2 * GRID_W - 1 - c] for c in range(GRID_W)], axis=2)
    toep = jnp.where(jnp.asarray(col_ok)[None, None], toep, MASKED)
    masked = jnp.full((rpb.shape[0], GRID_W, GRID_W), MASKED, F32)
    tabs = []
    for m in (0, 1, groups - 1):
        kstart = np.clip(NA_GROUP_ROWS * m - NA_WIN_ROWS // 2, 0, rows - NA_KEY_ROWS)
        qr = NA_GROUP_ROWS * m + np.arange(NA_GROUP_ROWS)
        kr = kstart + np.arange(NA_KEY_ROWS)
        rs = np.clip(qr - NA_WIN_ROWS // 2, 0, rows - NA_WIN_ROWS)
        row_ok = (kr[None, :] >= rs[:, None]) & (kr[None, :] < rs[:, None] + NA_WIN_ROWS)
        ridx = kr[None, :] - qr[:, None] + NA_WIN_ROWS - 1
        tabs.append(jnp.concatenate([
            jnp.concatenate([toep[:, ridx[i, a]] if row_ok[i, a] else masked
                             for a in range(NA_KEY_ROWS)], axis=-1)
            for i in range(NA_GROUP_ROWS)], axis=1))
    return jnp.stack(tabs)


def _na_kernel(bias_ref, q_ref, k_ref, v_ref, o_ref, *, rows):
    groups = rows // NA_GROUP_ROWS
    tq = NA_GROUP_ROWS * GRID_W
    span = NA_KEY_ROWS * GRID_W

    def body(m, carry):
        cls = jnp.where(m == 0, 0, jnp.where(m == groups - 1, 2, 1))
        krow = jnp.clip(NA_GROUP_ROWS * m - NA_WIN_ROWS // 2, 0, rows - NA_KEY_ROWS)
        koff = pl.multiple_of(krow * GRID_W, GRID_W)
        qoff = pl.multiple_of(m * tq, tq)
        s = _qk(q_ref[pl.ds(qoff, tq), :], k_ref[pl.ds(koff, span), :]) + bias_ref[cls, 0]
        p = jnp.exp(s - jnp.max(s, axis=-1, keepdims=True))
        l = jnp.sum(p, axis=-1, keepdims=True)
        v = v_ref[pl.ds(koff, span), :]
        o = jnp.dot(p.astype(v.dtype), v, preferred_element_type=F32) / l
        o_ref[pl.ds(qoff, tq), :] = o.astype(o_ref.dtype)
        return carry

    lax.fori_loop(0, groups, body, 0, unroll=NA_UNROLL)


def _na_attention(z, rpb, batch, seq):
    n = z.shape[0]
    rows = seq // GRID_W
    bias = _na_bias_tables(rpb, rows)
    tq, span = NA_GROUP_ROWS * GRID_W, NA_KEY_ROWS * GRID_W
    return pl.pallas_call(
        functools.partial(_na_kernel, rows=rows),
        grid=(batch, GROUP_HEADS),
        in_specs=[pl.BlockSpec((3, 1, tq, span), lambda b, h: (0, h, 0, 0)),
                  pl.BlockSpec((seq, LANES), lambda b, h: (b, 12 + h)),
                  pl.BlockSpec((seq, LANES), lambda b, h: (b, 16 + h)),
                  pl.BlockSpec((seq, LANES), lambda b, h: (b, 20 + h))],
        out_specs=pl.BlockSpec((seq, LANES), lambda b, h: (b, h)),
        out_shape=jax.ShapeDtypeStruct((n, GROUP_WIDTH), BF16),
        compiler_params=_cparams("parallel", "parallel"),
        name="na_attention",
    )(bias, z, z, z)


def _swa_kernel(sink_ref, q_ref, k_ref, v_ref, o_ref, *, seq):
    sink = sink_ref[pl.program_id(1)]
    for c in range(SWA_TQ // SWA_SUB):
        rows = slice(c * SWA_SUB, (c + 1) * SWA_SUB)
        q0 = pl.program_id(2) * SWA_TQ + c * SWA_SUB
        koff = pl.multiple_of(jnp.clip(q0 - SWA_WINDOW, 0, seq - SWA_SPAN), SWA_WINDOW)
        s = _qk(q_ref[rows, :], k_ref[pl.ds(koff, SWA_SPAN), :])
        qpos = q0 + lax.broadcasted_iota(jnp.int32, s.shape, 0)
        kpos = koff + lax.broadcasted_iota(jnp.int32, s.shape, 1)
        s = jnp.where(jnp.abs(qpos - kpos) <= SWA_WINDOW, s, MASKED)
        m = jnp.maximum(jnp.max(s, axis=-1, keepdims=True), sink)
        p = jnp.exp(s - m)
        l = jnp.sum(p, axis=-1, keepdims=True) + jnp.exp(sink - m)
        v = v_ref[pl.ds(koff, SWA_SPAN), :]
        o_ref[rows, :] = (jnp.dot(p.astype(v.dtype), v, preferred_element_type=F32) / l).astype(o_ref.dtype)


def _swa_attention(z, sinks, batch, seq):
    n = z.shape[0]
    assert seq % SWA_TQ == 0 and seq >= SWA_SPAN
    nq = seq // SWA_TQ
    return pl.pallas_call(
        functools.partial(_swa_kernel, seq=seq),
        grid=(batch, GROUP_HEADS, nq),
        in_specs=[pl.BlockSpec(memory_space=pltpu.SMEM),
                  pl.BlockSpec((SWA_TQ, LANES), lambda b, h, i: (b * nq + i, 24 + h)),
                  pl.BlockSpec((seq, LANES), lambda b, h, i: (b, 28 + h // 2)),
                  pl.BlockSpec((seq, LANES), lambda b, h, i: (b, 30 + h // 2))],
        out_specs=pl.BlockSpec((SWA_TQ, LANES), lambda b, h, i: (b * nq + i, h)),
        out_shape=jax.ShapeDtypeStruct((n, GROUP_WIDTH), BF16),
        compiler_params=_cparams("parallel", "parallel", "arbitrary"),
        name="swa_attention",
    )(sinks, z, z, z)


def _dot_f32(x, w):
    xh = x.astype(BF16)
    xl = (x - xh.astype(F32)).astype(BF16)
    wh = w.astype(BF16)
    wl = (w - wh.astype(F32)).astype(BF16)
    d = lambda a, b: jnp.dot(a, b, preferred_element_type=F32)
    return d(xh, wh) + (d(xh, wl) + d(xl, wh))


def _top2(logits):
    lane = lax.broadcasted_iota(jnp.int32, logits.shape, 1)
    l1 = jnp.where(lane < N_EXPERTS, logits, MASKED)
    m1 = jnp.max(l1, axis=-1, keepdims=True)
    i1 = jnp.min(jnp.where(l1 == m1, lane, LANES), axis=-1, keepdims=True)
    l2 = jnp.where(lane == i1, MASKED, l1)
    m2 = jnp.max(l2, axis=-1, keepdims=True)
    i2 = jnp.min(jnp.where(l2 == m2, lane, LANES), axis=-1, keepdims=True)
    e2 = jnp.exp(m2 - m1)
    den = 1.0 + e2
    out = jnp.where(lane == 0, i1.astype(F32), 0.0)
    out = jnp.where(lane == 1, i2.astype(F32), out)
    out = jnp.where(lane == 2, 1.0 / den, out)
    return jnp.where(lane == 3, e2 / den, out)


def _outproj_kernel(*refs, routed):
    if routed:
        md, mn, ms, ml, w_ref, h_ref, g_ref, wr_ref, h1_ref, hn_ref, route_ref = refs
    else:
        md, mn, ms, ml, w_ref, h_ref, g_ref, h1_ref, hn_ref = refs
    acc = h_ref[...]
    for k, m in enumerate((md, mn, ms, ml)):
        acc = acc + jnp.dot(m[...], w_ref[k * GROUP_WIDTH:(k + 1) * GROUP_WIDTH, :],
                            preferred_element_type=F32)
    h1_ref[...] = acc
    hn = _rms(acc, g_ref[...])
    if routed:
        _slab_store(hn_ref, hn)
        route_ref[...] = _top2(_dot_f32(hn, wr_ref[...]))
    else:
        hn_ref[...] = hn.astype(hn_ref.dtype)


def _outproj(mixed, w, h, g, w_router=None, tm=256):
    n, d = h.shape
    tm = _tile(n, tm)
    routed = w_router is not None
    row = lambda width: pl.BlockSpec((tm, width), lambda i: (i, 0))
    full = lambda a: pl.BlockSpec(a.shape, lambda i: (0, 0))
    g = g.reshape(1, d)
    args = list(mixed) + [w, h, g]
    in_specs = [row(GROUP_WIDTH)] * 4 + [full(w), row(d), full(g)]
    if routed:
        args.append(w_router)
        in_specs.append(full(w_router))
        out_specs = [row(d), _slab_spec(tm, d, lambda i: (i, 0)), row(LANES)]
        out_shape = [jax.ShapeDtypeStruct((n, d), F32), _slab_shape(n, d),
                     jax.ShapeDtypeStruct((n, LANES), F32)]
    else:
        out_specs = [row(d), row(d)]
        out_shape = [jax.ShapeDtypeStruct((n, d), F32), jax.ShapeDtypeStruct((n, d), BF16)]
    return pl.pallas_call(
        functools.partial(_outproj_kernel, routed=routed),
        grid=(n // tm,),
        in_specs=in_specs, out_specs=out_specs, out_shape=out_shape,
        compiler_params=_cparams("parallel"),
        name="outproj_routed" if routed else "outproj",
    )(*args)


def _swiglu_step(x, wg_ref, wu_ref, wd_ref, o_ref, f):
    @pl.when(f == 0)
    def _():
        o_ref[...] = jnp.zeros(o_ref.shape, o_ref.dtype)

    g = jnp.dot(x, wg_ref[...], preferred_element_type=F32)
    u = jnp.dot(x, wu_ref[...], preferred_element_type=F32)
    a = (g * jax.nn.sigmoid(g) * u).astype(BF16)
    o_ref[...] += jnp.dot(a, wd_ref[...], preferred_element_type=F32)


def _ffn_kernel(x_ref, wg_ref, wu_ref, wd_ref, o_ref):
    _swiglu_step(x_ref[...], wg_ref, wu_ref, wd_ref, o_ref, pl.program_id(1))


def _dense_ffn(x, wg, wu, wd, tm=512, tf=1024):
    n, d = x.shape
    ff = wg.shape[1]
    tm, tf = _tile(n, tm), _tile(ff, tf)
    return pl.pallas_call(
        _ffn_kernel,
        grid=(n // tm, ff // tf),
        in_specs=[pl.BlockSpec((tm, d), lambda i, f: (i, 0)),
                  pl.BlockSpec((d, tf), lambda i, f: (0, f)),
                  pl.BlockSpec((d, tf), lambda i, f: (0, f)),
                  pl.BlockSpec((tf, d), lambda i, f: (f, 0))],
        out_specs=pl.BlockSpec((tm, d), lambda i, f: (i, 0)),
        out_shape=jax.ShapeDtypeStruct((n, d), F32),
        compiler_params=_cparams("parallel", "arbitrary"),
        name="dense_ffn",
    )(x, wg, wu, wd)


def _moe_kernel(blk_exp_ref, n_used_ref, tok_ref, tok_next_ref, hn_ref, wg_ref, wu_ref, wd_ref, o_ref,
                stage_ref, sem, xb_ref, acc_ref):
    b, f = pl.program_id(0), pl.program_id(1)
    tm, d = xb_ref.shape
    chunks = d // LANES
    n_used = n_used_ref[0]
    used = b < n_used
    last = f == pl.num_programs(1) - 1

    def row_copy(tok, r, slot):
        return pltpu.make_async_copy(hn_ref.at[pl.ds(tok[0, 0, r] * chunks, chunks)],
                                     stage_ref.at[slot, pl.ds(r * chunks, chunks)], sem.at[slot])

    def start_block(tok, slot):
        def go(r, carry):
            row_copy(tok, r, slot).start()
            return carry
        lax.fori_loop(0, tm, go, 0, unroll=8)

    def wait_block(tok, slot):
        def go(r, carry):
            row_copy(tok, r, slot).wait()
            return carry
        lax.fori_loop(0, tm, go, 0, unroll=8)

    @pl.when(used & (b == 0) & (f == 0))
    def _():
        start_block(tok_ref, 0)

    @pl.when(used & (f == 0))
    def _():
        slot = b % 2
        wait_block(tok_ref, slot)
        for c in range(chunks):
            xb_ref[:, c * LANES:(c + 1) * LANES] = _slab_chunk(stage_ref.at[slot], c, tm, d).astype(BF16)

    @pl.when((b + 1 < n_used) & (f == 1))
    def _():
        start_block(tok_next_ref, (b + 1) % 2)

    @pl.when(used)
    def _():
        _swiglu_step(xb_ref[...], wg_ref, wu_ref, wd_ref, acc_ref, f)

    @pl.when(used & last)
    def _():
        _slab_store(o_ref, acc_ref[...])

    @pl.when(jnp.logical_not(used) & last)
    def _():
        o_ref[...] = jnp.zeros(o_ref.shape, o_ref.dtype)


def _moe_ffn(hn, slot_tok, blk_exp, n_used, wg, wu, wd, tf=1024):
    d, ff = wg.shape[1], wg.shape[2]
    n_slots = slot_tok.shape[0]
    tf = _tile(ff, tf)
    nf = ff // tf
    assert nf >= 2
    n_blk = n_slots // MOE_TM
    tok = slot_tok.reshape(n_blk, 1, MOE_TM)
    tok_spec = lambda nxt: pl.BlockSpec((1, 1, MOE_TM), lambda b, f, be, nu: (jnp.minimum(b + nxt, n_blk - 1), 0, 0),
                                        memory_space=pltpu.SMEM)

    def blk(b, nu):
        return jnp.minimum(b, nu[0] - 1)

    def col(b, f, nu):
        return jnp.where(b < nu[0], f, nf - 1)

    grid_spec = pltpu.PrefetchScalarGridSpec(
        num_scalar_prefetch=2,
        grid=(n_blk, nf),
        in_specs=[tok_spec(0), tok_spec(1), pl.BlockSpec(memory_space=pl.ANY),
                  pl.BlockSpec((None, d, tf), lambda b, f, be, nu: (be[blk(b, nu)], 0, col(b, f, nu))),
                  pl.BlockSpec((None, d, tf), lambda b, f, be, nu: (be[blk(b, nu)], 0, col(b, f, nu))),
                  pl.BlockSpec((None, tf, d), lambda b, f, be, nu: (be[blk(b, nu)], col(b, f, nu), 0))],
        out_specs=_slab_spec(MOE_TM, d, lambda b, f, be, nu: (b, 0)),
        scratch_shapes=[pltpu.VMEM((2, MOE_TM * (d // LANES), LANES), F32), pltpu.SemaphoreType.DMA((2,)),
                        pltpu.VMEM((MOE_TM, d), BF16), pltpu.VMEM((MOE_TM, d), F32)],
    )
    return pl.pallas_call(
        _moe_kernel,
        grid_spec=grid_spec,
        out_shape=_slab_shape(n_slots, d),
        compiler_params=_cparams("arbitrary", "arbitrary"),
        name="moe_ffn",
    )(blk_exp, n_used, tok, tok, hn, wg, wu, wd)


def _ple_kernel(*refs, routed, last):
    if routed:
        h1_ref, pos_ref, pos_next_ref, ys_ref, route_ref, p_ref, gp_ref, wg_ref, wp_ref, gn_ref = refs[:10]
        outs, (stage_ref, sem, mix_ref) = refs[10:-3], refs[-3:]
        tm, d = h1_ref.shape
        chunks = d // LANES
        i = pl.program_id(0)

        def row_copy(pos, r, slot):
            return pltpu.make_async_copy(ys_ref.at[pl.ds(pos[0, 0, r] * chunks, chunks)],
                                         stage_ref.at[slot, pl.ds(r * chunks, chunks)], sem.at[slot])

        def start_rows(pos, slot):
            def go(r, carry):
                row_copy(pos, r, slot).start()
                return carry
            lax.fori_loop(0, 2 * tm, go, 0, unroll=8)

        @pl.when(i == 0)
        def _():
            start_rows(pos_ref, 0)

        @pl.when(i + 1 < pl.num_programs(0))
        def _():
            start_rows(pos_next_ref, (i + 1) % 2)

        slot = i % 2

        def wait_row(r, carry):
            row_copy(pos_ref, r, slot).wait()
            return carry
        lax.fori_loop(0, 2 * tm, wait_row, 0, unroll=8)

        route = route_ref[...]
        g0, g1 = route[:, 2:3], route[:, 3:4]
        y_ref = stage_ref.at[slot]
        for c in range(chunks):
            mix_ref[:, c * LANES:(c + 1) * LANES] = (
                g0 * y_ref[pl.ds(c, tm, stride=chunks), :]
                + g1 * y_ref[pl.ds(tm * chunks + c, tm, stride=chunks), :])
        h2 = h1_ref[...] + mix_ref[...]
    else:
        h1_ref, y_ref, p_ref, gp_ref, wg_ref, wp_ref, gn_ref = refs[:7]
        outs = refs[7:]
        h2 = h1_ref[...] + y_ref[...]
    hp = _rms(h2, gp_ref[...]).astype(BF16)
    gate = jax.nn.sigmoid(jnp.dot(hp, wg_ref[...], preferred_element_type=F32))
    pp = jnp.dot(p_ref[...].astype(BF16), wp_ref[...], preferred_element_type=F32)
    h3 = h2 + gate * pp
    outs[0][...] = _rms(h3, gn_ref[...]) if last else h3


def _ple(h1, ys, route, p, g_ple, w_gate, w_proj, g_final, last, tm=256):
    n, d = h1.shape
    tm = _tile(n, tm)
    nt = n // tm
    routed = route is not None
    row = lambda width: pl.BlockSpec((tm, width), lambda i: (i, 0))
    full = lambda a: pl.BlockSpec(a.shape, lambda i: (0, 0))
    g_ple, g_final = g_ple.reshape(1, d), g_final.reshape(1, d)
    if routed:
        ys, pos = ys
        pos = pos.reshape(nt, tm, 2).transpose(0, 2, 1).reshape(nt, 1, 2 * tm)
        pos_spec = lambda nxt: pl.BlockSpec((1, 1, 2 * tm), lambda i: (jnp.minimum(i + nxt, nt - 1), 0, 0),
                                            memory_space=pltpu.SMEM)
        args = [h1, pos, pos, ys, route]
        in_specs = [row(d), pos_spec(0), pos_spec(1), pl.BlockSpec(memory_space=pl.ANY), row(LANES)]
        scratch = [pltpu.VMEM((2, 2 * tm * (d // LANES), LANES), F32), pltpu.SemaphoreType.DMA((2,)),
                   pltpu.VMEM((tm, d), F32)]
    else:
        args = [h1, ys]
        in_specs = [row(d), row(d)]
        scratch = []
    args += [p, g_ple, w_gate, w_proj, g_final]
    in_specs += [row(p.shape[1]), full(g_ple), full(w_gate), full(w_proj), full(g_final)]
    return pl.pallas_call(
        functools.partial(_ple_kernel, routed=routed, last=last),
        grid=(nt,),
        in_specs=in_specs, out_specs=[row(d)], out_shape=[jax.ShapeDtypeStruct((n, d), F32)],
        scratch_shapes=scratch,
        compiler_params=_cparams("arbitrary"),
        name="ple_routed" if routed else "ple",
    )(*args)


def _dispatch(route, n):
    idx = route[:, :2].astype(jnp.int32)
    flat_e = idx.reshape(-1)
    onehot = (flat_e[:, None] == jnp.arange(N_EXPERTS, dtype=jnp.int32)[None, :]).astype(jnp.int32)
    csum = jnp.cumsum(onehot, axis=0)
    rank = jnp.sum(onehot * csum, axis=1) - 1
    counts = csum[-1]
    padded = (counts + MOE_TM - 1) // MOE_TM * MOE_TM
    pad_end = jnp.cumsum(padded)
    pad_start = pad_end - padded
    dest = pad_start[flat_e] + rank
    n_blk = (2 * n) // MOE_TM + N_EXPERTS
    flat_tok = jnp.arange(2 * n, dtype=jnp.int32) // 2
    slot_tok = jnp.zeros((n_blk * MOE_TM,), jnp.int32).at[dest].set(flat_tok)
    blk_exp = jnp.minimum(jnp.searchsorted(pad_end, jnp.arange(n_blk, dtype=jnp.int32) * MOE_TM, side="right"),
                          N_EXPERTS - 1).astype(jnp.int32)
    n_used = (pad_end[-1] // MOE_TM).astype(jnp.int32).reshape(1)
    return slot_tok, blk_exp, n_used, dest.reshape(n, 2).astype(jnp.int32)


def _mla_weights(w_uq, w_ukv):
    lora = w_uq.shape[0]
    wq = w_uq.reshape(lora, GROUP_HEADS, MLA_NOPE + MLA_ROPE)
    wq = jnp.pad(wq, ((0, 0), (0, 0), (0, 2 * LANES - MLA_NOPE - MLA_ROPE)))
    wkv = w_ukv.reshape(lora, GROUP_HEADS, 2, HEAD_DIM).transpose(0, 2, 1, 3)
    return wq.reshape(lora, -1).astype(BF16), wkv.reshape(lora, -1).astype(BF16)


def kernel(x, p, attn_norm, w_in, diff_lq1, diff_lk1, diff_lq2, diff_lk2, diff_subln, na_rpb, swa_sinks,
           mla_q_norm, mla_kv_norm, mla_w_uq, mla_w_ukv, w_out, ffn_norm, dense_w_gate, dense_w_up,
           dense_w_down, moe_router, moe_w_gate, moe_w_up, moe_w_down, ple_norm, ple_gate, ple_proj,
           final_norm):
    batch, seq, d = x.shape
    n = batch * seq
    depth = w_in.shape[0]
    tabs = _rope_tables(seq)
    h = x.reshape(n, d)
    for i in range(depth):
        lambda_init = 0.8 - 0.6 * math.exp(-0.3 * i)
        w_in_i = jnp.pad(w_in[i].astype(BF16), ((0, 0), (0, IN_COLS_PAD - IN_COLS)))
        z = _inproj(h, attn_norm[i], w_in_i, tabs, seq)
        wq, wkv = _mla_weights(mla_w_uq[i], mla_w_ukv[i])
        q_l, k_l, v_l = _mla_up(z, mla_q_norm[i], mla_kv_norm[i], wq, wkv, tabs, seq)
        mixed = (
            _diff_attention(z, diff_lq1[i], diff_lk1[i], diff_lq2[i], diff_lk2[i], diff_subln[i],
                            lambda_init, batch, seq),
            _na_attention(z, na_rpb[i], batch, seq),
            _swa_attention(z, swa_sinks[i], batch, seq),
            _mla_attention(q_l, k_l, v_l, batch, seq),
        )
        j = i // 2
        routed = i % 2 == 1
        last = i == depth - 1
        if not routed:
            h1, hn = _outproj(mixed, w_out[i].astype(BF16), h, ffn_norm[i])
            y = _dense_ffn(hn, dense_w_gate[j].astype(BF16), dense_w_up[j].astype(BF16),
                           dense_w_down[j].astype(BF16))
            route = None
        else:
            w_router = jnp.pad(moe_router[j], ((0, 0), (0, LANES - N_EXPERTS)))
            h1, hn, route = _outproj(mixed, w_out[i].astype(BF16), h, ffn_norm[i], w_router)
            slot_tok, blk_exp, n_used, pos = _dispatch(route, n)
            ys = _moe_ffn(hn, slot_tok, blk_exp, n_used, moe_w_gate[j].astype(BF16),
                          moe_w_up[j].astype(BF16), moe_w_down[j].astype(BF16))
            y = (ys, pos)
        (h,) = _ple(h1, y, route, p[i].reshape(n, -1), ple_norm[i], ple_gate[i].astype(BF16),
                    ple_proj[i].astype(BF16), final_norm, last)
    return h.reshape(batch, seq, d)
```

```python
import functools
import math

import numpy as np
import jax
import jax.numpy as jnp
from jax import lax
from jax.experimental import pallas as pl
from jax.experimental.pallas import tpu as pltpu

F32 = jnp.float32
BF16 = jnp.bfloat16

HEAD_DIM = 128
GROUP_HEADS = 4
GROUP_WIDTH = GROUP_HEADS * HEAD_DIM
ROPE_THETA = 10000.0
NORM_EPS = 1e-6
DIFF_QK_DIM = 64
GRID_W = 64
NA_WIN_ROWS = 8
NA_WIN_COLS = 16
SWA_WINDOW = 128
MLA_LORA = 512
MLA_NOPE = 128
MLA_ROPE = 64
IN_COLS = 5184
N_EXPERTS = 8

LANES = 128
VMEM_LIMIT = 56 * 1024 * 1024
MASKED = -1e30
LOG2E = math.log2(math.e)

IN_TN = 512
IN_COLS_PAD = -(-IN_COLS // LANES) * LANES
NA_GROUP_ROWS = 4
NA_KEY_ROWS = NA_GROUP_ROWS + NA_WIN_ROWS
NA_UNROLL = 4
SWA_TQ = 1024
SWA_SUB = 256
SWA_SPAN = SWA_SUB + 2 * SWA_WINDOW
MOE_TM = 512


def _tile(n, pref):
    t = min(n, pref)
    assert n % t == 0, (n, t)
    return t


def _cparams(*sem):
    return pltpu.CompilerParams(dimension_semantics=sem, vmem_limit_bytes=VMEM_LIMIT)


def _rms(x, g):
    return x * lax.rsqrt(jnp.mean(x * x, axis=-1, keepdims=True) + NORM_EPS) * g


def _slab_shape(rows, width):
    return jax.ShapeDtypeStruct((rows * (width // LANES), LANES), F32)


def _slab_spec(tm, width, index_map):
    return pl.BlockSpec((tm * (width // LANES), LANES), index_map)


def _slab_chunk(ref, c, tm, width):
    return ref[pl.ds(c, tm, stride=width // LANES), :]


def _slab_store(ref, x):
    tm, width = x.shape
    for c in range(width // LANES):
        ref[pl.ds(c, tm, stride=width // LANES), :] = x[:, c * LANES:(c + 1) * LANES]


def _rope_tables(seq):
    def base(dim):
        inv = ROPE_THETA ** (-jnp.arange(0, dim, 2, dtype=F32) / dim)
        ang = jnp.arange(seq, dtype=F32)[:, None] * inv[None, :]
        ang = jnp.concatenate([ang, ang], axis=-1)
        return jnp.cos(ang), jnp.sin(ang)

    lane = np.arange(LANES)
    cos, sin = base(DIFF_QK_DIM)
    c64 = jnp.concatenate([cos, cos], axis=-1)
    s64 = jnp.concatenate([sin, sin], axis=-1)
    lo = jnp.asarray((lane % 64) < 32)
    sa64 = jnp.where(lo, -s64, 0.0)
    sb64 = jnp.where(lo, 0.0, s64)
    first = jnp.asarray(lane < 64)
    c64p, sa64p, sb64p = (jnp.where(first, t, 0.0) for t in (c64, sa64, sb64))
    cos, sin = base(HEAD_DIM)
    s128 = jnp.where(jnp.asarray(lane < 64), -sin, sin)
    return dict(c64=c64, sa64=sa64, sb64=sb64, c64p=c64p, sa64p=sa64p, sb64p=sb64p,
                c128=cos, s128=s128)


def _rope64(x, c, sa, sb):
    return x * c + pltpu.roll(x, 96, 1) * sa + pltpu.roll(x, 32, 1) * sb


def _rope128(x, c, s):
    return x * c + pltpu.roll(x, 64, 1) * s


def _inproj_kernel(h_ref, g_ref, w_ref, c64, sa64, sb64, c128, s128, o_ref):
    x = _rms(h_ref[...], g_ref[...]).astype(BF16)
    for j in range(-(-IN_COLS_PAD // IN_TN)):
        width = min(IN_TN, IN_COLS_PAD - j * IN_TN)
        acc = jnp.dot(x, w_ref[:, j * IN_TN:j * IN_TN + width], preferred_element_type=F32)
        for g in range(width // LANES):
            a = acc[:, g * LANES:(g + 1) * LANES]
            if j == 0:
                a = _rope64(a, c64[...], sa64[...], sb64[...]) * (DIFF_QK_DIM ** -0.5 * LOG2E)
            elif j == 1:
                a = _rope64(a, c64[...], sa64[...], sb64[...])
            elif j == 3:
                a = a * HEAD_DIM ** -0.5
            elif j == 6:
                a = _rope128(a, c128[...], s128[...]) * HEAD_DIM ** -0.5
            elif j == 7 and g < 2:
                a = _rope128(a, c128[...], s128[...])
            o_ref[:, j * IN_TN + g * LANES:j * IN_TN + (g + 1) * LANES] = a.astype(o_ref.dtype)


def _inproj(h, g, w, tabs, seq, tm=512):
    n, d = h.shape
    tm = _tile(seq, tm)
    per_seq = seq // tm
    tab_spec = pl.BlockSpec((tm, LANES), lambda i: (i % per_seq, 0))
    g = g.reshape(1, d)
    return pl.pallas_call(
        _inproj_kernel,
        grid=(n // tm,),
        in_specs=[pl.BlockSpec((tm, d), lambda i: (i, 0)),
                  pl.BlockSpec(g.shape, lambda i: (0, 0)),
                  pl.BlockSpec(w.shape, lambda i: (0, 0), pipeline_mode=pl.Buffered(1))] + [tab_spec] * 5,
        out_specs=pl.BlockSpec((tm, IN_COLS_PAD), lambda i: (i, 0)),
        out_shape=jax.ShapeDtypeStruct((n, IN_COLS_PAD), BF16),
        compiler_params=_cparams("parallel"),
        name="inproj",
    )(h, g, w, tabs["c64"], tabs["sa64"], tabs["sb64"], tabs["c128"], tabs["s128"])


def _mla_up_kernel(cq_ref, ckv_ref, kr_ref, gq_ref, gkv_ref, wq_ref, wkv_ref, c_ref, sa_ref, sb_ref,
                   q_ref, k_ref, v_ref):
    scale = (MLA_NOPE + MLA_ROPE) ** -0.5 * LOG2E
    c, sa, sb = c_ref[...], sa_ref[...], sb_ref[...]
    cq = _rms(cq_ref[...].astype(F32), gq_ref[...]).astype(BF16)
    qa = jnp.dot(cq, wq_ref[...], preferred_element_type=F32)
    for h in range(GROUP_HEADS):
        base = h * 2 * LANES
        q_ref[:, base:base + LANES] = (qa[:, base:base + LANES] * scale).astype(q_ref.dtype)
        q_ref[:, base + LANES:base + 2 * LANES] = (
            _rope64(qa[:, base + LANES:base + 2 * LANES], c, sa, sb) * scale).astype(q_ref.dtype)
    ckv = _rms(ckv_ref[...].astype(F32), gkv_ref[...]).astype(BF16)
    kva = jnp.dot(ckv, wkv_ref[...], preferred_element_type=F32)
    kr = _rope64(kr_ref[...].astype(F32), c, sa, sb).astype(k_ref.dtype)
    for h in range(GROUP_HEADS):
        base = h * 2 * LANES
        k_ref[:, base:base + LANES] = kva[:, h * LANES:(h + 1) * LANES].astype(k_ref.dtype)
        k_ref[:, base + LANES:base + 2 * LANES] = kr
    v_ref[...] = kva[:, GROUP_WIDTH:].astype(v_ref.dtype)


def _mla_up(z, gq, gkv, wq, wkv, tabs, seq, tm=512):
    n = z.shape[0]
    tm = _tile(seq, tm)
    per_seq = seq // tm
    tab_spec = pl.BlockSpec((tm, LANES), lambda i: (i % per_seq, 0))
    full = lambda a: pl.BlockSpec(a.shape, lambda i: (0,) * a.ndim)
    gq = gq.reshape(1, -1)
    gkv = gkv.reshape(1, -1)
    return pl.pallas_call(
        _mla_up_kernel,
        grid=(n // tm,),
        in_specs=[pl.BlockSpec((tm, MLA_LORA), lambda i: (i, 8)),
                  pl.BlockSpec((tm, MLA_LORA), lambda i: (i, 9)),
                  pl.BlockSpec((tm, LANES), lambda i: (i, 40)),
                  full(gq), full(gkv), full(wq), full(wkv), tab_spec, tab_spec, tab_spec],
        out_specs=[pl.BlockSpec((tm, 2 * GROUP_WIDTH), lambda i: (i, 0)),
                   pl.BlockSpec((tm, 2 * GROUP_WIDTH), lambda i: (i, 0)),
                   pl.BlockSpec((tm, GROUP_WIDTH), lambda i: (i, 0))],
        out_shape=[jax.ShapeDtypeStruct((n, 2 * GROUP_WIDTH), BF16),
                   jax.ShapeDtypeStruct((n, 2 * GROUP_WIDTH), BF16),
                   jax.ShapeDtypeStruct((n, GROUP_WIDTH), BF16)],
        compiler_params=_cparams("parallel"),
        name="mla_up",
    )(z, z, z, gq, gkv, wq, wkv, tabs["c64p"], tabs["sa64p"], tabs["sb64p"])


def _qk(q, k):
    return lax.dot_general(q, k, (((1,), (1,)), ((), ())), preferred_element_type=F32)


def _flash_core(q_ref, k_ref, v_ref, vt_ref, s_ref, m_ref, l_ref, acc_ref, tk):
    nt = k_ref.shape[0] // tk
    assert nt % 2 == 0

    @pl.when(pl.program_id(2) == 0)
    def _():
        vt_ref[...] = v_ref[...].T

    m_ref[...] = jnp.full(m_ref.shape, MASKED, F32)
    l_ref[...] = jnp.zeros(l_ref.shape, F32)
    acc_ref[...] = jnp.zeros(acc_ref.shape, F32)

    def scores(t, slot):
        off = pl.multiple_of(jnp.minimum(t, nt - 1) * tk, tk)
        s_ref[slot] = _qk(k_ref[pl.ds(off, tk), :], q_ref[...])

    def consume(t, slot):
        off = pl.multiple_of(t * tk, tk)
        s = s_ref[slot]
        m_prev = m_ref[...]
        m_new = jnp.maximum(m_prev, jnp.max(s, axis=0, keepdims=True))
        alpha = jnp.exp2(m_prev - m_new)
        p = jnp.exp2(s - m_new)
        l_ref[...] = alpha * l_ref[...] + jnp.sum(p, axis=0, keepdims=True)
        acc_ref[...] = alpha * acc_ref[...] + jnp.dot(vt_ref[:, pl.ds(off, tk)], p.astype(vt_ref.dtype),
                                                      preferred_element_type=F32)
        m_ref[...] = m_new

    scores(0, 0)

    def body(j, carry):
        scores(2 * j + 1, 1)
        consume(2 * j, 0)
        scores(2 * j + 2, 0)
        consume(2 * j + 1, 1)
        return carry

    lax.fori_loop(0, nt // 2, body, 0, unroll=2)


def _flash_scratch(seq, tk, queries):
    return [pltpu.VMEM((LANES, seq), BF16), pltpu.VMEM((2, tk, queries), F32),
            pltpu.VMEM((1, queries), F32), pltpu.VMEM((1, queries), F32), pltpu.VMEM((LANES, queries), F32)]


def _flash_kernel(q_ref, k_ref, v_ref, o_ref, vt_ref, s_ref, m_ref, l_ref, acc_ref, *, tk):
    _flash_core(q_ref, k_ref, v_ref, vt_ref, s_ref, m_ref, l_ref, acc_ref, tk)
    o_ref[...] = (acc_ref[...] / l_ref[...]).T.astype(o_ref.dtype)


def _mla_attention(q, k, v, batch, seq, tq=1024, tk=512):
    n = q.shape[0]
    tq, tk = _tile(seq, tq), _tile(seq, tk)
    nq = seq // tq
    dq = 2 * LANES
    return pl.pallas_call(
        functools.partial(_flash_kernel, tk=tk),
        grid=(batch, GROUP_HEADS, nq),
        in_specs=[pl.BlockSpec((tq, dq), lambda b, h, i: (b * nq + i, h)),
                  pl.BlockSpec((seq, dq), lambda b, h, i: (b, h)),
                  pl.BlockSpec((seq, LANES), lambda b, h, i: (b, h))],
        out_specs=pl.BlockSpec((tq, LANES), lambda b, h, i: (b * nq + i, h)),
        out_shape=jax.ShapeDtypeStruct((n, GROUP_WIDTH), BF16),
        scratch_shapes=_flash_scratch(seq, tk, tq),
        compiler_params=_cparams("parallel", "parallel", "arbitrary"),
        name="mla_attention",
    )(q, k, v)


def _diff_kernel(lq1_ref, lk1_ref, lq2_ref, lk2_ref, g_ref, q_ref, k_ref, v_ref, o_ref,
                 qq_ref, vt_ref, s_ref, m_ref, l_ref, acc_ref, *, tk, lambda_init):
    tq = q_ref.shape[0]
    q = q_ref[...]
    lane = lax.broadcasted_iota(jnp.int32, q.shape, 1)
    qq_ref[:tq, :] = jnp.where(lane < DIFF_QK_DIM, q, jnp.zeros_like(q))
    qq_ref[tq:, :] = jnp.where(lane >= DIFF_QK_DIM, q, jnp.zeros_like(q))
    _flash_core(qq_ref, k_ref, v_ref, vt_ref, s_ref, m_ref, l_ref, acc_ref, tk)
    lam = (jnp.exp(jnp.sum(lq1_ref[...] * lk1_ref[...], axis=-1, keepdims=True))
           - jnp.exp(jnp.sum(lq2_ref[...] * lk2_ref[...], axis=-1, keepdims=True)) + lambda_init)
    o = (acc_ref[:, :tq] / l_ref[:, :tq] - lam * (acc_ref[:, tq:] / l_ref[:, tq:])).T
    o_ref[...] = (_rms(o, g_ref[...]) * (1.0 - lambda_init)).astype(o_ref.dtype)


def _diff_attention(z, lq1, lk1, lq2, lk2, subln, lambda_init, batch, seq, tq=512, tk=512):
    n = z.shape[0]
    tq, tk = _tile(seq, tq), _tile(seq, tk)
    nq = seq // tq
    vec = lambda a: a.reshape(1, -1)
    small = lambda a: pl.BlockSpec(a.shape, lambda b, h, i: (0, 0))
    params = [vec(a) for a in (lq1, lk1, lq2, lk2, subln)]
    return pl.pallas_call(
        functools.partial(_diff_kernel, tk=tk, lambda_init=lambda_init),
        grid=(batch, GROUP_HEADS, nq),
        in_specs=[small(a) for a in params] + [
            pl.BlockSpec((tq, LANES), lambda b, h, i: (b * nq + i, h)),
            pl.BlockSpec((seq, LANES), lambda b, h, i: (b, 4 + h)),
            pl.BlockSpec((seq, LANES), lambda b, h, i: (b, 8 + h))],
        out_specs=pl.BlockSpec((tq, LANES), lambda b, h, i: (b * nq + i, h)),
        out_shape=jax.ShapeDtypeStruct((n, GROUP_WIDTH), BF16),
        scratch_shapes=[pltpu.VMEM((2 * tq, LANES), BF16)] + _flash_scratch(seq, tk, 2 * tq),
        compiler_params=_cparams("parallel", "parallel", "arbitrary"),
        name="diff_attention",
    )(*params, z, z, z)


def _na_bias_tables(rpb, rows):
    groups = rows // NA_GROUP_ROWS
    assert groups >= 3
    qc = np.arange(GRID_W)
    kc = np.arange(GRID_W)
    ws = np.clip(qc - NA_WIN_COLS // 2, 0, GRID_W - NA_WIN_COLS)
    col_ok = (kc[None, :] >= ws[:, None]) & (kc[None, :] < ws[:, None] + NA_WIN_COLS)
    edge = GRID_W - NA_WIN_COLS
    rp = jnp.pad(rpb.astype(F32), ((0, 0), (0, 0), (edge, edge)))
    toep = jnp.stack([rp[:, :, GRID_W - 1 - c:2 * GRID_W - 1 - c] for c in range(GRID_W)], axis=2)
    toep = jnp.where(jnp.asarray(col_ok)[None, None], toep, MASKED)
    masked = jnp.full((rpb.shape[0], GRID_W, GRID_W), MASKED, F32)
    tabs = []
    for m in (0, 1, groups - 1):
        kstart = np.clip(NA_GROUP_ROWS * m - NA_WIN_ROWS // 2, 0, rows - NA_KEY_ROWS)
        qr = NA_GROUP_ROWS * m + np.arange(NA_GROUP_ROWS)
        kr = kstart + np.arange(NA_KEY_ROWS)
        rs = np.clip(qr - NA_WIN_ROWS // 2, 0, rows - NA_WIN_ROWS)
        row_ok = (kr[None, :] >= rs[:, None]) & (kr[None, :] < rs[:, None] + NA_WIN_ROWS)
        ridx = kr[None, :] - qr[:, None] + NA_WIN_ROWS - 1
        tabs.append(jnp.concatenate([
            jnp.concatenate([toep[:, ridx[i, a]] if row_ok[i, a] else masked
                             for a in range(NA_KEY_ROWS)], axis=-1)
            for i in range(NA_GROUP_ROWS)], axis=1))
    return jnp.stack(tabs)


def _na_kernel(bias_ref, q_ref, k_ref, v_ref, o_ref, *, rows):
    groups = rows // NA_GROUP_ROWS
    tq = NA_GROUP_ROWS * GRID_W
    span = NA_KEY_ROWS * GRID_W

    def body(m, carry):
        cls = jnp.where(m == 0, 0, jnp.where(m == groups - 1, 2, 1))
        krow = jnp.clip(NA_GROUP_ROWS * m - NA_WIN_ROWS // 2, 0, rows - NA_KEY_ROWS)
        koff = pl.multiple_of(krow * GRID_W, GRID_W)
        qoff = pl.multiple_of(m * tq, tq)
        s = _qk(q_ref[pl.ds(qoff, tq), :], k_ref[pl.ds(koff, span), :]) + bias_ref[cls, 0]
        p = jnp.exp(s - jnp.max(s, axis=-1, keepdims=True))
        l = jnp.sum(p, axis=-1, keepdims=True)
        v = v_ref[pl.ds(koff, span), :]
        o = jnp.dot(p.astype(v.dtype), v, preferred_element_type=F32) / l
        o_ref[pl.ds(qoff, tq), :] = o.astype(o_ref.dtype)
        return carry

    lax.fori_loop(0, groups, body, 0, unroll=NA_UNROLL)


def _na_attention(z, rpb, batch, seq):
    n = z.shape[0]
    rows = seq // GRID_W
    bias = _na_bias_tables(rpb, rows)
    tq, span = NA_GROUP_ROWS * GRID_W, NA_KEY_ROWS * GRID_W
    return pl.pallas_call(
        functools.partial(_na_kernel, rows=rows),
        grid=(batch, GROUP_HEADS),
        in_specs=[pl.BlockSpec((3, 1, tq, span), lambda b, h: (0, h, 0, 0)),
                  pl.BlockSpec((seq, LANES), lambda b, h: (b, 12 + h)),
                  pl.BlockSpec((seq, LANES), lambda b, h: (b, 16 + h)),
                  pl.BlockSpec((seq, LANES), lambda b, h: (b, 20 + h))],
        out_specs=pl.BlockSpec((seq, LANES), lambda b, h: (b, h)),
        out_shape=jax.ShapeDtypeStruct((n, GROUP_WIDTH), BF16),
        compiler_params=_cparams("parallel", "parallel"),
        name="na_attention",
    )(bias, z, z, z)


def _swa_kernel(sink_ref, q_ref, k_ref, v_ref, o_ref, *, seq):
    sink = sink_ref[pl.program_id(1)]
    for c in range(SWA_TQ // SWA_SUB):
        rows = slice(c * SWA_SUB, (c + 1) * SWA_SUB)
        q0 = pl.program_id(2) * SWA_TQ + c * SWA_SUB
        koff = pl.multiple_of(jnp.clip(q0 - SWA_WINDOW, 0, seq - SWA_SPAN), SWA_WINDOW)
        s = _qk(q_ref[rows, :], k_ref[pl.ds(koff, SWA_SPAN), :])
        qpos = q0 + lax.broadcasted_iota(jnp.int32, s.shape, 0)
        kpos = koff + lax.broadcasted_iota(jnp.int32, s.shape, 1)
        s = jnp.where(jnp.abs(qpos - kpos) <= SWA_WINDOW, s, MASKED)
        m = jnp.maximum(jnp.max(s, axis=-1, keepdims=True), sink)
        p = jnp.exp(s - m)
        l = jnp.sum(p, axis=-1, keepdims=True) + jnp.exp(sink - m)
        v = v_ref[pl.ds(koff, SWA_SPAN), :]
        o_ref[rows, :] = (jnp.dot(p.astype(v.dtype), v, preferred_element_type=F32) / l).astype(o_ref.dtype)


def _swa_attention(z, sinks, batch, seq):
    n = z.shape[0]
    assert seq % SWA_TQ == 0 and seq >= SWA_SPAN
    nq = seq // SWA_TQ
    return pl.pallas_call(
        functools.partial(_swa_kernel, seq=seq),
        grid=(batch, GROUP_HEADS, nq),
        in_specs=[pl.BlockSpec(memory_space=pltpu.SMEM),
                  pl.BlockSpec((SWA_TQ, LANES), lambda b, h, i: (b * nq + i, 24 + h)),
                  pl.BlockSpec((seq, LANES), lambda b, h, i: (b, 28 + h // 2)),
                  pl.BlockSpec((seq, LANES), lambda b, h, i: (b, 30 + h // 2))],
        out_specs=pl.BlockSpec((SWA_TQ, LANES), lambda b, h, i: (b * nq + i, h)),
        out_shape=jax.ShapeDtypeStruct((n, GROUP_WIDTH), BF16),
        compiler_params=_cparams("parallel", "parallel", "arbitrary"),
        name="swa_attention",
    )(sinks, z, z, z)


def _dot_f32(x, w):
    xh = x.astype(BF16)
    xl = (x - xh.astype(F32)).astype(BF16)
    wh = w.astype(BF16)
    wl = (w - wh.astype(F32)).astype(BF16)
    d = lambda a, b: jnp.dot(a, b, preferred_element_type=F32)
    return d(xh, wh) + (d(xh, wl) + d(xl, wh))


def _top2(logits):
    lane = lax.broadcasted_iota(jnp.int32, logits.shape, 1)
    l1 = jnp.where(lane < N_EXPERTS, logits, MASKED)
    m1 = jnp.max(l1, axis=-1, keepdims=True)
    i1 = jnp.min(jnp.where(l1 == m1, lane, LANES), axis=-1, keepdims=True)
    l2 = jnp.where(lane == i1, MASKED, l1)
    m2 = jnp.max(l2, axis=-1, keepdims=True)
    i2 = jnp.min(jnp.where(l2 == m2, lane, LANES), axis=-1, keepdims=True)
    e2 = jnp.exp(m2 - m1)
    den = 1.0 + e2
    out = jnp.where(lane == 0, i1.astype(F32), 0.0)
    out = jnp.where(lane == 1, i2.astype(F32), out)
    out = jnp.where(lane == 2, 1.0 / den, out)
    return jnp.where(lane == 3, e2 / den, out)


def _outproj_kernel(*refs, routed):
    if routed:
        md, mn, ms, ml, w_ref, h_ref, g_ref, wr_ref, h1_ref, hn_ref, route_ref = refs
    else:
        md, mn, ms, ml, w_ref, h_ref, g_ref, h1_ref, hn_ref = refs
    acc = h_ref[...]
    for k, m in enumerate((md, mn, ms, ml)):
        acc = acc + jnp.dot(m[...], w_ref[k * GROUP_WIDTH:(k + 1) * GROUP_WIDTH, :],
                            preferred_element_type=F32)
    h1_ref[...] = acc
    hn = _rms(acc, g_ref[...])
    if routed:
        _slab_store(hn_ref, hn)
        route_ref[...] = _top2(_dot_f32(hn, wr_ref[...]))
    else:
        hn_ref[...] = hn.astype(hn_ref.dtype)


def _outproj(mixed, w, h, g, w_router=None, tm=256):
    n, d = h.shape
    tm = _tile(n, tm)
    routed = w_router is not None
    row = lambda width: pl.BlockSpec((tm, width), lambda i: (i, 0))
    full = lambda a: pl.BlockSpec(a.shape, lambda i: (0, 0))
    g = g.reshape(1, d)
    args = list(mixed) + [w, h, g]
    in_specs = [row(GROUP_WIDTH)] * 4 + [full(w), row(d), full(g)]
    if routed:
        args.append(w_router)
        in_specs.append(full(w_router))
        out_specs = [row(d), _slab_spec(tm, d, lambda i: (i, 0)), row(LANES)]
        out_shape = [jax.ShapeDtypeStruct((n, d), F32), _slab_shape(n, d),
                     jax.ShapeDtypeStruct((n, LANES), F32)]
    else:
        out_specs = [row(d), row(d)]
        out_shape = [jax.ShapeDtypeStruct((n, d), F32), jax.ShapeDtypeStruct((n, d), BF16)]
    return pl.pallas_call(
        functools.partial(_outproj_kernel, routed=routed),
        grid=(n // tm,),
        in_specs=in_specs, out_specs=out_specs, out_shape=out_shape,
        compiler_params=_cparams("parallel"),
        name="outproj_routed" if routed else "outproj",
    )(*args)


def _swiglu_step(x, wg_ref, wu_ref, wd_ref, o_ref, f):
    @pl.when(f == 0)
    def _():
        o_ref[...] = jnp.zeros(o_ref.shape, o_ref.dtype)

    g = jnp.dot(x, wg_ref[...], preferred_element_type=F32)
    u = jnp.dot(x, wu_ref[...], preferred_element_type=F32)
    a = (g * jax.nn.sigmoid(g) * u).astype(BF16)
    o_ref[...] += jnp.dot(a, wd_ref[...], preferred_element_type=F32)


def _ffn_kernel(x_ref, wg_ref, wu_ref, wd_ref, o_ref):
    _swiglu_step(x_ref[...], wg_ref, wu_ref, wd_ref, o_ref, pl.program_id(1))


def _dense_ffn(x, wg, wu, wd, tm=512, tf=1024):
    n, d = x.shape
    ff = wg.shape[1]
    tm, tf = _tile(n, tm), _tile(ff, tf)
    return pl.pallas_call(
        _ffn_kernel,
        grid=(n // tm, ff // tf),
        in_specs=[pl.BlockSpec((tm, d), lambda i, f: (i, 0)),
                  pl.BlockSpec((d, tf), lambda i, f: (0, f)),
                  pl.BlockSpec((d, tf), lambda i, f: (0, f)),
                  pl.BlockSpec((tf, d), lambda i, f: (f, 0))],
        out_specs=pl.BlockSpec((tm, d), lambda i, f: (i, 0)),
        out_shape=jax.ShapeDtypeStruct((n, d), F32),
        compiler_params=_cparams("parallel", "arbitrary"),
        name="dense_ffn",
    )(x, wg, wu, wd)


def _moe_kernel(blk_exp_ref, n_used_ref, tok_ref, tok_next_ref, hn_ref, wg_ref, wu_ref, wd_ref, o_ref,
                stage_ref, sem, xb_ref, acc_ref):
    b, f = pl.program_id(0), pl.program_id(1)
    tm, d = xb_ref.shape
    chunks = d // LANES
    n_used = n_used_ref[0]
    used = b < n_used
    last = f == pl.num_programs(1) - 1

    def row_copy(tok, r, slot):
        return pltpu.make_async_copy(hn_ref.at[pl.ds(tok[0, 0, r] * chunks, chunks)],
                                     stage_ref.at[slot, pl.ds(r * chunks, chunks)], sem.at[slot])

    def start_block(tok, slot):
        def go(r, carry):
            row_copy(tok, r, slot).start()
            return carry
        lax.fori_loop(0, tm, go, 0, unroll=8)

    def wait_block(tok, slot):
        def go(r, carry):
            row_copy(tok, r, slot).wait()
            return carry
        lax.fori_loop(0, tm, go, 0, unroll=8)

    @pl.when(used & (b == 0) & (f == 0))
    def _():
        start_block(tok_ref, 0)

    @pl.when(used & (f == 0))
    def _():
        slot = b % 2
        wait_block(tok_ref, slot)
        for c in range(chunks):
            xb_ref[:, c * LANES:(c + 1) * LANES] = _slab_chunk(stage_ref.at[slot], c, tm, d).astype(BF16)

    @pl.when((b + 1 < n_used) & (f == 1))
    def _():
        start_block(tok_next_ref, (b + 1) % 2)

    @pl.when(used)
    def _():
        _swiglu_step(xb_ref[...], wg_ref, wu_ref, wd_ref, acc_ref, f)

    @pl.when(used & last)
    def _():
        _slab_store(o_ref, acc_ref[...])

    @pl.when(jnp.logical_not(used) & last)
    def _():
        o_ref[...] = jnp.zeros(o_ref.shape, o_ref.dtype)


def _moe_ffn(hn, slot_tok, blk_exp, n_used, wg, wu, wd, tf=1024):
    d, ff = wg.shape[1], wg.shape[2]
    n_slots = slot_tok.shape[0]
    tf = _tile(ff, tf)
    nf = ff // tf
    assert nf >= 2
    n_blk = n_slots // MOE_TM
    tok = slot_tok.reshape(n_blk, 1, MOE_TM)
    tok_spec = lambda nxt: pl.BlockSpec((1, 1, MOE_TM), lambda b, f, be, nu: (jnp.minimum(b + nxt, n_blk - 1), 0, 0),
                                        memory_space=pltpu.SMEM)

    def blk(b, nu):
        return jnp.minimum(b, nu[0] - 1)

    def col(b, f, nu):
        return jnp.where(b < nu[0], f, nf - 1)

    grid_spec = pltpu.PrefetchScalarGridSpec(
        num_scalar_prefetch=2,
        grid=(n_blk, nf),
        in_specs=[tok_spec(0), tok_spec(1), pl.BlockSpec(memory_space=pl.ANY),
                  pl.BlockSpec((None, d, tf), lambda b, f, be, nu: (be[blk(b, nu)], 0, col(b, f, nu))),
                  pl.BlockSpec((None, d, tf), lambda b, f, be, nu: (be[blk(b, nu)], 0, col(b, f, nu))),
                  pl.BlockSpec((None, tf, d), lambda b, f, be, nu: (be[blk(b, nu)], col(b, f, nu), 0))],
        out_specs=_slab_spec(MOE_TM, d, lambda b, f, be, nu: (b, 0)),
        scratch_shapes=[pltpu.VMEM((2, MOE_TM * (d // LANES), LANES), F32), pltpu.SemaphoreType.DMA((2,)),
                        pltpu.VMEM((MOE_TM, d), BF16), pltpu.VMEM((MOE_TM, d), F32)],
    )
    return pl.pallas_call(
        _moe_kernel,
        grid_spec=grid_spec,
        out_shape=_slab_shape(n_slots, d),
        compiler_params=_cparams("arbitrary", "arbitrary"),
        name="moe_ffn",
    )(blk_exp, n_used, tok, tok, hn, wg, wu, wd)


def _ple_kernel(*refs, routed, last):
    if routed:
        h1_ref, pos_ref, pos_next_ref, ys_ref, route_ref, p_ref, gp_ref, wg_ref, wp_ref, gn_ref = refs[:10]
        outs, (stage_ref, sem, mix_ref) = refs[10:-3], refs[-3:]
        tm, d = h1_ref.shape
        chunks = d // LANES
        i = pl.program_id(0)

        def row_copy(pos, r, slot):
            return pltpu.make_async_copy(ys_ref.at[pl.ds(pos[0, 0, r] * chunks, chunks)],
                                         stage_ref.at[slot, pl.ds(r * chunks, chunks)], sem.at[slot])

        def start_rows(pos, slot):
            def go(r, carry):
                row_copy(pos, r, slot).start()
                return carry
            lax.fori_loop(0, 2 * tm, go, 0, unroll=8)

        @pl.when(i == 0)
        def _():
            start_rows(pos_ref, 0)

        @pl.when(i + 1 < pl.num_programs(0))
        def _():
            start_rows(pos_next_ref, (i + 1) % 2)

        slot = i % 2

        def wait_row(r, carry):
            row_copy(pos_ref, r, slot).wait()
            return carry
        lax.fori_loop(0, 2 * tm, wait_row, 0, unroll=8)

        route = route_ref[...]
        g0, g1 = route[:, 2:3], route[:, 3:4]
        y_ref = stage_ref.at[slot]
        for c in range(chunks):
            mix_ref[:, c * LANES:(c + 1) * LANES] = (
                g0 * y_ref[pl.ds(c, tm, stride=chunks), :]
                + g1 * y_ref[pl.ds(tm * chunks + c, tm, stride=chunks), :])
        h2 = h1_ref[...] + mix_ref[...]
    else:
        h1_ref, y_ref, p_ref, gp_ref, wg_ref, wp_ref, gn_ref = refs[:7]
        outs = refs[7:]
        h2 = h1_ref[...] + y_ref[...]
    hp = _rms(h2, gp_ref[...]).astype(BF16)
    gate = jax.nn.sigmoid(jnp.dot(hp, wg_ref[...], preferred_element_type=F32))
    pp = jnp.dot(p_ref[...].astype(BF16), wp_ref[...], preferred_element_type=F32)
    h3 = h2 + gate * pp
    outs[0][...] = _rms(h3, gn_ref[...]) if last else h3


def _ple(h1, y, routing, p, g_ple, w_gate, w_proj, g_final, last, tm=256):
    n, d = h1.shape
    tm = _tile(n, tm)
    nt = n // tm
    routed = routing is not None
    row = lambda width: pl.BlockSpec((tm, width), lambda i: (i, 0))
    full = lambda a: pl.BlockSpec(a.shape, lambda i: (0, 0))
    g_ple, g_final = g_ple.reshape(1, d), g_final.reshape(1, d)
    if routed:
        route, pos = routing
        pos = pos.reshape(nt, tm, 2).transpose(0, 2, 1).reshape(nt, 1, 2 * tm)
        pos_spec = lambda nxt: pl.BlockSpec((1, 1, 2 * tm), lambda i: (jnp.minimum(i + nxt, nt - 1), 0, 0),
                                            memory_space=pltpu.SMEM)
        args = [h1, pos, pos, y, route]
        in_specs = [row(d), pos_spec(0), pos_spec(1), pl.BlockSpec(memory_space=pl.ANY), row(LANES)]
        scratch = [pltpu.VMEM((2, 2 * tm * (d // LANES), LANES), F32), pltpu.SemaphoreType.DMA((2,)),
                   pltpu.VMEM((tm, d), F32)]
    else:
        args = [h1, y]
        in_specs = [row(d), row(d)]
        scratch = []
    args += [p, g_ple, w_gate, w_proj, g_final]
    in_specs += [row(p.shape[1]), full(g_ple), full(w_gate), full(w_proj), full(g_final)]
    return pl.pallas_call(
        functools.partial(_ple_kernel, routed=routed, last=last),
        grid=(nt,),
        in_specs=in_specs, out_specs=[row(d)], out_shape=[jax.ShapeDtypeStruct((n, d), F32)],
        scratch_shapes=scratch,
        compiler_params=_cparams("arbitrary"),
        name="ple_routed" if routed else "ple",
    )(*args)


def _dispatch(route, n):
    idx = route[:, :2].astype(jnp.int32)
    flat_e = idx.reshape(-1)
    onehot = (flat_e[:, None] == jnp.arange(N_EXPERTS, dtype=jnp.int32)[None, :]).astype(jnp.int32)
    csum = jnp.cumsum(onehot, axis=0)
    rank = jnp.sum(onehot * csum, axis=1) - 1
    counts = csum[-1]
    padded = (counts + MOE_TM - 1) // MOE_TM * MOE_TM
    pad_end = jnp.cumsum(padded)
    pad_start = pad_end - padded
    dest = pad_start[flat_e] + rank
    n_blk = (2 * n) // MOE_TM + N_EXPERTS
    flat_tok = jnp.arange(2 * n, dtype=jnp.int32) // 2
    slot_tok = jnp.zeros((n_blk * MOE_TM,), jnp.int32).at[dest].set(flat_tok)
    blk_exp = jnp.minimum(jnp.searchsorted(pad_end, jnp.arange(n_blk, dtype=jnp.int32) * MOE_TM, side="right"),
                          N_EXPERTS - 1).astype(jnp.int32)
    n_used = (pad_end[-1] // MOE_TM).astype(jnp.int32).reshape(1)
    return slot_tok, blk_exp, n_used, dest.reshape(n, 2).astype(jnp.int32)


def _mla_weights(w_uq, w_ukv):
    lora = w_uq.shape[0]
    wq = w_uq.reshape(lora, GROUP_HEADS, MLA_NOPE + MLA_ROPE)
    wq = jnp.pad(wq, ((0, 0), (0, 0), (0, 2 * LANES - MLA_NOPE - MLA_ROPE)))
    wkv = w_ukv.reshape(lora, GROUP_HEADS, 2, HEAD_DIM).transpose(0, 2, 1, 3)
    return wq.reshape(lora, -1).astype(BF16), wkv.reshape(lora, -1).astype(BF16)


def kernel(x, p, attn_norm, w_in, diff_lq1, diff_lk1, diff_lq2, diff_lk2, diff_subln, na_rpb, swa_sinks,
           mla_q_norm, mla_kv_norm, mla_w_uq, mla_w_ukv, w_out, ffn_norm, dense_w_gate, dense_w_up,
           dense_w_down, moe_router, moe_w_gate, moe_w_up, moe_w_down, ple_norm, ple_gate, ple_proj,
           final_norm):
    batch, seq, d = x.shape
    n = batch * seq
    depth = w_in.shape[0]
    tabs = _rope_tables(seq)
    h = x.reshape(n, d)
    for i in range(depth):
        lambda_init = 0.8 - 0.6 * math.exp(-0.3 * i)
        w_in_i = jnp.pad(w_in[i].astype(BF16), ((0, 0), (0, IN_COLS_PAD - IN_COLS)))
        z = _inproj(h, attn_norm[i], w_in_i, tabs, seq)
        wq, wkv = _mla_weights(mla_w_uq[i], mla_w_ukv[i])
        q_l, k_l, v_l = _mla_up(z, mla_q_norm[i], mla_kv_norm[i], wq, wkv, tabs, seq)
        mixed = (
            _diff_attention(z, diff_lq1[i], diff_lk1[i], diff_lq2[i], diff_lk2[i], diff_subln[i],
                            lambda_init, batch, seq),
            _na_attention(z, na_rpb[i], batch, seq),
            _swa_attention(z, swa_sinks[i], batch, seq),
            _mla_attention(q_l, k_l, v_l, batch, seq),
        )
        j = i // 2
        routed = i % 2 == 1
        last = i == depth - 1
        if not routed:
            h1, hn = _outproj(mixed, w_out[i].astype(BF16), h, ffn_norm[i])
            y = _dense_ffn(hn, dense_w_gate[j].astype(BF16), dense_w_up[j].astype(BF16),
                           dense_w_down[j].astype(BF16))
            routing = None
        else:
            w_router = jnp.pad(moe_router[j], ((0, 0), (0, LANES - N_EXPERTS)))
            h1, hn, route = _outproj(mixed, w_out[i].astype(BF16), h, ffn_norm[i], w_router)
            slot_tok, blk_exp, n_used, pos = _dispatch(route, n)
            y = _moe_ffn(hn, slot_tok, blk_exp, n_used, moe_w_gate[j].astype(BF16),
                         moe_w_up[j].astype(BF16), moe_w_down[j].astype(BF16))
            routing = (route, pos)
        (h,) = _ple(h1, y, routing, p[i].reshape(n, -1), ple_norm[i], ple_gate[i].astype(BF16),
                    ple_proj[i].astype(BF16), final_norm, last)
    return h.reshape(batch, seq, d)
```

```python
import functools
import math

import numpy as np
import jax
import jax.numpy as jnp
from jax import lax
from jax.experimental import pallas as pl
from jax.experimental.pallas import tpu as pltpu

F32 = jnp.float32
BF16 = jnp.bfloat16

HEAD_DIM = 128
GROUP_HEADS = 4
GROUP_WIDTH = GROUP_HEADS * HEAD_DIM
ROPE_THETA = 10000.0
NORM_EPS = 1e-6
DIFF_QK_DIM = 64
GRID_W = 64
NA_WIN_ROWS = 8
NA_WIN_COLS = 16
SWA_WINDOW = 128
MLA_LORA = 512
MLA_NOPE = 128
MLA_ROPE = 64
IN_COLS = 5184
N_EXPERTS = 8

LANES = 128
VMEM_LIMIT = 56 * 1024 * 1024
MASKED = -1e30
LOG2E = math.log2(math.e)

IN_TN = 512
IN_COLS_PAD = -(-IN_COLS // LANES) * LANES
NA_GROUP_ROWS = 4
NA_KEY_ROWS = NA_GROUP_ROWS + NA_WIN_ROWS
NA_UNROLL = 4
SWA_TQ = 1024
SWA_SUB = 256
SWA_SPAN = SWA_SUB + 2 * SWA_WINDOW
MOE_TM = 512


def _tile(n, pref):
    t = min(n, pref)
    assert n % t == 0, (n, t)
    return t


def _cparams(*sem):
    return pltpu.CompilerParams(dimension_semantics=sem, vmem_limit_bytes=VMEM_LIMIT)


def _rms(x, g):
    return x * lax.rsqrt(jnp.mean(x * x, axis=-1, keepdims=True) + NORM_EPS) * g


def _slab_shape(rows, width):
    return jax.ShapeDtypeStruct((rows * (width // LANES), LANES), F32)


def _slab_spec(tm, width, index_map):
    return pl.BlockSpec((tm * (width // LANES), LANES), index_map)


def _slab_chunk(ref, c, tm, width):
    return ref[pl.ds(c, tm, stride=width // LANES), :]


def _slab_store(ref, x):
    tm, width = x.shape
    for c in range(width // LANES):
        ref[pl.ds(c, tm, stride=width // LANES), :] = x[:, c * LANES:(c + 1) * LANES]


def _rope_tables(seq):
    def base(dim):
        inv = ROPE_THETA ** (-jnp.arange(0, dim, 2, dtype=F32) / dim)
        ang = jnp.arange(seq, dtype=F32)[:, None] * inv[None, :]
        ang = jnp.concatenate([ang, ang], axis=-1)
        return jnp.cos(ang), jnp.sin(ang)

    lane = np.arange(LANES)
    cos, sin = base(DIFF_QK_DIM)
    c64 = jnp.concatenate([cos, cos], axis=-1)
    s64 = jnp.concatenate([sin, sin], axis=-1)
    lo = jnp.asarray((lane % 64) < 32)
    sa64 = jnp.where(lo, -s64, 0.0)
    sb64 = jnp.where(lo, 0.0, s64)
    first = jnp.asarray(lane < 64)
    c64p, sa64p, sb64p = (jnp.where(first, t, 0.0) for t in (c64, sa64, sb64))
    cos, sin = base(HEAD_DIM)
    s128 = jnp.where(jnp.asarray(lane < 64), -sin, sin)
    return dict(c64=c64, sa64=sa64, sb64=sb64, c64p=c64p, sa64p=sa64p, sb64p=sb64p,
                c128=cos, s128=s128)


def _rope64(x, c, sa, sb):
    return x * c + pltpu.roll(x, 96, 1) * sa + pltpu.roll(x, 32, 1) * sb


def _rope128(x, c, s):
    return x * c + pltpu.roll(x, 64, 1) * s


def _inproj_kernel(h_ref, g_ref, w_ref, c64, sa64, sb64, c128, s128, o_ref):
    x = _rms(h_ref[...], g_ref[...]).astype(BF16)
    for j in range(-(-IN_COLS_PAD // IN_TN)):
        width = min(IN_TN, IN_COLS_PAD - j * IN_TN)
        acc = jnp.dot(x, w_ref[:, j * IN_TN:j * IN_TN + width], preferred_element_type=F32)
        for g in range(width // LANES):
            a = acc[:, g * LANES:(g + 1) * LANES]
            if j == 0:
                a = _rope64(a, c64[...], sa64[...], sb64[...]) * (DIFF_QK_DIM ** -0.5 * LOG2E)
            elif j == 1:
                a = _rope64(a, c64[...], sa64[...], sb64[...])
            elif j == 3:
                a = a * HEAD_DIM ** -0.5
            elif j == 6:
                a = _rope128(a, c128[...], s128[...]) * HEAD_DIM ** -0.5
            elif j == 7 and g < 2:
                a = _rope128(a, c128[...], s128[...])
            o_ref[:, j * IN_TN + g * LANES:j * IN_TN + (g + 1) * LANES] = a.astype(o_ref.dtype)


def _inproj(h, g, w, tabs, seq, tm=512):
    n, d = h.shape
    tm = _tile(seq, tm)
    per_seq = seq // tm
    tab_spec = pl.BlockSpec((tm, LANES), lambda i: (i % per_seq, 0))
    g = g.reshape(1, d)
    return pl.pallas_call(
        _inproj_kernel,
        grid=(n // tm,),
        in_specs=[pl.BlockSpec((tm, d), lambda i: (i, 0)),
                  pl.BlockSpec(g.shape, lambda i: (0, 0)),
                  pl.BlockSpec(w.shape, lambda i: (0, 0), pipeline_mode=pl.Buffered(1))] + [tab_spec] * 5,
        out_specs=pl.BlockSpec((tm, IN_COLS_PAD), lambda i: (i, 0)),
        out_shape=jax.ShapeDtypeStruct((n, IN_COLS_PAD), BF16),
        compiler_params=_cparams("parallel"),
        name="inproj",
    )(h, g, w, tabs["c64"], tabs["sa64"], tabs["sb64"], tabs["c128"], tabs["s128"])


def _mla_up_kernel(cq_ref, ckv_ref, kr_ref, gq_ref, gkv_ref, wq_ref, wkv_ref, c_ref, sa_ref, sb_ref,
                   q_ref, k_ref, v_ref):
    scale = (MLA_NOPE + MLA_ROPE) ** -0.5 * LOG2E
    c, sa, sb = c_ref[...], sa_ref[...], sb_ref[...]
    cq = _rms(cq_ref[...].astype(F32), gq_ref[...]).astype(BF16)
    qa = jnp.dot(cq, wq_ref[...], preferred_element_type=F32)
    for h in range(GROUP_HEADS):
        base = h * 2 * LANES
        q_ref[:, base:base + LANES] = (qa[:, base:base + LANES] * scale).astype(q_ref.dtype)
        q_ref[:, base + LANES:base + 2 * LANES] = (
            _rope64(qa[:, base + LANES:base + 2 * LANES], c, sa, sb) * scale).astype(q_ref.dtype)
    ckv = _rms(ckv_ref[...].astype(F32), gkv_ref[...]).astype(BF16)
    kva = jnp.dot(ckv, wkv_ref[...], preferred_element_type=F32)
    kr = _rope64(kr_ref[...].astype(F32), c, sa, sb).astype(k_ref.dtype)
    for h in range(GROUP_HEADS):
        base = h * 2 * LANES
        k_ref[:, base:base + LANES] = kva[:, h * LANES:(h + 1) * LANES].astype(k_ref.dtype)
        k_ref[:, base + LANES:base + 2 * LANES] = kr
    v_ref[...] = kva[:, GROUP_WIDTH:].astype(v_ref.dtype)


def _mla_up(z, gq, gkv, wq, wkv, tabs, seq, tm=512):
    n = z.shape[0]
    tm = _tile(seq, tm)
    per_seq = seq // tm
    tab_spec = pl.BlockSpec((tm, LANES), lambda i: (i % per_seq, 0))
    full = lambda a: pl.BlockSpec(a.shape, lambda i: (0,) * a.ndim)
    gq = gq.reshape(1, -1)
    gkv = gkv.reshape(1, -1)
    return pl.pallas_call(
        _mla_up_kernel,
        grid=(n // tm,),
        in_specs=[pl.BlockSpec((tm, MLA_LORA), lambda i: (i, 8)),
                  pl.BlockSpec((tm, MLA_LORA), lambda i: (i, 9)),
                  pl.BlockSpec((tm, LANES), lambda i: (i, 40)),
                  full(gq), full(gkv), full(wq), full(wkv), tab_spec, tab_spec, tab_spec],
        out_specs=[pl.BlockSpec((tm, 2 * GROUP_WIDTH), lambda i: (i, 0)),
                   pl.BlockSpec((tm, 2 * GROUP_WIDTH), lambda i: (i, 0)),
                   pl.BlockSpec((tm, GROUP_WIDTH), lambda i: (i, 0))],
        out_shape=[jax.ShapeDtypeStruct((n, 2 * GROUP_WIDTH), BF16),
                   jax.ShapeDtypeStruct((n, 2 * GROUP_WIDTH), BF16),
                   jax.ShapeDtypeStruct((n, GROUP_WIDTH), BF16)],
        compiler_params=_cparams("parallel"),
        name="mla_up",
    )(z, z, z, gq, gkv, wq, wkv, tabs["c64p"], tabs["sa64p"], tabs["sb64p"])


def _qk(q, k):
    return lax.dot_general(q, k, (((1,), (1,)), ((), ())), preferred_element_type=F32)


def _flash_core(q_ref, k_ref, v_ref, vt_ref, s_ref, m_ref, l_ref, acc_ref, tk):
    nt = k_ref.shape[0] // tk
    assert nt % 2 == 0

    @pl.when(pl.program_id(2) == 0)
    def _():
        vt_ref[...] = v_ref[...].T

    m_ref[...] = jnp.full(m_ref.shape, MASKED, F32)
    l_ref[...] = jnp.zeros(l_ref.shape, F32)
    acc_ref[...] = jnp.zeros(acc_ref.shape, F32)

    def scores(t, slot):
        off = pl.multiple_of(jnp.minimum(t, nt - 1) * tk, tk)
        s_ref[slot] = _qk(k_ref[pl.ds(off, tk), :], q_ref[...])

    def consume(t, slot):
        off = pl.multiple_of(t * tk, tk)
        s = s_ref[slot]
        m_prev = m_ref[...]
        m_new = jnp.maximum(m_prev, jnp.max(s, axis=0, keepdims=True))
        alpha = jnp.exp2(m_prev - m_new)
        p = jnp.exp2(s - m_new)
        l_ref[...] = alpha * l_ref[...] + jnp.sum(p, axis=0, keepdims=True)
        acc_ref[...] = alpha * acc_ref[...] + jnp.dot(vt_ref[:, pl.ds(off, tk)], p.astype(vt_ref.dtype),
                                                      preferred_element_type=F32)
        m_ref[...] = m_new

    scores(0, 0)

    def body(j, carry):
        scores(2 * j + 1, 1)
        consume(2 * j, 0)
        scores(2 * j + 2, 0)
        consume(2 * j + 1, 1)
        return carry

    lax.fori_loop(0, nt // 2, body, 0, unroll=2)


def _flash_scratch(seq, tk, queries):
    return [pltpu.VMEM((LANES, seq), BF16), pltpu.VMEM((2, tk, queries), F32),
            pltpu.VMEM((1, queries), F32), pltpu.VMEM((1, queries), F32), pltpu.VMEM((LANES, queries), F32)]


def _flash_kernel(q_ref, k_ref, v_ref, o_ref, vt_ref, s_ref, m_ref, l_ref, acc_ref, *, tk):
    _flash_core(q_ref, k_ref, v_ref, vt_ref, s_ref, m_ref, l_ref, acc_ref, tk)
    o_ref[...] = (acc_ref[...] / l_ref[...]).T.astype(o_ref.dtype)


def _mla_attention(q, k, v, batch, seq, tq=1024, tk=512):
    n = q.shape[0]
    tq, tk = _tile(seq, tq), _tile(seq, tk)
    nq = seq // tq
    dq = 2 * LANES
    return pl.pallas_call(
        functools.partial(_flash_kernel, tk=tk),
        grid=(batch, GROUP_HEADS, nq),
        in_specs=[pl.BlockSpec((tq, dq), lambda b, h, i: (b * nq + i, h)),
                  pl.BlockSpec((seq, dq), lambda b, h, i: (b, h)),
                  pl.BlockSpec((seq, LANES), lambda b, h, i: (b, h))],
        out_specs=pl.BlockSpec((tq, LANES), lambda b, h, i: (b * nq + i, h)),
        out_shape=jax.ShapeDtypeStruct((n, GROUP_WIDTH), BF16),
        scratch_shapes=_flash_scratch(seq, tk, tq),
        compiler_params=_cparams("parallel", "parallel", "arbitrary"),
        name="mla_attention",
    )(q, k, v)


def _diff_kernel(lq1_ref, lk1_ref, lq2_ref, lk2_ref, g_ref, q_ref, k_ref, v_ref, o_ref,
                 qq_ref, vt_ref, s_ref, m_ref, l_ref, acc_ref, *, tk, lambda_init):
    tq = q_ref.shape[0]
    q = q_ref[...]
    lane = lax.broadcasted_iota(jnp.int32, q.shape, 1)
    qq_ref[:tq, :] = jnp.where(lane < DIFF_QK_DIM, q, jnp.zeros_like(q))
    qq_ref[tq:, :] = jnp.where(lane >= DIFF_QK_DIM, q, jnp.zeros_like(q))
    _flash_core(qq_ref, k_ref, v_ref, vt_ref, s_ref, m_ref, l_ref, acc_ref, tk)
    lam = (jnp.exp(jnp.sum(lq1_ref[...] * lk1_ref[...], axis=-1, keepdims=True))
           - jnp.exp(jnp.sum(lq2_ref[...] * lk2_ref[...], axis=-1, keepdims=True)) + lambda_init)
    o = (acc_ref[:, :tq] / l_ref[:, :tq] - lam * (acc_ref[:, tq:] / l_ref[:, tq:])).T
    o_ref[...] = (_rms(o, g_ref[...]) * (1.0 - lambda_init)).astype(o_ref.dtype)


def _diff_attention(z, lq1, lk1, lq2, lk2, subln, lambda_init, batch, seq, tq=512, tk=512):
    n = z.shape[0]
    tq, tk = _tile(seq, tq), _tile(seq, tk)
    nq = seq // tq
    vec = lambda a: a.reshape(1, -1)
    small = lambda a: pl.BlockSpec(a.shape, lambda b, h, i: (0, 0))
    params = [vec(a) for a in (lq1, lk1, lq2, lk2, subln)]
    return pl.pallas_call(
        functools.partial(_diff_kernel, tk=tk, lambda_init=lambda_init),
        grid=(batch, GROUP_HEADS, nq),
        in_specs=[small(a) for a in params] + [
            pl.BlockSpec((tq, LANES), lambda b, h, i: (b * nq + i, h)),
            pl.BlockSpec((seq, LANES), lambda b, h, i: (b, 4 + h)),
            pl.BlockSpec((seq, LANES), lambda b, h, i: (b, 8 + h))],
        out_specs=pl.BlockSpec((tq, LANES), lambda b, h, i: (b * nq + i, h)),
        out_shape=jax.ShapeDtypeStruct((n, GROUP_WIDTH), BF16),
        scratch_shapes=[pltpu.VMEM((2 * tq, LANES), BF16)] + _flash_scratch(seq, tk, 2 * tq),
        compiler_params=_cparams("parallel", "parallel", "arbitrary"),
        name="diff_attention",
    )(*params, z, z, z)


def _na_bias_tables(rpb, rows):
    groups = rows // NA_GROUP_ROWS
    assert groups >= 3
    qc = np.arange(GRID_W)
    kc = np.arange(GRID_W)
    ws = np.clip(qc - NA_WIN_COLS // 2, 0, GRID_W - NA_WIN_COLS)
    col_ok = (kc[None, :] >= ws[:, None]) & (kc[None, :] < ws[:, None] + NA_WIN_COLS)
    edge = GRID_W - NA_WIN_COLS
    rp = jnp.pad(rpb.astype(F32), ((0, 0), (0, 0), (edge, edge)))
    toep = jnp.stack([rp[:, :, GRID_W - 1 - c:2 * GRID_W - 1 - c] for c in range(GRID_W)], axis=2)
    toep = jnp.where(jnp.asarray(col_ok)[None, None], toep, MASKED)
    masked = jnp.full((rpb.shape[0], GRID_W, GRID_W), MASKED, F32)
    tabs = []
    for m in (0, 1, groups - 1):
        kstart = np.clip(NA_GROUP_ROWS * m - NA_WIN_ROWS // 2, 0, rows - NA_KEY_ROWS)
        qr = NA_GROUP_ROWS * m + np.arange(NA_GROUP_ROWS)
        kr = kstart + np.arange(NA_KEY_ROWS)
        rs = np.clip(qr - NA_WIN_ROWS // 2, 0, rows - NA_WIN_ROWS)
        row_ok = (kr[None, :] >= rs[:, None]) & (kr[None, :] < rs[:, None] + NA_WIN_ROWS)
        ridx = kr[None, :] - qr[:, None] + NA_WIN_ROWS - 1
        tabs.append(jnp.concatenate([
            jnp.concatenate([toep[:, ridx[i, a]] if row_ok[i, a] else masked
                             for a in range(NA_KEY_ROWS)], axis=-1)
            for i in range(NA_GROUP_ROWS)], axis=1))
    return jnp.stack(tabs)


def _na_kernel(bias_ref, q_ref, k_ref, v_ref, o_ref, *, rows):
    groups = rows // NA_GROUP_ROWS
    tq = NA_GROUP_ROWS * GRID_W
    span = NA_KEY_ROWS * GRID_W

    def body(m, carry):
        cls = jnp.where(m == 0, 0, jnp.where(m == groups - 1, 2, 1))
        krow = jnp.clip(NA_GROUP_ROWS * m - NA_WIN_ROWS // 2, 0, rows - NA_KEY_ROWS)
        koff = pl.multiple_of(krow * GRID_W, GRID_W)
        qoff = pl.multiple_of(m * tq, tq)
        s = _qk(q_ref[pl.ds(qoff, tq), :], k_ref[pl.ds(koff, span), :]) + bias_ref[cls, 0]
        p = jnp.exp(s - jnp.max(s, axis=-1, keepdims=True))
        l = jnp.sum(p, axis=-1, keepdims=True)
        v = v_ref[pl.ds(koff, span), :]
        o = jnp.dot(p.astype(v.dtype), v, preferred_element_type=F32) / l
        o_ref[pl.ds(qoff, tq), :] = o.astype(o_ref.dtype)
        return carry

    lax.fori_loop(0, groups, body, 0, unroll=NA_UNROLL)


def _na_attention(z, rpb, batch, seq):
    n = z.shape[0]
    rows = seq // GRID_W
    bias = _na_bias_tables(rpb, rows)
    tq, span = NA_GROUP_ROWS * GRID_W, NA_KEY_ROWS * GRID_W
    return pl.pallas_call(
        functools.partial(_na_kernel, rows=rows),
        grid=(batch, GROUP_HEADS),
        in_specs=[pl.BlockSpec((3, 1, tq, span), lambda b, h: (0, h, 0, 0)),
                  pl.BlockSpec((seq, LANES), lambda b, h: (b, 12 + h)),
                  pl.BlockSpec((seq, LANES), lambda b, h: (b, 16 + h)),
                  pl.BlockSpec((seq, LANES), lambda b, h: (b, 20 + h))],
        out_specs=pl.BlockSpec((seq, LANES), lambda b, h: (b, h)),
        out_shape=jax.ShapeDtypeStruct((n, GROUP_WIDTH), BF16),
        compiler_params=_cparams("parallel", "parallel"),
        name="na_attention",
    )(bias, z, z, z)


def _swa_kernel(sink_ref, q_ref, k_ref, v_ref, o_ref, *, seq):
    sink = sink_ref[pl.program_id(1)]
    for c in range(SWA_TQ // SWA_SUB):
        rows = slice(c * SWA_SUB, (c + 1) * SWA_SUB)
        q0 = pl.program_id(2) * SWA_TQ + c * SWA_SUB
        koff = pl.multiple_of(jnp.clip(q0 - SWA_WINDOW, 0, seq - SWA_SPAN), SWA_WINDOW)
        s = _qk(q_ref[rows, :], k_ref[pl.ds(koff, SWA_SPAN), :])
        qpos = q0 + lax.broadcasted_iota(jnp.int32, s.shape, 0)
        kpos = koff + lax.broadcasted_iota(jnp.int32, s.shape, 1)
        s = jnp.where(jnp.abs(qpos - kpos) <= SWA_WINDOW, s, MASKED)
        m = jnp.maximum(jnp.max(s, axis=-1, keepdims=True), sink)
        p = jnp.exp(s - m)
        l = jnp.sum(p, axis=-1, keepdims=True) + jnp.exp(sink - m)
        v = v_ref[pl.ds(koff, SWA_SPAN), :]
        o_ref[rows, :] = (jnp.dot(p.astype(v.dtype), v, preferred_element_type=F32) / l).astype(o_ref.dtype)


def _swa_attention(z, sinks, batch, seq):
    n = z.shape[0]
    assert seq % SWA_TQ == 0 and seq >= SWA_SPAN
    nq = seq // SWA_TQ
    return pl.pallas_call(
        functools.partial(_swa_kernel, seq=seq),
        grid=(batch, GROUP_HEADS, nq),
        in_specs=[pl.BlockSpec(memory_space=pltpu.SMEM),
                  pl.BlockSpec((SWA_TQ, LANES), lambda b, h, i: (b * nq + i, 24 + h)),
                  pl.BlockSpec((seq, LANES), lambda b, h, i: (b, 28 + h // 2)),
                  pl.BlockSpec((seq, LANES), lambda b, h, i: (b, 30 + h // 2))],
        out_specs=pl.BlockSpec((SWA_TQ, LANES), lambda b, h, i: (b * nq + i, h)),
        out_shape=jax.ShapeDtypeStruct((n, GROUP_WIDTH), BF16),
        compiler_params=_cparams("parallel", "parallel", "arbitrary"),
        name="swa_attention",
    )(sinks, z, z, z)


def _dot_f32(x, w):
    xh = x.astype(BF16)
    xl = (x - xh.astype(F32)).astype(BF16)
    wh = w.astype(BF16)
    wl = (w - wh.astype(F32)).astype(BF16)
    d = lambda a, b: jnp.dot(a, b, preferred_element_type=F32)
    return d(xh, wh) + (d(xh, wl) + d(xl, wh))


def _top2(logits):
    lane = lax.broadcasted_iota(jnp.int32, logits.shape, 1)
    l1 = jnp.where(lane < N_EXPERTS, logits, MASKED)
    m1 = jnp.max(l1, axis=-1, keepdims=True)
    i1 = jnp.min(jnp.where(l1 == m1, lane, LANES), axis=-1, keepdims=True)
    l2 = jnp.where(lane == i1, MASKED, l1)
    m2 = jnp.max(l2, axis=-1, keepdims=True)
    i2 = jnp.min(jnp.where(l2 == m2, lane, LANES), axis=-1, keepdims=True)
    e2 = jnp.exp(m2 - m1)
    den = 1.0 + e2
    out = jnp.where(lane == 0, i1.astype(F32), 0.0)
    out = jnp.where(lane == 1, i2.astype(F32), out)
    out = jnp.where(lane == 2, 1.0 / den, out)
    return jnp.where(lane == 3, e2 / den, out)


def _outproj_kernel(*refs, routed):
    if routed:
        md, mn, ms, ml, w_ref, h_ref, g_ref, wr_ref, h1_ref, hn_ref, route_ref = refs
    else:
        md, mn, ms, ml, w_ref, h_ref, g_ref, h1_ref, hn_ref = refs
    acc = h_ref[...]
    for k, m in enumerate((md, mn, ms, ml)):
        acc = acc + jnp.dot(m[...], w_ref[k * GROUP_WIDTH:(k + 1) * GROUP_WIDTH, :],
                            preferred_element_type=F32)
    h1_ref[...] = acc
    hn = _rms(acc, g_ref[...])
    if routed:
        _slab_store(hn_ref, hn)
        route_ref[...] = _top2(_dot_f32(hn, wr_ref[...]))
    else:
        hn_ref[...] = hn.astype(hn_ref.dtype)


def _outproj(mixed, w, h, g, w_router=None, tm=256):
    n, d = h.shape
    tm = _tile(n, tm)
    routed = w_router is not None
    row = lambda width: pl.BlockSpec((tm, width), lambda i: (i, 0))
    full = lambda a: pl.BlockSpec(a.shape, lambda i: (0, 0))
    g = g.reshape(1, d)
    args = list(mixed) + [w, h, g]
    in_specs = [row(GROUP_WIDTH)] * 4 + [full(w), row(d), full(g)]
    if routed:
        args.append(w_router)
        in_specs.append(full(w_router))
        out_specs = [row(d), _slab_spec(tm, d, lambda i: (i, 0)), row(LANES)]
        out_shape = [jax.ShapeDtypeStruct((n, d), F32), _slab_shape(n, d),
                     jax.ShapeDtypeStruct((n, LANES), F32)]
    else:
        out_specs = [row(d), row(d)]
        out_shape = [jax.ShapeDtypeStruct((n, d), F32), jax.ShapeDtypeStruct((n, d), BF16)]
    return pl.pallas_call(
        functools.partial(_outproj_kernel, routed=routed),
        grid=(n // tm,),
        in_specs=in_specs, out_specs=out_specs, out_shape=out_shape,
        compiler_params=_cparams("parallel"),
        name="outproj_routed" if routed else "outproj",
    )(*args)


def _swiglu_step(x, wg_ref, wu_ref, wd_ref, o_ref, f):
    @pl.when(f == 0)
    def _():
        o_ref[...] = jnp.zeros(o_ref.shape, o_ref.dtype)

    g = jnp.dot(x, wg_ref[...], preferred_element_type=F32)
    u = jnp.dot(x, wu_ref[...], preferred_element_type=F32)
    a = (g * jax.nn.sigmoid(g) * u).astype(BF16)
    o_ref[...] += jnp.dot(a, wd_ref[...], preferred_element_type=F32)


def _ffn_kernel(x_ref, wg_ref, wu_ref, wd_ref, o_ref):
    _swiglu_step(x_ref[...], wg_ref, wu_ref, wd_ref, o_ref, pl.program_id(1))


def _dense_ffn(x, wg, wu, wd, tm=512, tf=1024):
    n, d = x.shape
    ff = wg.shape[1]
    tm, tf = _tile(n, tm), _tile(ff, tf)
    return pl.pallas_call(
        _ffn_kernel,
        grid=(n // tm, ff // tf),
        in_specs=[pl.BlockSpec((tm, d), lambda i, f: (i, 0)),
                  pl.BlockSpec((d, tf), lambda i, f: (0, f)),
                  pl.BlockSpec((d, tf), lambda i, f: (0, f)),
                  pl.BlockSpec((tf, d), lambda i, f: (f, 0))],
        out_specs=pl.BlockSpec((tm, d), lambda i, f: (i, 0)),
        out_shape=jax.ShapeDtypeStruct((n, d), F32),
        compiler_params=_cparams("parallel", "arbitrary"),
        name="dense_ffn",
    )(x, wg, wu, wd)


def _moe_kernel(blk_exp_ref, n_used_ref, tok_ref, tok_next_ref, hn_ref, wg_ref, wu_ref, wd_ref, o_ref,
                stage_ref, sem, xb_ref, acc_ref):
    b, f = pl.program_id(0), pl.program_id(1)
    tm, d = xb_ref.shape
    chunks = d // LANES
    n_used = n_used_ref[0]
    used = b < n_used
    last = f == pl.num_programs(1) - 1

    def row_copy(tok, r, slot):
        return pltpu.make_async_copy(hn_ref.at[pl.ds(tok[0, 0, r] * chunks, chunks)],
                                     stage_ref.at[slot, pl.ds(r * chunks, chunks)], sem.at[slot])

    def start_block(tok, slot):
        def go(r, carry):
            row_copy(tok, r, slot).start()
            return carry
        lax.fori_loop(0, tm, go, 0, unroll=8)

    def wait_block(slot):
        pltpu.make_async_copy(hn_ref.at[pl.ds(0, tm * chunks)], stage_ref.at[slot], sem.at[slot]).wait()

    @pl.when(used & (b == 0) & (f == 0))
    def _():
        start_block(tok_ref, 0)

    @pl.when(used & (f == 0))
    def _():
        slot = b % 2
        wait_block(slot)
        for c in range(chunks):
            xb_ref[:, c * LANES:(c + 1) * LANES] = _slab_chunk(stage_ref.at[slot], c, tm, d).astype(BF16)

    @pl.when((b + 1 < n_used) & (f == 1))
    def _():
        start_block(tok_next_ref, (b + 1) % 2)

    @pl.when(used)
    def _():
        _swiglu_step(xb_ref[...], wg_ref, wu_ref, wd_ref, acc_ref, f)

    @pl.when(used & last)
    def _():
        _slab_store(o_ref, acc_ref[...])

    @pl.when(jnp.logical_not(used) & last)
    def _():
        o_ref[...] = jnp.zeros(o_ref.shape, o_ref.dtype)


def _moe_ffn(hn, slot_tok, blk_exp, n_used, wg, wu, wd, tf=1024):
    d, ff = wg.shape[1], wg.shape[2]
    n_slots = slot_tok.shape[0]
    tf = _tile(ff, tf)
    nf = ff // tf
    assert nf >= 2
    n_blk = n_slots // MOE_TM
    tok = slot_tok.reshape(n_blk, 1, MOE_TM)
    tok_spec = lambda nxt: pl.BlockSpec((1, 1, MOE_TM), lambda b, f, be, nu: (jnp.minimum(b + nxt, n_blk - 1), 0, 0),
                                        memory_space=pltpu.SMEM)

    def blk(b, nu):
        return jnp.minimum(b, nu[0] - 1)

    def col(b, f, nu):
        return jnp.where(b < nu[0], f, nf - 1)

    grid_spec = pltpu.PrefetchScalarGridSpec(
        num_scalar_prefetch=2,
        grid=(n_blk, nf),
        in_specs=[tok_spec(0), tok_spec(1), pl.BlockSpec(memory_space=pl.ANY),
                  pl.BlockSpec((None, d, tf), lambda b, f, be, nu: (be[blk(b, nu)], 0, col(b, f, nu))),
                  pl.BlockSpec((None, d, tf), lambda b, f, be, nu: (be[blk(b, nu)], 0, col(b, f, nu))),
                  pl.BlockSpec((None, tf, d), lambda b, f, be, nu: (be[blk(b, nu)], col(b, f, nu), 0))],
        out_specs=_slab_spec(MOE_TM, d, lambda b, f, be, nu: (b, 0)),
        scratch_shapes=[pltpu.VMEM((2, MOE_TM * (d // LANES), LANES), F32), pltpu.SemaphoreType.DMA((2,)),
                        pltpu.VMEM((MOE_TM, d), BF16), pltpu.VMEM((MOE_TM, d), F32)],
    )
    return pl.pallas_call(
        _moe_kernel,
        grid_spec=grid_spec,
        out_shape=_slab_shape(n_slots, d),
        compiler_params=_cparams("arbitrary", "arbitrary"),
        name="moe_ffn",
    )(blk_exp, n_used, tok, tok, hn, wg, wu, wd)


def _ple_kernel(*refs, routed, last):
    if routed:
        h1_ref, pos_ref, pos_next_ref, ys_ref, route_ref, p_ref, gp_ref, wg_ref, wp_ref, gn_ref = refs[:10]
        outs, (stage_ref, sem, mix_ref) = refs[10:-3], refs[-3:]
        tm, d = h1_ref.shape
        chunks = d // LANES
        i = pl.program_id(0)

        def row_copy(pos, r, slot):
            return pltpu.make_async_copy(ys_ref.at[pl.ds(pos[0, 0, r] * chunks, chunks)],
                                         stage_ref.at[slot, pl.ds(r * chunks, chunks)], sem.at[slot])

        def start_rows(pos, slot):
            def go(r, carry):
                row_copy(pos, 2 * r, slot).start(priority=0)
                row_copy(pos, 2 * r + 1, slot).start(priority=1)
                return carry
            lax.fori_loop(0, tm, go, 0, unroll=4)

        @pl.when(i == 0)
        def _():
            start_rows(pos_ref, 0)

        @pl.when(i + 1 < pl.num_programs(0))
        def _():
            start_rows(pos_next_ref, (i + 1) % 2)

        slot = i % 2

        pltpu.make_async_copy(ys_ref.at[pl.ds(0, 2 * tm * chunks)], stage_ref.at[slot], sem.at[slot]).wait()

        route = route_ref[...]
        g0, g1 = route[:, 2:3], route[:, 3:4]
        y_ref = stage_ref.at[slot]
        for c in range(chunks):
            mix_ref[:, c * LANES:(c + 1) * LANES] = (
                g0 * y_ref[pl.ds(c, tm, stride=chunks), :]
                + g1 * y_ref[pl.ds(tm * chunks + c, tm, stride=chunks), :])
        h2 = h1_ref[...] + mix_ref[...]
    else:
        h1_ref, y_ref, p_ref, gp_ref, wg_ref, wp_ref, gn_ref = refs[:7]
        outs = refs[7:]
        h2 = h1_ref[...] + y_ref[...]
    hp = _rms(h2, gp_ref[...]).astype(BF16)
    gate = jax.nn.sigmoid(jnp.dot(hp, wg_ref[...], preferred_element_type=F32))
    pp = jnp.dot(p_ref[...].astype(BF16), wp_ref[...], preferred_element_type=F32)
    h3 = h2 + gate * pp
    outs[0][...] = _rms(h3, gn_ref[...]) if last else h3


def _ple(h1, y, routing, p, g_ple, w_gate, w_proj, g_final, last, tm=256):
    n, d = h1.shape
    tm = _tile(n, tm)
    nt = n // tm
    routed = routing is not None
    row = lambda width: pl.BlockSpec((tm, width), lambda i: (i, 0))
    full = lambda a: pl.BlockSpec(a.shape, lambda i: (0, 0))
    g_ple, g_final = g_ple.reshape(1, d), g_final.reshape(1, d)
    if routed:
        route, pos = routing
        pos = pos.reshape(nt, tm, 2).transpose(0, 2, 1).reshape(nt, 1, 2 * tm)
        pos_spec = lambda nxt: pl.BlockSpec((1, 1, 2 * tm), lambda i: (jnp.minimum(i + nxt, nt - 1), 0, 0),
                                            memory_space=pltpu.SMEM)
        args = [h1, pos, pos, y, route]
        in_specs = [row(d), pos_spec(0), pos_spec(1), pl.BlockSpec(memory_space=pl.ANY), row(LANES)]
        scratch = [pltpu.VMEM((2, 2 * tm * (d // LANES), LANES), F32), pltpu.SemaphoreType.DMA((2,)),
                   pltpu.VMEM((tm, d), F32)]
    else:
        args = [h1, y]
        in_specs = [row(d), row(d)]
        scratch = []
    args += [p, g_ple, w_gate, w_proj, g_final]
    in_specs += [row(p.shape[1]), full(g_ple), full(w_gate), full(w_proj), full(g_final)]
    return pl.pallas_call(
        functools.partial(_ple_kernel, routed=routed, last=last),
        grid=(nt,),
        in_specs=in_specs, out_specs=[row(d)], out_shape=[jax.ShapeDtypeStruct((n, d), F32)],
        scratch_shapes=scratch,
        compiler_params=_cparams("arbitrary"),
        name="ple_routed" if routed else "ple",
    )(*args)


def _dispatch(route, n):
    idx = route[:, :2].astype(jnp.int32)
    flat_e = idx.reshape(-1)
    onehot = (flat_e[:, None] == jnp.arange(N_EXPERTS, dtype=jnp.int32)[None, :]).astype(jnp.int32)
    csum = jnp.cumsum(onehot, axis=0)
    rank = jnp.sum(onehot * csum, axis=1) - 1
    counts = csum[-1]
    padded = (counts + MOE_TM - 1) // MOE_TM * MOE_TM
    pad_end = jnp.cumsum(padded)
    pad_start = pad_end - padded
    dest = pad_start[flat_e] + rank
    n_blk = (2 * n) // MOE_TM + N_EXPERTS
    flat_tok = jnp.arange(2 * n, dtype=jnp.int32) // 2
    slot_tok = jnp.zeros((n_blk * MOE_TM,), jnp.int32).at[dest].set(flat_tok)
    blk_exp = jnp.minimum(jnp.searchsorted(pad_end, jnp.arange(n_blk, dtype=jnp.int32) * MOE_TM, side="right"),
                          N_EXPERTS - 1).astype(jnp.int32)
    n_used = (pad_end[-1] // MOE_TM).astype(jnp.int32).reshape(1)
    return slot_tok, blk_exp, n_used, dest.reshape(n, 2).astype(jnp.int32)


def _mla_weights(w_uq, w_ukv):
    lora = w_uq.shape[0]
    wq = w_uq.reshape(lora, GROUP_HEADS, MLA_NOPE + MLA_ROPE)
    wq = jnp.pad(wq, ((0, 0), (0, 0), (0, 2 * LANES - MLA_NOPE - MLA_ROPE)))
    wkv = w_ukv.reshape(lora, GROUP_HEADS, 2, HEAD_DIM).transpose(0, 2, 1, 3)
    return wq.reshape(lora, -1).astype(BF16), wkv.reshape(lora, -1).astype(BF16)


def kernel(x, p, attn_norm, w_in, diff_lq1, diff_lk1, diff_lq2, diff_lk2, diff_subln, na_rpb, swa_sinks,
           mla_q_norm, mla_kv_norm, mla_w_uq, mla_w_ukv, w_out, ffn_norm, dense_w_gate, dense_w_up,
           dense_w_down, moe_router, moe_w_gate, moe_w_up, moe_w_down, ple_norm, ple_gate, ple_proj,
           final_norm):
    batch, seq, d = x.shape
    n = batch * seq
    depth = w_in.shape[0]
    tabs = _rope_tables(seq)
    h = x.reshape(n, d)
    for i in range(depth):
        lambda_init = 0.8 - 0.6 * math.exp(-0.3 * i)
        w_in_i = jnp.pad(w_in[i].astype(BF16), ((0, 0), (0, IN_COLS_PAD - IN_COLS)))
        z = _inproj(h, attn_norm[i], w_in_i, tabs, seq)
        wq, wkv = _mla_weights(mla_w_uq[i], mla_w_ukv[i])
        q_l, k_l, v_l = _mla_up(z, mla_q_norm[i], mla_kv_norm[i], wq, wkv, tabs, seq)
        mixed = (
            _diff_attention(z, diff_lq1[i], diff_lk1[i], diff_lq2[i], diff_lk2[i], diff_subln[i],
                            lambda_init, batch, seq),
            _na_attention(z, na_rpb[i], batch, seq),
            _swa_attention(z, swa_sinks[i], batch, seq),
            _mla_attention(q_l, k_l, v_l, batch, seq),
        )
        j = i // 2
        routed = i % 2 == 1
        last = i == depth - 1
        if not routed:
            h1, hn = _outproj(mixed, w_out[i].astype(BF16), h, ffn_norm[i])
            y = _dense_ffn(hn, dense_w_gate[j].astype(BF16), dense_w_up[j].astype(BF16),
                           dense_w_down[j].astype(BF16))
            routing = None
        else:
            w_router = jnp.pad(moe_router[j], ((0, 0), (0, LANES - N_EXPERTS)))
            h1, hn, route = _outproj(mixed, w_out[i].astype(BF16), h, ffn_norm[i], w_router)
            slot_tok, blk_exp, n_used, pos = _dispatch(route, n)
            y = _moe_ffn(hn, slot_tok, blk_exp, n_used, moe_w_gate[j].astype(BF16),
                         moe_w_up[j].astype(BF16), moe_w_down[j].astype(BF16))
            routing = (route, pos)
        (h,) = _ple(h1, y, routing, p[i].reshape(n, -1), ple_norm[i], ple_gate[i].astype(BF16),
                    ple_proj[i].astype(BF16), final_norm, last)
    return h.reshape(batch, seq, d)
```

```python
import functools
import math

import numpy as np
import jax
import jax.numpy as jnp
from jax import lax
from jax.experimental import pallas as pl
from jax.experimental.pallas import tpu as pltpu

F32 = jnp.float32
BF16 = jnp.bfloat16

HEAD_DIM = 128
GROUP_HEADS = 4
GROUP_WIDTH = GROUP_HEADS * HEAD_DIM
ROPE_THETA = 10000.0
NORM_EPS = 1e-6
DIFF_QK_DIM = 64
GRID_W = 64
NA_WIN_ROWS = 8
NA_WIN_COLS = 16
SWA_WINDOW = 128
MLA_LORA = 512
MLA_NOPE = 128
MLA_ROPE = 64
IN_COLS = 5184
N_EXPERTS = 8

LANES = 128
VMEM_LIMIT = 56 * 1024 * 1024
MASKED = -1e30
LOG2E = math.log2(math.e)

IN_TN = 512
IN_COLS_PAD = -(-IN_COLS // LANES) * LANES
NA_GROUP_ROWS = 4
NA_KEY_ROWS = NA_GROUP_ROWS + NA_WIN_ROWS
NA_UNROLL = 4
SWA_TQ = 1024
SWA_SUB = 256
SWA_SPAN = SWA_SUB + 2 * SWA_WINDOW
MOE_TM = 512


def _tile(n, pref):
    t = min(n, pref)
    assert n % t == 0, (n, t)
    return t


def _cparams(*sem):
    return pltpu.CompilerParams(dimension_semantics=sem, vmem_limit_bytes=VMEM_LIMIT)


def _rms(x, g):
    return x * lax.rsqrt(jnp.mean(x * x, axis=-1, keepdims=True) + NORM_EPS) * g


def _slab_shape(rows, width):
    return jax.ShapeDtypeStruct((rows * (width // LANES), LANES), F32)


def _slab_spec(tm, width, index_map):
    return pl.BlockSpec((tm * (width // LANES), LANES), index_map)


def _slab_chunk(ref, c, tm, width):
    return ref[pl.ds(c, tm, stride=width // LANES), :]


def _slab_store(ref, x):
    tm, width = x.shape
    for c in range(width // LANES):
        ref[pl.ds(c, tm, stride=width // LANES), :] = x[:, c * LANES:(c + 1) * LANES]


def _rope_tables(seq):
    def base(dim):
        inv = ROPE_THETA ** (-jnp.arange(0, dim, 2, dtype=F32) / dim)
        ang = jnp.arange(seq, dtype=F32)[:, None] * inv[None, :]
        ang = jnp.concatenate([ang, ang], axis=-1)
        return jnp.cos(ang), jnp.sin(ang)

    lane = np.arange(LANES)
    cos, sin = base(DIFF_QK_DIM)
    c64 = jnp.concatenate([cos, cos], axis=-1)
    s64 = jnp.concatenate([sin, sin], axis=-1)
    lo = jnp.asarray((lane % 64) < 32)
    sa64 = jnp.where(lo, -s64, 0.0)
    sb64 = jnp.where(lo, 0.0, s64)
    first = jnp.asarray(lane < 64)
    c64p, sa64p, sb64p = (jnp.where(first, t, 0.0) for t in (c64, sa64, sb64))
    cos, sin = base(HEAD_DIM)
    s128 = jnp.where(jnp.asarray(lane < 64), -sin, sin)
    return dict(c64=c64, sa64=sa64, sb64=sb64, c64p=c64p, sa64p=sa64p, sb64p=sb64p,
                c128=cos, s128=s128)


def _rope64(x, c, sa, sb):
    return x * c + pltpu.roll(x, 96, 1) * sa + pltpu.roll(x, 32, 1) * sb


def _rope128(x, c, s):
    return x * c + pltpu.roll(x, 64, 1) * s


def _inproj_kernel(h_ref, g_ref, w_ref, c64, sa64, sb64, c128, s128, o_ref):
    x = _rms(h_ref[...], g_ref[...]).astype(BF16)
    for j in range(-(-IN_COLS_PAD // IN_TN)):
        width = min(IN_TN, IN_COLS_PAD - j * IN_TN)
        acc = jnp.dot(x, w_ref[:, j * IN_TN:j * IN_TN + width], preferred_element_type=F32)
        for g in range(width // LANES):
            a = acc[:, g * LANES:(g + 1) * LANES]
            if j == 0:
                a = _rope64(a, c64[...], sa64[...], sb64[...]) * (DIFF_QK_DIM ** -0.5 * LOG2E)
            elif j == 1:
                a = _rope64(a, c64[...], sa64[...], sb64[...])
            elif j == 3:
                a = a * HEAD_DIM ** -0.5
            elif j == 6:
                a = _rope128(a, c128[...], s128[...]) * HEAD_DIM ** -0.5
            elif j == 7 and g < 2:
                a = _rope128(a, c128[...], s128[...])
            o_ref[:, j * IN_TN + g * LANES:j * IN_TN + (g + 1) * LANES] = a.astype(o_ref.dtype)


def _inproj(h, g, w, tabs, seq, tm=512):
    n, d = h.shape
    tm = _tile(seq, tm)
    per_seq = seq // tm
    tab_spec = pl.BlockSpec((tm, LANES), lambda i: (i % per_seq, 0))
    g = g.reshape(1, d)
    return pl.pallas_call(
        _inproj_kernel,
        grid=(n // tm,),
        in_specs=[pl.BlockSpec((tm, d), lambda i: (i, 0)),
                  pl.BlockSpec(g.shape, lambda i: (0, 0)),
                  pl.BlockSpec(w.shape, lambda i: (0, 0), pipeline_mode=pl.Buffered(1))] + [tab_spec] * 5,
        out_specs=pl.BlockSpec((tm, IN_COLS_PAD), lambda i: (i, 0)),
        out_shape=jax.ShapeDtypeStruct((n, IN_COLS_PAD), BF16),
        compiler_params=_cparams("parallel"),
        name="inproj",
    )(h, g, w, tabs["c64"], tabs["sa64"], tabs["sb64"], tabs["c128"], tabs["s128"])


def _mla_up_kernel(cq_ref, ckv_ref, kr_ref, gq_ref, gkv_ref, wq_ref, wkv_ref, c_ref, sa_ref, sb_ref,
                   q_ref, k_ref, v_ref):
    scale = (MLA_NOPE + MLA_ROPE) ** -0.5 * LOG2E
    c, sa, sb = c_ref[...], sa_ref[...], sb_ref[...]
    cq = _rms(cq_ref[...].astype(F32), gq_ref[...]).astype(BF16)
    qa = jnp.dot(cq, wq_ref[...], preferred_element_type=F32)
    for h in range(GROUP_HEADS):
        base = h * 2 * LANES
        q_ref[:, base:base + LANES] = (qa[:, base:base + LANES] * scale).astype(q_ref.dtype)
        q_ref[:, base + LANES:base + 2 * LANES] = (
            _rope64(qa[:, base + LANES:base + 2 * LANES], c, sa, sb) * scale).astype(q_ref.dtype)
    ckv = _rms(ckv_ref[...].astype(F32), gkv_ref[...]).astype(BF16)
    kva = jnp.dot(ckv, wkv_ref[...], preferred_element_type=F32)
    kr = _rope64(kr_ref[...].astype(F32), c, sa, sb).astype(k_ref.dtype)
    for h in range(GROUP_HEADS):
        base = h * 2 * LANES
        k_ref[:, base:base + LANES] = kva[:, h * LANES:(h + 1) * LANES].astype(k_ref.dtype)
        k_ref[:, base + LANES:base + 2 * LANES] = kr
    v_ref[...] = kva[:, GROUP_WIDTH:].astype(v_ref.dtype)


def _mla_up(z, gq, gkv, wq, wkv, tabs, seq, tm=512):
    n = z.shape[0]
    tm = _tile(seq, tm)
    per_seq = seq // tm
    tab_spec = pl.BlockSpec((tm, LANES), lambda i: (i % per_seq, 0))
    full = lambda a: pl.BlockSpec(a.shape, lambda i: (0,) * a.ndim)
    gq = gq.reshape(1, -1)
    gkv = gkv.reshape(1, -1)
    return pl.pallas_call(
        _mla_up_kernel,
        grid=(n // tm,),
        in_specs=[pl.BlockSpec((tm, MLA_LORA), lambda i: (i, 8)),
                  pl.BlockSpec((tm, MLA_LORA), lambda i: (i, 9)),
                  pl.BlockSpec((tm, LANES), lambda i: (i, 40)),
                  full(gq), full(gkv), full(wq), full(wkv), tab_spec, tab_spec, tab_spec],
        out_specs=[pl.BlockSpec((tm, 2 * GROUP_WIDTH), lambda i: (i, 0)),
                   pl.BlockSpec((tm, 2 * GROUP_WIDTH), lambda i: (i, 0)),
                   pl.BlockSpec((tm, GROUP_WIDTH), lambda i: (i, 0))],
        out_shape=[jax.ShapeDtypeStruct((n, 2 * GROUP_WIDTH), BF16),
                   jax.ShapeDtypeStruct((n, 2 * GROUP_WIDTH), BF16),
                   jax.ShapeDtypeStruct((n, GROUP_WIDTH), BF16)],
        compiler_params=_cparams("parallel"),
        name="mla_up",
    )(z, z, z, gq, gkv, wq, wkv, tabs["c64p"], tabs["sa64p"], tabs["sb64p"])


def _qk(q, k):
    return lax.dot_general(q, k, (((1,), (1,)), ((), ())), preferred_element_type=F32)


def _flash_core(q_ref, k_ref, v_ref, vt_ref, s_ref, m_ref, l_ref, acc_ref, tk):
    nt = k_ref.shape[0] // tk
    assert nt % 2 == 0

    @pl.when(pl.program_id(2) == 0)
    def _():
        vt_ref[...] = v_ref[...].T

    m_ref[...] = jnp.full(m_ref.shape, MASKED, F32)
    l_ref[...] = jnp.zeros(l_ref.shape, F32)
    acc_ref[...] = jnp.zeros(acc_ref.shape, F32)

    def scores(t, slot):
        off = pl.multiple_of(jnp.minimum(t, nt - 1) * tk, tk)
        s_ref[slot] = _qk(k_ref[pl.ds(off, tk), :], q_ref[...])

    def consume(t, slot):
        off = pl.multiple_of(t * tk, tk)
        s = s_ref[slot]
        m_prev = m_ref[...]
        m_new = jnp.maximum(m_prev, jnp.max(s, axis=0, keepdims=True))
        alpha = jnp.exp2(m_prev - m_new)
        p = jnp.exp2(s - m_new)
        l_ref[...] = alpha * l_ref[...] + jnp.sum(p, axis=0, keepdims=True)
        acc_ref[...] = alpha * acc_ref[...] + jnp.dot(vt_ref[:, pl.ds(off, tk)], p.astype(vt_ref.dtype),
                                                      preferred_element_type=F32)
        m_ref[...] = m_new

    scores(0, 0)

    def body(j, carry):
        scores(2 * j + 1, 1)
        consume(2 * j, 0)
        scores(2 * j + 2, 0)
        consume(2 * j + 1, 1)
        return carry

    lax.fori_loop(0, nt // 2, body, 0, unroll=2)


def _flash_scratch(seq, tk, queries):
    return [pltpu.VMEM((LANES, seq), BF16), pltpu.VMEM((2, tk, queries), F32),
            pltpu.VMEM((1, queries), F32), pltpu.VMEM((1, queries), F32), pltpu.VMEM((LANES, queries), F32)]


def _flash_kernel(q_ref, k_ref, v_ref, o_ref, vt_ref, s_ref, m_ref, l_ref, acc_ref, *, tk):
    _flash_core(q_ref, k_ref, v_ref, vt_ref, s_ref, m_ref, l_ref, acc_ref, tk)
    o_ref[...] = (acc_ref[...] / l_ref[...]).T.astype(o_ref.dtype)


def _mla_attention(q, k, v, batch, seq, tq=1024, tk=512):
    n = q.shape[0]
    tq, tk = _tile(seq, tq), _tile(seq, tk)
    nq = seq // tq
    dq = 2 * LANES
    return pl.pallas_call(
        functools.partial(_flash_kernel, tk=tk),
        grid=(batch, GROUP_HEADS, nq),
        in_specs=[pl.BlockSpec((tq, dq), lambda b, h, i: (b * nq + i, h)),
                  pl.BlockSpec((seq, dq), lambda b, h, i: (b, h)),
                  pl.BlockSpec((seq, LANES), lambda b, h, i: (b, h))],
        out_specs=pl.BlockSpec((tq, LANES), lambda b, h, i: (b * nq + i, h)),
        out_shape=jax.ShapeDtypeStruct((n, GROUP_WIDTH), BF16),
        scratch_shapes=_flash_scratch(seq, tk, tq),
        compiler_params=_cparams("parallel", "parallel", "arbitrary"),
        name="mla_attention",
    )(q, k, v)


def _diff_kernel(lq1_ref, lk1_ref, lq2_ref, lk2_ref, g_ref, q_ref, k_ref, v_ref, o_ref,
                 qq_ref, vt_ref, s_ref, m_ref, l_ref, acc_ref, *, tk, lambda_init):
    tq = q_ref.shape[0]
    q = q_ref[...]
    lane = lax.broadcasted_iota(jnp.int32, q.shape, 1)
    qq_ref[:tq, :] = jnp.where(lane < DIFF_QK_DIM, q, jnp.zeros_like(q))
    qq_ref[tq:, :] = jnp.where(lane >= DIFF_QK_DIM, q, jnp.zeros_like(q))
    _flash_core(qq_ref, k_ref, v_ref, vt_ref, s_ref, m_ref, l_ref, acc_ref, tk)
    lam = (jnp.exp(jnp.sum(lq1_ref[...] * lk1_ref[...], axis=-1, keepdims=True))
           - jnp.exp(jnp.sum(lq2_ref[...] * lk2_ref[...], axis=-1, keepdims=True)) + lambda_init)
    o = (acc_ref[:, :tq] / l_ref[:, :tq] - lam * (acc_ref[:, tq:] / l_ref[:, tq:])).T
    o_ref[...] = (_rms(o, g_ref[...]) * (1.0 - lambda_init)).astype(o_ref.dtype)


def _diff_attention(z, lq1, lk1, lq2, lk2, subln, lambda_init, batch, seq, tq=512, tk=512):
    n = z.shape[0]
    tq, tk = _tile(seq, tq), _tile(seq, tk)
    nq = seq // tq
    vec = lambda a: a.reshape(1, -1)
    small = lambda a: pl.BlockSpec(a.shape, lambda b, h, i: (0, 0))
    params = [vec(a) for a in (lq1, lk1, lq2, lk2, subln)]
    return pl.pallas_call(
        functools.partial(_diff_kernel, tk=tk, lambda_init=lambda_init),
        grid=(batch, GROUP_HEADS, nq),
        in_specs=[small(a) for a in params] + [
            pl.BlockSpec((tq, LANES), lambda b, h, i: (b * nq + i, h)),
            pl.BlockSpec((seq, LANES), lambda b, h, i: (b, 4 + h)),
            pl.BlockSpec((seq, LANES), lambda b, h, i: (b, 8 + h))],
        out_specs=pl.BlockSpec((tq, LANES), lambda b, h, i: (b * nq + i, h)),
        out_shape=jax.ShapeDtypeStruct((n, GROUP_WIDTH), BF16),
        scratch_shapes=[pltpu.VMEM((2 * tq, LANES), BF16)] + _flash_scratch(seq, tk, 2 * tq),
        compiler_params=_cparams("parallel", "parallel", "arbitrary"),
        name="diff_attention",
    )(*params, z, z, z)


def _na_bias_tables(rpb, rows):
    groups = rows // NA_GROUP_ROWS
    assert groups >= 3
    qc = np.arange(GRID_W)
    kc = np.arange(GRID_W)
    ws = np.clip(qc - NA_WIN_COLS // 2, 0, GRID_W - NA_WIN_COLS)
    col_ok = (kc[None, :] >= ws[:, None]) & (kc[None, :] < ws[:, None] + NA_WIN_COLS)
    edge = GRID_W - NA_WIN_COLS
    rp = jnp.pad(rpb.astype(F32), ((0, 0), (0, 0), (edge, edge)))
    toep = jnp.stack([rp[:, :, GRID_W - 1 - c:2 * GRID_W - 1 - c] for c in range(GRID_W)], axis=2)
    toep = jnp.where(jnp.asarray(col_ok)[None, None], toep, MASKED)
    masked = jnp.full((rpb.shape[0], GRID_W, GRID_W), MASKED, F32)
    tabs = []
    for m in (0, 1, groups - 1):
        kstart = np.clip(NA_GROUP_ROWS * m - NA_WIN_ROWS // 2, 0, rows - NA_KEY_ROWS)
        qr = NA_GROUP_ROWS * m + np.arange(NA_GROUP_ROWS)
        kr = kstart + np.arange(NA_KEY_ROWS)
        rs = np.clip(qr - NA_WIN_ROWS // 2, 0, rows - NA_WIN_ROWS)
        row_ok = (kr[None, :] >= rs[:, None]) & (kr[None, :] < rs[:, None] + NA_WIN_ROWS)
        ridx = kr[None, :] - qr[:, None] + NA_WIN_ROWS - 1
        tabs.append(jnp.concatenate([
            jnp.concatenate([toep[:, ridx[i, a]] if row_ok[i, a] else masked
                             for a in range(NA_KEY_ROWS)], axis=-1)
            for i in range(NA_GROUP_ROWS)], axis=1))
    return jnp.stack(tabs)


def _na_kernel(bias_ref, q_ref, k_ref, v_ref, o_ref, *, rows):
    groups = rows // NA_GROUP_ROWS
    tq = NA_GROUP_ROWS * GRID_W
    span = NA_KEY_ROWS * GRID_W

    def body(m, carry):
        cls = jnp.where(m == 0, 0, jnp.where(m == groups - 1, 2, 1))
        krow = jnp.clip(NA_GROUP_ROWS * m - NA_WIN_ROWS // 2, 0, rows - NA_KEY_ROWS)
        koff = pl.multiple_of(krow * GRID_W, GRID_W)
        qoff = pl.multiple_of(m * tq, tq)
        s = _qk(q_ref[pl.ds(qoff, tq), :], k_ref[pl.ds(koff, span), :]) + bias_ref[cls, 0]
        p = jnp.exp(s - jnp.max(s, axis=-1, keepdims=True))
        l = jnp.sum(p, axis=-1, keepdims=True)
        v = v_ref[pl.ds(koff, span), :]
        o = jnp.dot(p.astype(v.dtype), v, preferred_element_type=F32) / l
        o_ref[pl.ds(qoff, tq), :] = o.astype(o_ref.dtype)
        return carry

    lax.fori_loop(0, groups, body, 0, unroll=NA_UNROLL)


def _na_attention(z, rpb, batch, seq):
    n = z.shape[0]
    rows = seq // GRID_W
    bias = _na_bias_tables(rpb, rows)
    tq, span = NA_GROUP_ROWS * GRID_W, NA_KEY_ROWS * GRID_W
    return pl.pallas_call(
        functools.partial(_na_kernel, rows=rows),
        grid=(batch, GROUP_HEADS),
        in_specs=[pl.BlockSpec((3, 1, tq, span), lambda b, h: (0, h, 0, 0)),
                  pl.BlockSpec((seq, LANES), lambda b, h: (b, 12 + h)),
                  pl.BlockSpec((seq, LANES), lambda b, h: (b, 16 + h)),
                  pl.BlockSpec((seq, LANES), lambda b, h: (b, 20 + h))],
        out_specs=pl.BlockSpec((seq, LANES), lambda b, h: (b, h)),
        out_shape=jax.ShapeDtypeStruct((n, GROUP_WIDTH), BF16),
        compiler_params=_cparams("parallel", "parallel"),
        name="na_attention",
    )(bias, z, z, z)


def _swa_kernel(sink_ref, q_ref, k_ref, v_ref, o_ref, *, seq):
    sink = sink_ref[pl.program_id(1)]
    for c in range(SWA_TQ // SWA_SUB):
        rows = slice(c * SWA_SUB, (c + 1) * SWA_SUB)
        q0 = pl.program_id(2) * SWA_TQ + c * SWA_SUB
        koff = pl.multiple_of(jnp.clip(q0 - SWA_WINDOW, 0, seq - SWA_SPAN), SWA_WINDOW)
        s = _qk(q_ref[rows, :], k_ref[pl.ds(koff, SWA_SPAN), :])
        qpos = q0 + lax.broadcasted_iota(jnp.int32, s.shape, 0)
        kpos = koff + lax.broadcasted_iota(jnp.int32, s.shape, 1)
        s = jnp.where(jnp.abs(qpos - kpos) <= SWA_WINDOW, s, MASKED)
        m = jnp.maximum(jnp.max(s, axis=-1, keepdims=True), sink)
        p = jnp.exp(s - m)
        l = jnp.sum(p, axis=-1, keepdims=True) + jnp.exp(sink - m)
        v = v_ref[pl.ds(koff, SWA_SPAN), :]
        o_ref[rows, :] = (jnp.dot(p.astype(v.dtype), v, preferred_element_type=F32) / l).astype(o_ref.dtype)


def _swa_attention(z, sinks, batch, seq):
    n = z.shape[0]
    assert seq % SWA_TQ == 0 and seq >= SWA_SPAN
    nq = seq // SWA_TQ
    return pl.pallas_call(
        functools.partial(_swa_kernel, seq=seq),
        grid=(batch, GROUP_HEADS, nq),
        in_specs=[pl.BlockSpec(memory_space=pltpu.SMEM),
                  pl.BlockSpec((SWA_TQ, LANES), lambda b, h, i: (b * nq + i, 24 + h)),
                  pl.BlockSpec((seq, LANES), lambda b, h, i: (b, 28 + h // 2)),
                  pl.BlockSpec((seq, LANES), lambda b, h, i: (b, 30 + h // 2))],
        out_specs=pl.BlockSpec((SWA_TQ, LANES), lambda b, h, i: (b * nq + i, h)),
        out_shape=jax.ShapeDtypeStruct((n, GROUP_WIDTH), BF16),
        compiler_params=_cparams("parallel", "parallel", "arbitrary"),
        name="swa_attention",
    )(sinks, z, z, z)


def _dot_f32(x, w):
    xh = x.astype(BF16)
    xl = (x - xh.astype(F32)).astype(BF16)
    wh = w.astype(BF16)
    wl = (w - wh.astype(F32)).astype(BF16)
    d = lambda a, b: jnp.dot(a, b, preferred_element_type=F32)
    return d(xh, wh) + (d(xh, wl) + d(xl, wh))


def _top2(logits):
    lane = lax.broadcasted_iota(jnp.int32, logits.shape, 1)
    l1 = jnp.where(lane < N_EXPERTS, logits, MASKED)
    m1 = jnp.max(l1, axis=-1, keepdims=True)
    i1 = jnp.min(jnp.where(l1 == m1, lane, LANES), axis=-1, keepdims=True)
    l2 = jnp.where(lane == i1, MASKED, l1)
    m2 = jnp.max(l2, axis=-1, keepdims=True)
    i2 = jnp.min(jnp.where(l2 == m2, lane, LANES), axis=-1, keepdims=True)
    e2 = jnp.exp(m2 - m1)
    den = 1.0 + e2
    out = jnp.where(lane == 0, i1.astype(F32), 0.0)
    out = jnp.where(lane == 1, i2.astype(F32), out)
    out = jnp.where(lane == 2, 1.0 / den, out)
    return jnp.where(lane == 3, e2 / den, out)


def _outproj_kernel(*refs, routed):
    if routed:
        md, mn, ms, ml, w_ref, h_ref, g_ref, wr_ref, h1_ref, hn_ref, route_ref = refs
    else:
        md, mn, ms, ml, w_ref, h_ref, g_ref, h1_ref, hn_ref = refs
    acc = h_ref[...]
    for k, m in enumerate((md, mn, ms, ml)):
        acc = acc + jnp.dot(m[...], w_ref[k * GROUP_WIDTH:(k + 1) * GROUP_WIDTH, :],
                            preferred_element_type=F32)
    h1_ref[...] = acc
    hn = _rms(acc, g_ref[...])
    if routed:
        _slab_store(hn_ref, hn)
        route_ref[...] = _top2(_dot_f32(hn, wr_ref[...]))
    else:
        hn_ref[...] = hn.astype(hn_ref.dtype)


def _outproj(mixed, w, h, g, w_router=None, tm=256):
    n, d = h.shape
    tm = _tile(n, tm)
    routed = w_router is not None
    row = lambda width: pl.BlockSpec((tm, width), lambda i: (i, 0))
    full = lambda a: pl.BlockSpec(a.shape, lambda i: (0, 0))
    g = g.reshape(1, d)
    args = list(mixed) + [w, h, g]
    in_specs = [row(GROUP_WIDTH)] * 4 + [full(w), row(d), full(g)]
    if routed:
        args.append(w_router)
        in_specs.append(full(w_router))
        out_specs = [row(d), _slab_spec(tm, d, lambda i: (i, 0)), row(LANES)]
        out_shape = [jax.ShapeDtypeStruct((n, d), F32), _slab_shape(n, d),
                     jax.ShapeDtypeStruct((n, LANES), F32)]
    else:
        out_specs = [row(d), row(d)]
        out_shape = [jax.ShapeDtypeStruct((n, d), F32), jax.ShapeDtypeStruct((n, d), BF16)]
    return pl.pallas_call(
        functools.partial(_outproj_kernel, routed=routed),
        grid=(n // tm,),
        in_specs=in_specs, out_specs=out_specs, out_shape=out_shape,
        compiler_params=_cparams("parallel"),
        name="outproj_routed" if routed else "outproj",
    )(*args)


def _swiglu_step(x_ref, wg_ref, wu_ref, wd_ref, o_ref, a_ref, f):
    nf = pl.num_programs(1) - 1

    def gate_up():
        x = x_ref[...]
        g = jnp.dot(x, wg_ref[...], preferred_element_type=F32)
        u = jnp.dot(x, wu_ref[...], preferred_element_type=F32)
        return (g * jax.nn.sigmoid(g) * u).astype(BF16)

    @pl.when(f == 0)
    def _():
        o_ref[...] = jnp.zeros(o_ref.shape, o_ref.dtype)
        a_ref[...] = gate_up()

    @pl.when((f > 0) & (f < nf))
    def _():
        y = jnp.dot(a_ref[...], wd_ref[...], preferred_element_type=F32)
        a_next = gate_up()
        o_ref[...] += y
        a_ref[...] = a_next

    @pl.when(f == nf)
    def _():
        o_ref[...] += jnp.dot(a_ref[...], wd_ref[...], preferred_element_type=F32)


def _ffn_kernel(x_ref, wg_ref, wu_ref, wd_ref, o_ref, a_ref):
    _swiglu_step(x_ref, wg_ref, wu_ref, wd_ref, o_ref, a_ref, pl.program_id(1))


def _dense_ffn(x, wg, wu, wd, tm=512, tf=1024):
    n, d = x.shape
    ff = wg.shape[1]
    tm, tf = _tile(n, tm), _tile(ff, tf)
    nf = ff // tf
    return pl.pallas_call(
        _ffn_kernel,
        grid=(n // tm, nf + 1),
        in_specs=[pl.BlockSpec((tm, d), lambda i, f: (i, 0)),
                  pl.BlockSpec((d, tf), lambda i, f: (0, jnp.minimum(f, nf - 1))),
                  pl.BlockSpec((d, tf), lambda i, f: (0, jnp.minimum(f, nf - 1))),
                  pl.BlockSpec((tf, d), lambda i, f: (jnp.maximum(f - 1, 0), 0))],
        out_specs=pl.BlockSpec((tm, d), lambda i, f: (i, 0)),
        out_shape=jax.ShapeDtypeStruct((n, d), F32),
        scratch_shapes=[pltpu.VMEM((tm, tf), BF16)],
        compiler_params=_cparams("parallel", "arbitrary"),
        name="dense_ffn",
    )(x, wg, wu, wd)


def _moe_kernel(blk_exp_ref, n_used_ref, tok_ref, tok_next_ref, hn_ref, wg_ref, wu_ref, wd_ref, o_ref,
                stage_ref, sem, xb_ref, acc_ref, a_ref):
    b, f = pl.program_id(0), pl.program_id(1)
    tm, d = xb_ref.shape
    chunks = d // LANES
    n_used = n_used_ref[0]
    used = b < n_used
    last = f == pl.num_programs(1) - 1

    def row_copy(tok, r, slot):
        return pltpu.make_async_copy(hn_ref.at[pl.ds(tok[0, 0, r] * chunks, chunks)],
                                     stage_ref.at[slot, pl.ds(r * chunks, chunks)], sem.at[slot])

    def start_block(tok, slot):
        def go(r, carry):
            row_copy(tok, r, slot).start()
            return carry
        lax.fori_loop(0, tm, go, 0, unroll=8)

    def wait_block(slot):
        pltpu.make_async_copy(hn_ref.at[pl.ds(0, tm * chunks)], stage_ref.at[slot], sem.at[slot]).wait()

    @pl.when(used & (b == 0) & (f == 0))
    def _():
        start_block(tok_ref, 0)

    @pl.when(used & (f == 0))
    def _():
        slot = b % 2
        wait_block(slot)
        for c in range(chunks):
            xb_ref[:, c * LANES:(c + 1) * LANES] = _slab_chunk(stage_ref.at[slot], c, tm, d).astype(BF16)

    @pl.when((b + 1 < n_used) & (f == 1))
    def _():
        start_block(tok_next_ref, (b + 1) % 2)

    @pl.when(used)
    def _():
        _swiglu_step(xb_ref, wg_ref, wu_ref, wd_ref, acc_ref, a_ref, f)

    @pl.when(used & last)
    def _():
        _slab_store(o_ref, acc_ref[...])

    @pl.when(jnp.logical_not(used) & last)
    def _():
        o_ref[...] = jnp.zeros(o_ref.shape, o_ref.dtype)


def _moe_ffn(hn, slot_tok, blk_exp, n_used, wg, wu, wd, tf=1024):
    d, ff = wg.shape[1], wg.shape[2]
    n_slots = slot_tok.shape[0]
    tf = _tile(ff, tf)
    nf = ff // tf
    assert nf >= 2
    n_blk = n_slots // MOE_TM
    tok = slot_tok.reshape(n_blk, 1, MOE_TM)
    tok_spec = lambda nxt: pl.BlockSpec((1, 1, MOE_TM), lambda b, f, be, nu: (jnp.minimum(b + nxt, n_blk - 1), 0, 0),
                                        memory_space=pltpu.SMEM)

    def blk(b, nu):
        return jnp.minimum(b, nu[0] - 1)

    def col(b, f, nu, skew):
        return jnp.where(b < nu[0], jnp.clip(f - skew, 0, nf - 1), nf - 1)

    grid_spec = pltpu.PrefetchScalarGridSpec(
        num_scalar_prefetch=2,
        grid=(n_blk, nf + 1),
        in_specs=[tok_spec(0), tok_spec(1), pl.BlockSpec(memory_space=pl.ANY),
                  pl.BlockSpec((None, d, tf), lambda b, f, be, nu: (be[blk(b, nu)], 0, col(b, f, nu, 0))),
                  pl.BlockSpec((None, d, tf), lambda b, f, be, nu: (be[blk(b, nu)], 0, col(b, f, nu, 0))),
                  pl.BlockSpec((None, tf, d), lambda b, f, be, nu: (be[blk(b, nu)], col(b, f, nu, 1), 0))],
        out_specs=_slab_spec(MOE_TM, d, lambda b, f, be, nu: (b, 0)),
        scratch_shapes=[pltpu.VMEM((2, MOE_TM * (d // LANES), LANES), F32), pltpu.SemaphoreType.DMA((2,)),
                        pltpu.VMEM((MOE_TM, d), BF16), pltpu.VMEM((MOE_TM, d), F32),
                        pltpu.VMEM((MOE_TM, tf), BF16)],
    )
    return pl.pallas_call(
        _moe_kernel,
        grid_spec=grid_spec,
        out_shape=_slab_shape(n_slots, d),
        compiler_params=_cparams("arbitrary", "arbitrary"),
        name="moe_ffn",
    )(blk_exp, n_used, tok, tok, hn, wg, wu, wd)


def _ple_kernel(*refs, routed, last):
    if routed:
        h1_ref, pos_ref, pos_next_ref, ys_ref, route_ref, p_ref, gp_ref, wg_ref, wp_ref, gn_ref = refs[:10]
        outs, (stage_ref, sem, mix_ref) = refs[10:-3], refs[-3:]
        tm, d = h1_ref.shape
        chunks = d // LANES
        i = pl.program_id(0)

        def row_copy(pos, r, slot):
            return pltpu.make_async_copy(ys_ref.at[pl.ds(pos[0, 0, r] * chunks, chunks)],
                                         stage_ref.at[slot, pl.ds(r * chunks, chunks)], sem.at[slot])

        def start_rows(pos, slot):
            def go(r, carry):
                row_copy(pos, 2 * r, slot).start(priority=0)
                row_copy(pos, 2 * r + 1, slot).start(priority=1)
                return carry
            lax.fori_loop(0, tm, go, 0, unroll=4)

        @pl.when(i == 0)
        def _():
            start_rows(pos_ref, 0)

        @pl.when(i + 1 < pl.num_programs(0))
        def _():
            start_rows(pos_next_ref, (i + 1) % 2)

        slot = i % 2

        pltpu.make_async_copy(ys_ref.at[pl.ds(0, 2 * tm * chunks)], stage_ref.at[slot], sem.at[slot]).wait()

        route = route_ref[...]
        g0, g1 = route[:, 2:3], route[:, 3:4]
        y_ref = stage_ref.at[slot]
        for c in range(chunks):
            mix_ref[:, c * LANES:(c + 1) * LANES] = (
                g0 * y_ref[pl.ds(c, tm, stride=chunks), :]
                + g1 * y_ref[pl.ds(tm * chunks + c, tm, stride=chunks), :])
        h2 = h1_ref[...] + mix_ref[...]
    else:
        h1_ref, y_ref, p_ref, gp_ref, wg_ref, wp_ref, gn_ref = refs[:7]
        outs = refs[7:]
        h2 = h1_ref[...] + y_ref[...]
    hp = _rms(h2, gp_ref[...]).astype(BF16)
    gate = jax.nn.sigmoid(jnp.dot(hp, wg_ref[...], preferred_element_type=F32))
    pp = jnp.dot(p_ref[...].astype(BF16), wp_ref[...], preferred_element_type=F32)
    h3 = h2 + gate * pp
    outs[0][...] = _rms(h3, gn_ref[...]) if last else h3


def _ple(h1, y, routing, p, g_ple, w_gate, w_proj, g_final, last, tm=256):
    n, d = h1.shape
    tm = _tile(n, tm)
    nt = n // tm
    routed = routing is not None
    row = lambda width: pl.BlockSpec((tm, width), lambda i: (i, 0))
    full = lambda a: pl.BlockSpec(a.shape, lambda i: (0, 0))
    g_ple, g_final = g_ple.reshape(1, d), g_final.reshape(1, d)
    if routed:
        route, pos = routing
        pos = pos.reshape(nt, tm, 2).transpose(0, 2, 1).reshape(nt, 1, 2 * tm)
        pos_spec = lambda nxt: pl.BlockSpec((1, 1, 2 * tm), lambda i: (jnp.minimum(i + nxt, nt - 1), 0, 0),
                                            memory_space=pltpu.SMEM)
        args = [h1, pos, pos, y, route]
        in_specs = [row(d), pos_spec(0), pos_spec(1), pl.BlockSpec(memory_space=pl.ANY), row(LANES)]
        scratch = [pltpu.VMEM((2, 2 * tm * (d // LANES), LANES), F32), pltpu.SemaphoreType.DMA((2,)),
                   pltpu.VMEM((tm, d), F32)]
    else:
        args = [h1, y]
        in_specs = [row(d), row(d)]
        scratch = []
    args += [p, g_ple, w_gate, w_proj, g_final]
    in_specs += [row(p.shape[1]), full(g_ple), full(w_gate), full(w_proj), full(g_final)]
    return pl.pallas_call(
        functools.partial(_ple_kernel, routed=routed, last=last),
        grid=(nt,),
        in_specs=in_specs, out_specs=[row(d)], out_shape=[jax.ShapeDtypeStruct((n, d), F32)],
        scratch_shapes=scratch,
        compiler_params=_cparams("arbitrary"),
        name="ple_routed" if routed else "ple",
    )(*args)


def _dispatch(route, n):
    idx = route[:, :2].astype(jnp.int32)
    flat_e = idx.reshape(-1)
    onehot = (flat_e[:, None] == jnp.arange(N_EXPERTS, dtype=jnp.int32)[None, :]).astype(jnp.int32)
    csum = jnp.cumsum(onehot, axis=0)
    rank = jnp.sum(onehot * csum, axis=1) - 1
    counts = csum[-1]
    padded = (counts + MOE_TM - 1) // MOE_TM * MOE_TM
    pad_end = jnp.cumsum(padded)
    pad_start = pad_end - padded
    dest = pad_start[flat_e] + rank
    n_blk = (2 * n) // MOE_TM + N_EXPERTS
    flat_tok = jnp.arange(2 * n, dtype=jnp.int32) // 2
    slot_tok = jnp.zeros((n_blk * MOE_TM,), jnp.int32).at[dest].set(flat_tok)
    blk_exp = jnp.minimum(jnp.searchsorted(pad_end, jnp.arange(n_blk, dtype=jnp.int32) * MOE_TM, side="right"),
                          N_EXPERTS - 1).astype(jnp.int32)
    n_used = (pad_end[-1] // MOE_TM).astype(jnp.int32).reshape(1)
    return slot_tok, blk_exp, n_used, dest.reshape(n, 2).astype(jnp.int32)


def _mla_weights(w_uq, w_ukv):
    lora = w_uq.shape[0]
    wq = w_uq.reshape(lora, GROUP_HEADS, MLA_NOPE + MLA_ROPE)
    wq = jnp.pad(wq, ((0, 0), (0, 0), (0, 2 * LANES - MLA_NOPE - MLA_ROPE)))
    wkv = w_ukv.reshape(lora, GROUP_HEADS, 2, HEAD_DIM).transpose(0, 2, 1, 3)
    return wq.reshape(lora, -1).astype(BF16), wkv.reshape(lora, -1).astype(BF16)


def kernel(x, p, attn_norm, w_in, diff_lq1, diff_lk1, diff_lq2, diff_lk2, diff_subln, na_rpb, swa_sinks,
           mla_q_norm, mla_kv_norm, mla_w_uq, mla_w_ukv, w_out, ffn_norm, dense_w_gate, dense_w_up,
           dense_w_down, moe_router, moe_w_gate, moe_w_up, moe_w_down, ple_norm, ple_gate, ple_proj,
           final_norm):
    batch, seq, d = x.shape
    n = batch * seq
    depth = w_in.shape[0]
    tabs = _rope_tables(seq)
    h = x.reshape(n, d)
    for i in range(depth):
        lambda_init = 0.8 - 0.6 * math.exp(-0.3 * i)
        w_in_i = jnp.pad(w_in[i].astype(BF16), ((0, 0), (0, IN_COLS_PAD - IN_COLS)))
        z = _inproj(h, attn_norm[i], w_in_i, tabs, seq)
        wq, wkv = _mla_weights(mla_w_uq[i], mla_w_ukv[i])
        q_l, k_l, v_l = _mla_up(z, mla_q_norm[i], mla_kv_norm[i], wq, wkv, tabs, seq)
        mixed = (
            _diff_attention(z, diff_lq1[i], diff_lk1[i], diff_lq2[i], diff_lk2[i], diff_subln[i],
                            lambda_init, batch, seq),
            _na_attention(z, na_rpb[i], batch, seq),
            _swa_attention(z, swa_sinks[i], batch, seq),
            _mla_attention(q_l, k_l, v_l, batch, seq),
        )
        j = i // 2
        routed = i % 2 == 1
        last = i == depth - 1
        if not routed:
            h1, hn = _outproj(mixed, w_out[i].astype(BF16), h, ffn_norm[i])
            y = _dense_ffn(hn, dense_w_gate[j].astype(BF16), dense_w_up[j].astype(BF16),
                           dense_w_down[j].astype(BF16))
            routing = None
        else:
            w_router = jnp.pad(moe_router[j], ((0, 0), (0, LANES - N_EXPERTS)))
            h1, hn, route = _outproj(mixed, w_out[i].astype(BF16), h, ffn_norm[i], w_router)
            slot_tok, blk_exp, n_used, pos = _dispatch(route, n)
            y = _moe_ffn(hn, slot_tok, blk_exp, n_used, moe_w_gate[j].astype(BF16),
                         moe_w_up[j].astype(BF16), moe_w_down[j].astype(BF16))
            routing = (route, pos)
        (h,) = _ple(h1, y, routing, p[i].reshape(n, -1), ple_norm[i], ple_gate[i].astype(BF16),
                    ple_proj[i].astype(BF16), final_norm, last)
    return h.reshape(batch, seq, d)
```

```python
import functools
import math

import numpy as np
import jax
import jax.numpy as jnp
from jax import lax
from jax.experimental import pallas as pl
from jax.experimental.pallas import tpu as pltpu

F32 = jnp.float32
BF16 = jnp.bfloat16

HEAD_DIM = 128
GROUP_HEADS = 4
GROUP_WIDTH = GROUP_HEADS * HEAD_DIM
ROPE_THETA = 10000.0
NORM_EPS = 1e-6
DIFF_QK_DIM = 64
GRID_W = 64
NA_WIN_ROWS = 8
NA_WIN_COLS = 16
SWA_WINDOW = 128
MLA_LORA = 512
MLA_NOPE = 128
MLA_ROPE = 64
IN_COLS = 5184
N_EXPERTS = 8

LANES = 128
VMEM_LIMIT = 56 * 1024 * 1024
MASKED = -1e30
LOG2E = math.log2(math.e)

IN_TN = 512
IN_COLS_PAD = -(-IN_COLS // LANES) * LANES
NA_GROUP_ROWS = 4
NA_KEY_ROWS = NA_GROUP_ROWS + NA_WIN_ROWS
NA_UNROLL = 4
SWA_TQ = 1024
SWA_SUB = 256
SWA_SPAN = SWA_SUB + 2 * SWA_WINDOW
MOE_TM = 512


def _tile(n, pref):
    t = min(n, pref)
    assert n % t == 0, (n, t)
    return t


def _cparams(*sem):
    return pltpu.CompilerParams(dimension_semantics=sem, vmem_limit_bytes=VMEM_LIMIT)


def _rms(x, g):
    return x * lax.rsqrt(jnp.mean(x * x, axis=-1, keepdims=True) + NORM_EPS) * g


def _slab_shape(rows, width):
    return jax.ShapeDtypeStruct((rows * (width // LANES), LANES), F32)


def _slab_spec(tm, width, index_map):
    return pl.BlockSpec((tm * (width // LANES), LANES), index_map)


def _slab_chunk(ref, c, tm, width):
    return ref[pl.ds(c, tm, stride=width // LANES), :]


def _slab_store(ref, x):
    tm, width = x.shape
    for c in range(width // LANES):
        ref[pl.ds(c, tm, stride=width // LANES), :] = x[:, c * LANES:(c + 1) * LANES]


def _rope_tables(seq):
    def base(dim):
        inv = ROPE_THETA ** (-jnp.arange(0, dim, 2, dtype=F32) / dim)
        ang = jnp.arange(seq, dtype=F32)[:, None] * inv[None, :]
        ang = jnp.concatenate([ang, ang], axis=-1)
        return jnp.cos(ang), jnp.sin(ang)

    lane = np.arange(LANES)
    cos, sin = base(DIFF_QK_DIM)
    c64 = jnp.concatenate([cos, cos], axis=-1)
    s64 = jnp.concatenate([sin, sin], axis=-1)
    lo = jnp.asarray((lane % 64) < 32)
    sa64 = jnp.where(lo, -s64, 0.0)
    sb64 = jnp.where(lo, 0.0, s64)
    first = jnp.asarray(lane < 64)
    c64p, sa64p, sb64p = (jnp.where(first, t, 0.0) for t in (c64, sa64, sb64))
    cos, sin = base(HEAD_DIM)
    s128 = jnp.where(jnp.asarray(lane < 64), -sin, sin)
    return dict(c64=c64, sa64=sa64, sb64=sb64, c64p=c64p, sa64p=sa64p, sb64p=sb64p,
                c128=cos, s128=s128)


def _rope64(x, c, sa, sb):
    return x * c + pltpu.roll(x, 96, 1) * sa + pltpu.roll(x, 32, 1) * sb


def _rope128(x, c, s):
    return x * c + pltpu.roll(x, 64, 1) * s


def _inproj_kernel(h_ref, g_ref, w_ref, c64, sa64, sb64, c128, s128, o_ref):
    x = _rms(h_ref[...], g_ref[...]).astype(BF16)
    for j in range(-(-IN_COLS_PAD // IN_TN)):
        width = min(IN_TN, IN_COLS_PAD - j * IN_TN)
        acc = jnp.dot(x, w_ref[:, j * IN_TN:j * IN_TN + width], preferred_element_type=F32)
        for g in range(width // LANES):
            a = acc[:, g * LANES:(g + 1) * LANES]
            if j == 0:
                a = _rope64(a, c64[...], sa64[...], sb64[...]) * (DIFF_QK_DIM ** -0.5 * LOG2E)
            elif j == 1:
                a = _rope64(a, c64[...], sa64[...], sb64[...])
            elif j == 3:
                a = a * HEAD_DIM ** -0.5
            elif j == 6:
                a = _rope128(a, c128[...], s128[...]) * HEAD_DIM ** -0.5
            elif j == 7 and g < 2:
                a = _rope128(a, c128[...], s128[...])
            o_ref[:, j * IN_TN + g * LANES:j * IN_TN + (g + 1) * LANES] = a.astype(o_ref.dtype)


def _inproj(h, g, w, tabs, seq, tm=512):
    n, d = h.shape
    tm = _tile(seq, tm)
    per_seq = seq // tm
    tab_spec = pl.BlockSpec((tm, LANES), lambda i: (i % per_seq, 0))
    g = g.reshape(1, d)
    return pl.pallas_call(
        _inproj_kernel,
        grid=(n // tm,),
        in_specs=[pl.BlockSpec((tm, d), lambda i: (i, 0)),
                  pl.BlockSpec(g.shape, lambda i: (0, 0)),
                  pl.BlockSpec(w.shape, lambda i: (0, 0), pipeline_mode=pl.Buffered(1))] + [tab_spec] * 5,
        out_specs=pl.BlockSpec((tm, IN_COLS_PAD), lambda i: (i, 0)),
        out_shape=jax.ShapeDtypeStruct((n, IN_COLS_PAD), BF16),
        compiler_params=_cparams("parallel"),
        name="inproj",
    )(h, g, w, tabs["c64"], tabs["sa64"], tabs["sb64"], tabs["c128"], tabs["s128"])


def _mla_up_kernel(cq_ref, ckv_ref, kr_ref, gq_ref, gkv_ref, wq_ref, wkv_ref, c_ref, sa_ref, sb_ref,
                   q_ref, k_ref, v_ref):
    scale = (MLA_NOPE + MLA_ROPE) ** -0.5 * LOG2E
    c, sa, sb = c_ref[...], sa_ref[...], sb_ref[...]
    cq = _rms(cq_ref[...].astype(F32), gq_ref[...]).astype(BF16)
    qa = jnp.dot(cq, wq_ref[...], preferred_element_type=F32)
    for h in range(GROUP_HEADS):
        base = h * 2 * LANES
        q_ref[:, base:base + LANES] = (qa[:, base:base + LANES] * scale).astype(q_ref.dtype)
        q_ref[:, base + LANES:base + 2 * LANES] = (
            _rope64(qa[:, base + LANES:base + 2 * LANES], c, sa, sb) * scale).astype(q_ref.dtype)
    ckv = _rms(ckv_ref[...].astype(F32), gkv_ref[...]).astype(BF16)
    kva = jnp.dot(ckv, wkv_ref[...], preferred_element_type=F32)
    kr = _rope64(kr_ref[...].astype(F32), c, sa, sb).astype(k_ref.dtype)
    for h in range(GROUP_HEADS):
        base = h * 2 * LANES
        k_ref[:, base:base + LANES] = kva[:, h * LANES:(h + 1) * LANES].astype(k_ref.dtype)
        k_ref[:, base + LANES:base + 2 * LANES] = kr
    v_ref[...] = kva[:, GROUP_WIDTH:].astype(v_ref.dtype)


def _mla_up(z, gq, gkv, wq, wkv, tabs, seq, tm=512):
    n = z.shape[0]
    tm = _tile(seq, tm)
    per_seq = seq // tm
    tab_spec = pl.BlockSpec((tm, LANES), lambda i: (i % per_seq, 0))
    full = lambda a: pl.BlockSpec(a.shape, lambda i: (0,) * a.ndim)
    gq = gq.reshape(1, -1)
    gkv = gkv.reshape(1, -1)
    return pl.pallas_call(
        _mla_up_kernel,
        grid=(n // tm,),
        in_specs=[pl.BlockSpec((tm, MLA_LORA), lambda i: (i, 8)),
                  pl.BlockSpec((tm, MLA_LORA), lambda i: (i, 9)),
                  pl.BlockSpec((tm, LANES), lambda i: (i, 40)),
                  full(gq), full(gkv), full(wq), full(wkv), tab_spec, tab_spec, tab_spec],
        out_specs=[pl.BlockSpec((tm, 2 * GROUP_WIDTH), lambda i: (i, 0)),
                   pl.BlockSpec((tm, 2 * GROUP_WIDTH), lambda i: (i, 0)),
                   pl.BlockSpec((tm, GROUP_WIDTH), lambda i: (i, 0))],
        out_shape=[jax.ShapeDtypeStruct((n, 2 * GROUP_WIDTH), BF16),
                   jax.ShapeDtypeStruct((n, 2 * GROUP_WIDTH), BF16),
                   jax.ShapeDtypeStruct((n, GROUP_WIDTH), BF16)],
        compiler_params=_cparams("parallel"),
        name="mla_up",
    )(z, z, z, gq, gkv, wq, wkv, tabs["c64p"], tabs["sa64p"], tabs["sb64p"])


def _qk(q, k):
    return lax.dot_general(q, k, (((1,), (1,)), ((), ())), preferred_element_type=F32)


def _flash_core(q_ref, k_ref, v_ref, vt_ref, s_ref, m_ref, l_ref, acc_ref, tk):
    nt = k_ref.shape[0] // tk
    assert nt % 2 == 0

    @pl.when(pl.program_id(2) == 0)
    def _():
        vt_ref[...] = v_ref[...].T

    m_ref[...] = jnp.full(m_ref.shape, MASKED, F32)
    l_ref[...] = jnp.zeros(l_ref.shape, F32)
    acc_ref[...] = jnp.zeros(acc_ref.shape, F32)

    def scores(t, slot):
        off = pl.multiple_of(jnp.minimum(t, nt - 1) * tk, tk)
        s_ref[slot] = _qk(k_ref[pl.ds(off, tk), :], q_ref[...])

    def consume(t, slot):
        off = pl.multiple_of(t * tk, tk)
        s = s_ref[slot]
        m_prev = m_ref[...]
        m_new = jnp.maximum(m_prev, jnp.max(s, axis=0, keepdims=True))
        alpha = jnp.exp2(m_prev - m_new)
        p = jnp.exp2(s - m_new)
        l_ref[...] = alpha * l_ref[...] + jnp.sum(p, axis=0, keepdims=True)
        acc_ref[...] = alpha * acc_ref[...] + jnp.dot(vt_ref[:, pl.ds(off, tk)], p.astype(vt_ref.dtype),
                                                      preferred_element_type=F32)
        m_ref[...] = m_new

    scores(0, 0)

    def body(j, carry):
        scores(2 * j + 1, 1)
        consume(2 * j, 0)
        scores(2 * j + 2, 0)
        consume(2 * j + 1, 1)
        return carry

    lax.fori_loop(0, nt // 2, body, 0, unroll=2)


def _flash_scratch(seq, tk, queries):
    return [pltpu.VMEM((LANES, seq), BF16), pltpu.VMEM((2, tk, queries), F32),
            pltpu.VMEM((1, queries), F32), pltpu.VMEM((1, queries), F32), pltpu.VMEM((LANES, queries), F32)]


def _flash_kernel(q_ref, k_ref, v_ref, o_ref, vt_ref, s_ref, m_ref, l_ref, acc_ref, *, tk):
    _flash_core(q_ref, k_ref, v_ref, vt_ref, s_ref, m_ref, l_ref, acc_ref, tk)
    o_ref[...] = (acc_ref[...] / l_ref[...]).T.astype(o_ref.dtype)


def _mla_attention(q, k, v, batch, seq, tq=1024, tk=512):
    n = q.shape[0]
    tq, tk = _tile(seq, tq), _tile(seq, tk)
    nq = seq // tq
    dq = 2 * LANES
    return pl.pallas_call(
        functools.partial(_flash_kernel, tk=tk),
        grid=(batch, GROUP_HEADS, nq),
        in_specs=[pl.BlockSpec((tq, dq), lambda b, h, i: (b * nq + i, h)),
                  pl.BlockSpec((seq, dq), lambda b, h, i: (b, h)),
                  pl.BlockSpec((seq, LANES), lambda b, h, i: (b, h))],
        out_specs=pl.BlockSpec((tq, LANES), lambda b, h, i: (b * nq + i, h)),
        out_shape=jax.ShapeDtypeStruct((n, GROUP_WIDTH), BF16),
        scratch_shapes=_flash_scratch(seq, tk, tq),
        compiler_params=_cparams("parallel", "parallel", "arbitrary"),
        name="mla_attention",
    )(q, k, v)


def _diff_kernel(lq1_ref, lk1_ref, lq2_ref, lk2_ref, g_ref, q_ref, k_ref, v_ref, o_ref,
                 qq_ref, vt_ref, s_ref, m_ref, l_ref, acc_ref, *, tk, lambda_init):
    tq = q_ref.shape[0]
    q = q_ref[...]
    lane = lax.broadcasted_iota(jnp.int32, q.shape, 1)
    qq_ref[:tq, :] = jnp.where(lane < DIFF_QK_DIM, q, jnp.zeros_like(q))
    qq_ref[tq:, :] = jnp.where(lane >= DIFF_QK_DIM, q, jnp.zeros_like(q))
    _flash_core(qq_ref, k_ref, v_ref, vt_ref, s_ref, m_ref, l_ref, acc_ref, tk)
    lam = (jnp.exp(jnp.sum(lq1_ref[...] * lk1_ref[...], axis=-1, keepdims=True))
           - jnp.exp(jnp.sum(lq2_ref[...] * lk2_ref[...], axis=-1, keepdims=True)) + lambda_init)
    o = (acc_ref[:, :tq] / l_ref[:, :tq] - lam * (acc_ref[:, tq:] / l_ref[:, tq:])).T
    o_ref[...] = (_rms(o, g_ref[...]) * (1.0 - lambda_init)).astype(o_ref.dtype)


def _diff_attention(z, lq1, lk1, lq2, lk2, subln, lambda_init, batch, seq, tq=512, tk=512):
    n = z.shape[0]
    tq, tk = _tile(seq, tq), _tile(seq, tk)
    nq = seq // tq
    vec = lambda a: a.reshape(1, -1)
    small = lambda a: pl.BlockSpec(a.shape, lambda b, h, i: (0, 0))
    params = [vec(a) for a in (lq1, lk1, lq2, lk2, subln)]
    return pl.pallas_call(
        functools.partial(_diff_kernel, tk=tk, lambda_init=lambda_init),
        grid=(batch, GROUP_HEADS, nq),
        in_specs=[small(a) for a in params] + [
            pl.BlockSpec((tq, LANES), lambda b, h, i: (b * nq + i, h)),
            pl.BlockSpec((seq, LANES), lambda b, h, i: (b, 4 + h)),
            pl.BlockSpec((seq, LANES), lambda b, h, i: (b, 8 + h))],
        out_specs=pl.BlockSpec((tq, LANES), lambda b, h, i: (b * nq + i, h)),
        out_shape=jax.ShapeDtypeStruct((n, GROUP_WIDTH), BF16),
        scratch_shapes=[pltpu.VMEM((2 * tq, LANES), BF16)] + _flash_scratch(seq, tk, 2 * tq),
        compiler_params=_cparams("parallel", "parallel", "arbitrary"),
        name="diff_attention",
    )(*params, z, z, z)


def _na_bias_tables(rpb, rows):
    groups = rows // NA_GROUP_ROWS
    assert groups >= 3
    qc = np.arange(GRID_W)
    kc = np.arange(GRID_W)
    ws = np.clip(qc - NA_WIN_COLS // 2, 0, GRID_W - NA_WIN_COLS)
    col_ok = (kc[None, :] >= ws[:, None]) & (kc[None, :] < ws[:, None] + NA_WIN_COLS)
    edge = GRID_W - NA_WIN_COLS
    rp = jnp.pad(rpb.astype(F32), ((0, 0), (0, 0), (edge, edge)))
    toep = jnp.stack([rp[:, :, GRID_W - 1 - c:2 * GRID_W - 1 - c] for c in range(GRID_W)], axis=2)
    toep = jnp.where(jnp.asarray(col_ok)[None, None], toep, MASKED)
    masked = jnp.full((rpb.shape[0], GRID_W, GRID_W), MASKED, F32)
    tabs = []
    for m in (0, 1, groups - 1):
        kstart = np.clip(NA_GROUP_ROWS * m - NA_WIN_ROWS // 2, 0, rows - NA_KEY_ROWS)
        qr = NA_GROUP_ROWS * m + np.arange(NA_GROUP_ROWS)
        kr = kstart + np.arange(NA_KEY_ROWS)
        rs = np.clip(qr - NA_WIN_ROWS // 2, 0, rows - NA_WIN_ROWS)
        row_ok = (kr[None, :] >= rs[:, None]) & (kr[None, :] < rs[:, None] + NA_WIN_ROWS)
        ridx = kr[None, :] - qr[:, None] + NA_WIN_ROWS - 1
        tabs.append(jnp.concatenate([
            jnp.concatenate([toep[:, ridx[i, a]] if row_ok[i, a] else masked
                             for a in range(NA_KEY_ROWS)], axis=-1)
            for i in range(NA_GROUP_ROWS)], axis=1))
    return jnp.stack(tabs)


def _na_kernel(bias_ref, q_ref, k_ref, v_ref, o_ref, *, rows):
    groups = rows // NA_GROUP_ROWS
    tq = NA_GROUP_ROWS * GRID_W
    span = NA_KEY_ROWS * GRID_W

    def body(m, carry):
        cls = jnp.where(m == 0, 0, jnp.where(m == groups - 1, 2, 1))
        krow = jnp.clip(NA_GROUP_ROWS * m - NA_WIN_ROWS // 2, 0, rows - NA_KEY_ROWS)
        koff = pl.multiple_of(krow * GRID_W, GRID_W)
        qoff = pl.multiple_of(m * tq, tq)
        s = _qk(q_ref[pl.ds(qoff, tq), :], k_ref[pl.ds(koff, span), :]) + bias_ref[cls, 0]
        p = jnp.exp(s - jnp.max(s, axis=-1, keepdims=True))
        l = jnp.sum(p, axis=-1, keepdims=True)
        v = v_ref[pl.ds(koff, span), :]
        o = jnp.dot(p.astype(v.dtype), v, preferred_element_type=F32) / l
        o_ref[pl.ds(qoff, tq), :] = o.astype(o_ref.dtype)
        return carry

    lax.fori_loop(0, groups, body, 0, unroll=NA_UNROLL)


def _na_attention(z, rpb, batch, seq):
    n = z.shape[0]
    rows = seq // GRID_W
    bias = _na_bias_tables(rpb, rows)
    tq, span = NA_GROUP_ROWS * GRID_W, NA_KEY_ROWS * GRID_W
    return pl.pallas_call(
        functools.partial(_na_kernel, rows=rows),
        grid=(batch, GROUP_HEADS),
        in_specs=[pl.BlockSpec((3, 1, tq, span), lambda b, h: (0, h, 0, 0)),
                  pl.BlockSpec((seq, LANES), lambda b, h: (b, 12 + h)),
                  pl.BlockSpec((seq, LANES), lambda b, h: (b, 16 + h)),
                  pl.BlockSpec((seq, LANES), lambda b, h: (b, 20 + h))],
        out_specs=pl.BlockSpec((seq, LANES), lambda b, h: (b, h)),
        out_shape=jax.ShapeDtypeStruct((n, GROUP_WIDTH), BF16),
        compiler_params=_cparams("parallel", "parallel"),
        name="na_attention",
    )(bias, z, z, z)


def _swa_kernel(sink_ref, q_ref, k_ref, v_ref, o_ref, *, seq):
    sink = sink_ref[pl.program_id(1)]
    for c in range(SWA_TQ // SWA_SUB):
        rows = slice(c * SWA_SUB, (c + 1) * SWA_SUB)
        q0 = pl.program_id(2) * SWA_TQ + c * SWA_SUB
        koff = pl.multiple_of(jnp.clip(q0 - SWA_WINDOW, 0, seq - SWA_SPAN), SWA_WINDOW)
        s = _qk(q_ref[rows, :], k_ref[pl.ds(koff, SWA_SPAN), :])
        qpos = q0 + lax.broadcasted_iota(jnp.int32, s.shape, 0)
        kpos = koff + lax.broadcasted_iota(jnp.int32, s.shape, 1)
        s = jnp.where(jnp.abs(qpos - kpos) <= SWA_WINDOW, s, MASKED)
        m = jnp.maximum(jnp.max(s, axis=-1, keepdims=True), sink)
        p = jnp.exp(s - m)
        l = jnp.sum(p, axis=-1, keepdims=True) + jnp.exp(sink - m)
        v = v_ref[pl.ds(koff, SWA_SPAN), :]
        o_ref[rows, :] = (jnp.dot(p.astype(v.dtype), v, preferred_element_type=F32) / l).astype(o_ref.dtype)


def _swa_attention(z, sinks, batch, seq):
    n = z.shape[0]
    assert seq % SWA_TQ == 0 and seq >= SWA_SPAN
    nq = seq // SWA_TQ
    return pl.pallas_call(
        functools.partial(_swa_kernel, seq=seq),
        grid=(batch, GROUP_HEADS, nq),
        in_specs=[pl.BlockSpec(memory_space=pltpu.SMEM),
                  pl.BlockSpec((SWA_TQ, LANES), lambda b, h, i: (b * nq + i, 24 + h)),
                  pl.BlockSpec((seq, LANES), lambda b, h, i: (b, 28 + h // 2)),
                  pl.BlockSpec((seq, LANES), lambda b, h, i: (b, 30 + h // 2))],
        out_specs=pl.BlockSpec((SWA_TQ, LANES), lambda b, h, i: (b * nq + i, h)),
        out_shape=jax.ShapeDtypeStruct((n, GROUP_WIDTH), BF16),
        compiler_params=_cparams("parallel", "parallel", "arbitrary"),
        name="swa_attention",
    )(sinks, z, z, z)


def _dot_f32(x, w):
    xh = x.astype(BF16)
    xl = (x - xh.astype(F32)).astype(BF16)
    wh = w.astype(BF16)
    wl = (w - wh.astype(F32)).astype(BF16)
    d = lambda a, b: jnp.dot(a, b, preferred_element_type=F32)
    return d(xh, wh) + (d(xh, wl) + d(xl, wh))


def _top2(logits):
    lane = lax.broadcasted_iota(jnp.int32, logits.shape, 1)
    l1 = jnp.where(lane < N_EXPERTS, logits, MASKED)
    m1 = jnp.max(l1, axis=-1, keepdims=True)
    i1 = jnp.min(jnp.where(l1 == m1, lane, LANES), axis=-1, keepdims=True)
    l2 = jnp.where(lane == i1, MASKED, l1)
    m2 = jnp.max(l2, axis=-1, keepdims=True)
    i2 = jnp.min(jnp.where(l2 == m2, lane, LANES), axis=-1, keepdims=True)
    e2 = jnp.exp(m2 - m1)
    den = 1.0 + e2
    out = jnp.where(lane == 0, i1.astype(F32), 0.0)
    out = jnp.where(lane == 1, i2.astype(F32), out)
    out = jnp.where(lane == 2, 1.0 / den, out)
    return jnp.where(lane == 3, e2 / den, out)


def _outproj_kernel(*refs, routed):
    if routed:
        md, mn, ms, ml, w_ref, h_ref, g_ref, wr_ref, h1_ref, hn_ref, route_ref = refs
    else:
        md, mn, ms, ml, w_ref, h_ref, g_ref, h1_ref, hn_ref = refs
    acc = h_ref[...]
    for k, m in enumerate((md, mn, ms, ml)):
        acc = acc + jnp.dot(m[...], w_ref[k * GROUP_WIDTH:(k + 1) * GROUP_WIDTH, :],
                            preferred_element_type=F32)
    h1_ref[...] = acc
    hn = _rms(acc, g_ref[...])
    if routed:
        _slab_store(hn_ref, hn)
        route_ref[...] = _top2(_dot_f32(hn, wr_ref[...]))
    else:
        hn_ref[...] = hn.astype(hn_ref.dtype)


def _outproj(mixed, w, h, g, w_router=None, tm=256):
    n, d = h.shape
    tm = _tile(n, tm)
    routed = w_router is not None
    row = lambda width: pl.BlockSpec((tm, width), lambda i: (i, 0))
    full = lambda a: pl.BlockSpec(a.shape, lambda i: (0, 0))
    g = g.reshape(1, d)
    args = list(mixed) + [w, h, g]
    in_specs = [row(GROUP_WIDTH)] * 4 + [full(w), row(d), full(g)]
    if routed:
        args.append(w_router)
        in_specs.append(full(w_router))
        out_specs = [row(d), _slab_spec(tm, d, lambda i: (i, 0)), row(LANES)]
        out_shape = [jax.ShapeDtypeStruct((n, d), F32), _slab_shape(n, d),
                     jax.ShapeDtypeStruct((n, LANES), F32)]
    else:
        out_specs = [row(d), row(d)]
        out_shape = [jax.ShapeDtypeStruct((n, d), F32), jax.ShapeDtypeStruct((n, d), BF16)]
    return pl.pallas_call(
        functools.partial(_outproj_kernel, routed=routed),
        grid=(n // tm,),
        in_specs=in_specs, out_specs=out_specs, out_shape=out_shape,
        compiler_params=_cparams("parallel"),
        name="outproj_routed" if routed else "outproj",
    )(*args)


def _swiglu_step(x, wg_ref, wu_ref, wd_ref, o_ref, f):
    @pl.when(f == 0)
    def _():
        o_ref[...] = jnp.zeros(o_ref.shape, o_ref.dtype)

    bf16 = lambda w_ref: w_ref[...].astype(BF16)
    g = jnp.dot(x, bf16(wg_ref), preferred_element_type=F32)
    u = jnp.dot(x, bf16(wu_ref), preferred_element_type=F32)
    a = (g * jax.nn.sigmoid(g) * u).astype(BF16)
    o_ref[...] += jnp.dot(a, bf16(wd_ref), preferred_element_type=F32)


def _ffn_kernel(x_ref, wg_ref, wu_ref, wd_ref, o_ref):
    _swiglu_step(x_ref[...], wg_ref, wu_ref, wd_ref, o_ref, pl.program_id(1))


def _dense_ffn(x, wg, wu, wd, tm=1024, tf=256):
    n, d = x.shape
    ff = wg.shape[1]
    tm, tf = _tile(n, tm), _tile(ff, tf)
    return pl.pallas_call(
        _ffn_kernel,
        grid=(n // tm, ff // tf),
        in_specs=[pl.BlockSpec((tm, d), lambda i, f: (i, 0)),
                  pl.BlockSpec((d, tf), lambda i, f: (0, f)),
                  pl.BlockSpec((d, tf), lambda i, f: (0, f)),
                  pl.BlockSpec((tf, d), lambda i, f: (f, 0))],
        out_specs=pl.BlockSpec((tm, d), lambda i, f: (i, 0)),
        out_shape=jax.ShapeDtypeStruct((n, d), F32),
        compiler_params=_cparams("parallel", "arbitrary"),
        name="dense_ffn",
    )(x, wg, wu, wd)


def _moe_kernel(blk_exp_ref, n_used_ref, tok_ref, tok_next_ref, hn_ref, wg_ref, wu_ref, wd_ref, o_ref,
                stage_ref, sem, xb_ref, acc_ref):
    b, f = pl.program_id(0), pl.program_id(1)
    tm, d = xb_ref.shape
    chunks = d // LANES
    n_used = n_used_ref[0]
    used = b < n_used
    last = f == pl.num_programs(1) - 1

    def row_copy(tok, r, slot):
        return pltpu.make_async_copy(hn_ref.at[pl.ds(tok[0, 0, r] * chunks, chunks)],
                                     stage_ref.at[slot, pl.ds(r * chunks, chunks)], sem.at[slot])

    def start_block(tok, slot):
        def go(r, carry):
            row_copy(tok, r, slot).start()
            return carry
        lax.fori_loop(0, tm, go, 0, unroll=8)

    def wait_block(slot):
        pltpu.make_async_copy(hn_ref.at[pl.ds(0, tm * chunks)], stage_ref.at[slot], sem.at[slot]).wait()

    @pl.when(used & (b == 0) & (f == 0))
    def _():
        start_block(tok_ref, 0)

    @pl.when(used & (f == 0))
    def _():
        slot = b % 2
        wait_block(slot)
        for c in range(chunks):
            xb_ref[:, c * LANES:(c + 1) * LANES] = _slab_chunk(stage_ref.at[slot], c, tm, d).astype(BF16)

    @pl.when((b + 1 < n_used) & (f == 1))
    def _():
        start_block(tok_next_ref, (b + 1) % 2)

    @pl.when(used)
    def _():
        _swiglu_step(xb_ref[...], wg_ref, wu_ref, wd_ref, acc_ref, f)

    @pl.when(used & last)
    def _():
        _slab_store(o_ref, acc_ref[...])

    @pl.when(jnp.logical_not(used) & last)
    def _():
        o_ref[...] = jnp.zeros(o_ref.shape, o_ref.dtype)


def _moe_ffn(hn, slot_tok, blk_exp, n_used, wg, wu, wd, tf=1024):
    d, ff = wg.shape[1], wg.shape[2]
    n_slots = slot_tok.shape[0]
    tf = _tile(ff, tf)
    nf = ff // tf
    assert nf >= 2
    n_blk = n_slots // MOE_TM
    tok = slot_tok.reshape(n_blk, 1, MOE_TM)
    tok_spec = lambda nxt: pl.BlockSpec((1, 1, MOE_TM), lambda b, f, be, nu: (jnp.minimum(b + nxt, n_blk - 1), 0, 0),
                                        memory_space=pltpu.SMEM)

    def blk(b, nu):
        return jnp.minimum(b, nu[0] - 1)

    def col(b, f, nu):
        return jnp.where(b < nu[0], f, nf - 1)

    grid_spec = pltpu.PrefetchScalarGridSpec(
        num_scalar_prefetch=2,
        grid=(n_blk, nf),
        in_specs=[tok_spec(0), tok_spec(1), pl.BlockSpec(memory_space=pl.ANY),
                  pl.BlockSpec((None, d, tf), lambda b, f, be, nu: (be[blk(b, nu)], 0, col(b, f, nu))),
                  pl.BlockSpec((None, d, tf), lambda b, f, be, nu: (be[blk(b, nu)], 0, col(b, f, nu))),
                  pl.BlockSpec((None, tf, d), lambda b, f, be, nu: (be[blk(b, nu)], col(b, f, nu), 0))],
        out_specs=_slab_spec(MOE_TM, d, lambda b, f, be, nu: (b, 0)),
        scratch_shapes=[pltpu.VMEM((2, MOE_TM * (d // LANES), LANES), F32), pltpu.SemaphoreType.DMA((2,)),
                        pltpu.VMEM((MOE_TM, d), BF16), pltpu.VMEM((MOE_TM, d), F32)],
    )
    return pl.pallas_call(
        _moe_kernel,
        grid_spec=grid_spec,
        out_shape=_slab_shape(n_slots, d),
        compiler_params=_cparams("arbitrary", "arbitrary"),
        name="moe_ffn",
    )(blk_exp, n_used, tok, tok, hn, wg, wu, wd)


def _ple_kernel(*refs, routed, last):
    if routed:
        h1_ref, pos_ref, pos_next_ref, ys_ref, route_ref, p_ref, gp_ref, wg_ref, wp_ref, gn_ref = refs[:10]
        outs, (stage_ref, sem, mix_ref) = refs[10:-3], refs[-3:]
        tm, d = h1_ref.shape
        chunks = d // LANES
        i = pl.program_id(0)

        def row_copy(pos, r, slot):
            return pltpu.make_async_copy(ys_ref.at[pl.ds(pos[0, 0, r] * chunks, chunks)],
                                         stage_ref.at[slot, pl.ds(r * chunks, chunks)], sem.at[slot])

        def start_rows(pos, slot):
            def go(r, carry):
                row_copy(pos, 2 * r, slot).start(priority=0)
                row_copy(pos, 2 * r + 1, slot).start(priority=1)
                return carry
            lax.fori_loop(0, tm, go, 0, unroll=4)

        @pl.when(i == 0)
        def _():
            start_rows(pos_ref, 0)

        @pl.when(i + 1 < pl.num_programs(0))
        def _():
            start_rows(pos_next_ref, (i + 1) % 2)

        slot = i % 2

        pltpu.make_async_copy(ys_ref.at[pl.ds(0, 2 * tm * chunks)], stage_ref.at[slot], sem.at[slot]).wait()

        route = route_ref[...]
        g0, g1 = route[:, 2:3], route[:, 3:4]
        y_ref = stage_ref.at[slot]
        for c in range(chunks):
            mix_ref[:, c * LANES:(c + 1) * LANES] = (
                g0 * y_ref[pl.ds(c, tm, stride=chunks), :]
                + g1 * y_ref[pl.ds(tm * chunks + c, tm, stride=chunks), :])
        h2 = h1_ref[...] + mix_ref[...]
    else:
        h1_ref, y_ref, p_ref, gp_ref, wg_ref, wp_ref, gn_ref = refs[:7]
        outs = refs[7:]
        h2 = h1_ref[...] + y_ref[...]
    hp = _rms(h2, gp_ref[...]).astype(BF16)
    gate = jax.nn.sigmoid(jnp.dot(hp, wg_ref[...], preferred_element_type=F32))
    pp = jnp.dot(p_ref[...].astype(BF16), wp_ref[...], preferred_element_type=F32)
    h3 = h2 + gate * pp
    outs[0][...] = _rms(h3, gn_ref[...]) if last else h3


def _ple(h1, y, routing, p, g_ple, w_gate, w_proj, g_final, last, tm=256):
    n, d = h1.shape
    tm = _tile(n, tm)
    nt = n // tm
    routed = routing is not None
    row = lambda width: pl.BlockSpec((tm, width), lambda i: (i, 0))
    full = lambda a: pl.BlockSpec(a.shape, lambda i: (0, 0))
    g_ple, g_final = g_ple.reshape(1, d), g_final.reshape(1, d)
    if routed:
        route, pos = routing
        pos = pos.reshape(nt, tm, 2).transpose(0, 2, 1).reshape(nt, 1, 2 * tm)
        pos_spec = lambda nxt: pl.BlockSpec((1, 1, 2 * tm), lambda i: (jnp.minimum(i + nxt, nt - 1), 0, 0),
                                            memory_space=pltpu.SMEM)
        args = [h1, pos, pos, y, route]
        in_specs = [row(d), pos_spec(0), pos_spec(1), pl.BlockSpec(memory_space=pl.ANY), row(LANES)]
        scratch = [pltpu.VMEM((2, 2 * tm * (d // LANES), LANES), F32), pltpu.SemaphoreType.DMA((2,)),
                   pltpu.VMEM((tm, d), F32)]
    else:
        args = [h1, y]
        in_specs = [row(d), row(d)]
        scratch = []
    args += [p, g_ple, w_gate, w_proj, g_final]
    in_specs += [row(p.shape[1]), full(g_ple), full(w_gate), full(w_proj), full(g_final)]
    return pl.pallas_call(
        functools.partial(_ple_kernel, routed=routed, last=last),
        grid=(nt,),
        in_specs=in_specs, out_specs=[row(d)], out_shape=[jax.ShapeDtypeStruct((n, d), F32)],
        scratch_shapes=scratch,
        compiler_params=_cparams("arbitrary"),
        name="ple_routed" if routed else "ple",
    )(*args)


def _dispatch(route, n):
    idx = route[:, :2].astype(jnp.int32)
    flat_e = idx.reshape(-1)
    onehot = (flat_e[:, None] == jnp.arange(N_EXPERTS, dtype=jnp.int32)[None, :]).astype(jnp.int32)
    csum = jnp.cumsum(onehot, axis=0)
    rank = jnp.sum(onehot * csum, axis=1) - 1
    counts = csum[-1]
    padded = (counts + MOE_TM - 1) // MOE_TM * MOE_TM
    pad_end = jnp.cumsum(padded)
    pad_start = pad_end - padded
    dest = pad_start[flat_e] + rank
    n_blk = (2 * n) // MOE_TM + N_EXPERTS
    flat_tok = jnp.arange(2 * n, dtype=jnp.int32) // 2
    slot_tok = jnp.zeros((n_blk * MOE_TM,), jnp.int32).at[dest].set(flat_tok)
    blk_exp = jnp.minimum(jnp.searchsorted(pad_end, jnp.arange(n_blk, dtype=jnp.int32) * MOE_TM, side="right"),
                          N_EXPERTS - 1).astype(jnp.int32)
    n_used = (pad_end[-1] // MOE_TM).astype(jnp.int32).reshape(1)
    return slot_tok, blk_exp, n_used, dest.reshape(n, 2).astype(jnp.int32)


def _mla_weights(w_uq, w_ukv):
    lora = w_uq.shape[0]
    wq = w_uq.reshape(lora, GROUP_HEADS, MLA_NOPE + MLA_ROPE)
    wq = jnp.pad(wq, ((0, 0), (0, 0), (0, 2 * LANES - MLA_NOPE - MLA_ROPE)))
    wkv = w_ukv.reshape(lora, GROUP_HEADS, 2, HEAD_DIM).transpose(0, 2, 1, 3)
    return wq.reshape(lora, -1).astype(BF16), wkv.reshape(lora, -1).astype(BF16)


def kernel(x, p, attn_norm, w_in, diff_lq1, diff_lk1, diff_lq2, diff_lk2, diff_subln, na_rpb, swa_sinks,
           mla_q_norm, mla_kv_norm, mla_w_uq, mla_w_ukv, w_out, ffn_norm, dense_w_gate, dense_w_up,
           dense_w_down, moe_router, moe_w_gate, moe_w_up, moe_w_down, ple_norm, ple_gate, ple_proj,
           final_norm):
    batch, seq, d = x.shape
    n = batch * seq
    depth = w_in.shape[0]
    tabs = _rope_tables(seq)
    h = x.reshape(n, d)
    for i in range(depth):
        lambda_init = 0.8 - 0.6 * math.exp(-0.3 * i)
        w_in_i = jnp.pad(w_in[i].astype(BF16), ((0, 0), (0, IN_COLS_PAD - IN_COLS)))
        z = _inproj(h, attn_norm[i], w_in_i, tabs, seq)
        wq, wkv = _mla_weights(mla_w_uq[i], mla_w_ukv[i])
        q_l, k_l, v_l = _mla_up(z, mla_q_norm[i], mla_kv_norm[i], wq, wkv, tabs, seq)
        mixed = (
            _diff_attention(z, diff_lq1[i], diff_lk1[i], diff_lq2[i], diff_lk2[i], diff_subln[i],
                            lambda_init, batch, seq),
            _na_attention(z, na_rpb[i], batch, seq),
            _swa_attention(z, swa_sinks[i], batch, seq),
            _mla_attention(q_l, k_l, v_l, batch, seq),
        )
        j = i // 2
        routed = i % 2 == 1
        last = i == depth - 1
        if not routed:
            h1, hn = _outproj(mixed, w_out[i].astype(BF16), h, ffn_norm[i])
            y = _dense_ffn(hn, dense_w_gate[j], dense_w_up[j], dense_w_down[j])
            routing = None
        else:
            w_router = jnp.pad(moe_router[j], ((0, 0), (0, LANES - N_EXPERTS)))
            h1, hn, route = _outproj(mixed, w_out[i].astype(BF16), h, ffn_norm[i], w_router)
            slot_tok, blk_exp, n_used, pos = _dispatch(route, n)
            y = _moe_ffn(hn, slot_tok, blk_exp, n_used, moe_w_gate[j].astype(BF16),
                         moe_w_up[j].astype(BF16), moe_w_down[j].astype(BF16))
            routing = (route, pos)
        (h,) = _ple(h1, y, routing, p[i].reshape(n, -1), ple_norm[i], ple_gate[i].astype(BF16),
                    ple_proj[i].astype(BF16), final_norm, last)
    return h.reshape(batch, seq, d)
```

```python
import functools
import math

import numpy as np
import jax
import jax.numpy as jnp
from jax import lax
from jax.experimental import pallas as pl
from jax.experimental.pallas import tpu as pltpu

F32 = jnp.float32
BF16 = jnp.bfloat16

HEAD_DIM = 128
GROUP_HEADS = 4
GROUP_WIDTH = GROUP_HEADS * HEAD_DIM
ROPE_THETA = 10000.0
NORM_EPS = 1e-6
DIFF_QK_DIM = 64
GRID_W = 64
NA_WIN_ROWS = 8
NA_WIN_COLS = 16
SWA_WINDOW = 128
MLA_LORA = 512
MLA_NOPE = 128
MLA_ROPE = 64
IN_COLS = 5184
N_EXPERTS = 8

LANES = 128
VMEM_LIMIT = 56 * 1024 * 1024
MASKED = -1e30
LOG2E = math.log2(math.e)

IN_TN = 512
IN_COLS_PAD = -(-IN_COLS // LANES) * LANES
NA_GROUP_ROWS = 4
NA_KEY_ROWS = NA_GROUP_ROWS + NA_WIN_ROWS
NA_UNROLL = 4
SWA_TQ = 1024
SWA_SUB = 256
SWA_SPAN = SWA_SUB + 2 * SWA_WINDOW
MOE_TM = 512


def _tile(n, pref):
    t = min(n, pref)
    assert n % t == 0, (n, t)
    return t


def _cparams(*sem):
    return pltpu.CompilerParams(dimension_semantics=sem, vmem_limit_bytes=VMEM_LIMIT)


def _rms(x, g):
    return x * lax.rsqrt(jnp.mean(x * x, axis=-1, keepdims=True) + NORM_EPS) * g


def _slab_shape(rows, width):
    return jax.ShapeDtypeStruct((rows * (width // LANES), LANES), F32)


def _slab_spec(tm, width, index_map):
    return pl.BlockSpec((tm * (width // LANES), LANES), index_map)


def _slab_chunk(ref, c, tm, width):
    return ref[pl.ds(c, tm, stride=width // LANES), :]


def _slab_store(ref, x):
    tm, width = x.shape
    for c in range(width // LANES):
        ref[pl.ds(c, tm, stride=width // LANES), :] = x[:, c * LANES:(c + 1) * LANES]


def _rope_tables(seq):
    def base(dim):
        inv = ROPE_THETA ** (-jnp.arange(0, dim, 2, dtype=F32) / dim)
        ang = jnp.arange(seq, dtype=F32)[:, None] * inv[None, :]
        ang = jnp.concatenate([ang, ang], axis=-1)
        return jnp.cos(ang), jnp.sin(ang)

    lane = np.arange(LANES)
    cos, sin = base(DIFF_QK_DIM)
    c64 = jnp.concatenate([cos, cos], axis=-1)
    s64 = jnp.concatenate([sin, sin], axis=-1)
    lo = jnp.asarray((lane % 64) < 32)
    sa64 = jnp.where(lo, -s64, 0.0)
    sb64 = jnp.where(lo, 0.0, s64)
    first = jnp.asarray(lane < 64)
    c64p, sa64p, sb64p = (jnp.where(first, t, 0.0) for t in (c64, sa64, sb64))
    cos, sin = base(HEAD_DIM)
    s128 = jnp.where(jnp.asarray(lane < 64), -sin, sin)
    return dict(c64=c64, sa64=sa64, sb64=sb64, c64p=c64p, sa64p=sa64p, sb64p=sb64p,
                c128=cos, s128=s128)


def _rope64(x, c, sa, sb):
    return x * c + pltpu.roll(x, 96, 1) * sa + pltpu.roll(x, 32, 1) * sb


def _rope128(x, c, s):
    return x * c + pltpu.roll(x, 64, 1) * s


def _inproj_kernel(h_ref, g_ref, w_ref, c64, sa64, sb64, c128, s128, o_ref):
    x = _rms(h_ref[...], g_ref[...]).astype(BF16)
    for j in range(-(-IN_COLS_PAD // IN_TN)):
        width = min(IN_TN, IN_COLS_PAD - j * IN_TN)
        acc = jnp.dot(x, w_ref[:, j * IN_TN:j * IN_TN + width], preferred_element_type=F32)
        for g in range(width // LANES):
            a = acc[:, g * LANES:(g + 1) * LANES]
            if j == 0:
                a = _rope64(a, c64[...], sa64[...], sb64[...]) * (DIFF_QK_DIM ** -0.5 * LOG2E)
            elif j == 1:
                a = _rope64(a, c64[...], sa64[...], sb64[...])
            elif j == 3:
                a = a * HEAD_DIM ** -0.5
            elif j == 6:
                a = _rope128(a, c128[...], s128[...]) * HEAD_DIM ** -0.5
            elif j == 7 and g < 2:
                a = _rope128(a, c128[...], s128[...])
            o_ref[:, j * IN_TN + g * LANES:j * IN_TN + (g + 1) * LANES] = a.astype(o_ref.dtype)


def _inproj(h, g, w, tabs, seq, tm=512):
    n, d = h.shape
    tm = _tile(seq, tm)
    per_seq = seq // tm
    tab_spec = pl.BlockSpec((tm, LANES), lambda i: (i % per_seq, 0))
    g = g.reshape(1, d)
    return pl.pallas_call(
        _inproj_kernel,
        grid=(n // tm,),
        in_specs=[pl.BlockSpec((tm, d), lambda i: (i, 0)),
                  pl.BlockSpec(g.shape, lambda i: (0, 0)),
                  pl.BlockSpec(w.shape, lambda i: (0, 0), pipeline_mode=pl.Buffered(1))] + [tab_spec] * 5,
        out_specs=pl.BlockSpec((tm, IN_COLS_PAD), lambda i: (i, 0)),
        out_shape=jax.ShapeDtypeStruct((n, IN_COLS_PAD), BF16),
        compiler_params=_cparams("parallel"),
        name="inproj",
    )(h, g, w, tabs["c64"], tabs["sa64"], tabs["sb64"], tabs["c128"], tabs["s128"])


def _mla_up_kernel(cq_ref, ckv_ref, kr_ref, gq_ref, gkv_ref, wq_ref, wkv_ref, c_ref, sa_ref, sb_ref,
                   q_ref, k_ref, v_ref):
    scale = (MLA_NOPE + MLA_ROPE) ** -0.5 * LOG2E
    c, sa, sb = c_ref[...], sa_ref[...], sb_ref[...]
    cq = _rms(cq_ref[...].astype(F32), gq_ref[...]).astype(BF16)
    qa = jnp.dot(cq, wq_ref[...], preferred_element_type=F32)
    for h in range(GROUP_HEADS):
        base = h * 2 * LANES
        q_ref[:, base:base + LANES] = (qa[:, base:base + LANES] * scale).astype(q_ref.dtype)
        q_ref[:, base + LANES:base + 2 * LANES] = (
            _rope64(qa[:, base + LANES:base + 2 * LANES], c, sa, sb) * scale).astype(q_ref.dtype)
    ckv = _rms(ckv_ref[...].astype(F32), gkv_ref[...]).astype(BF16)
    kva = jnp.dot(ckv, wkv_ref[...], preferred_element_type=F32)
    kr = _rope64(kr_ref[...].astype(F32), c, sa, sb).astype(k_ref.dtype)
    for h in range(GROUP_HEADS):
        base = h * 2 * LANES
        k_ref[:, base:base + LANES] = kva[:, h * LANES:(h + 1) * LANES].astype(k_ref.dtype)
        k_ref[:, base + LANES:base + 2 * LANES] = kr
    v_ref[...] = kva[:, GROUP_WIDTH:].astype(v_ref.dtype)


def _mla_up(z, gq, gkv, wq, wkv, tabs, seq, tm=512):
    n = z.shape[0]
    tm = _tile(seq, tm)
    per_seq = seq // tm
    tab_spec = pl.BlockSpec((tm, LANES), lambda i: (i % per_seq, 0))
    full = lambda a: pl.BlockSpec(a.shape, lambda i: (0,) * a.ndim)
    gq = gq.reshape(1, -1)
    gkv = gkv.reshape(1, -1)
    return pl.pallas_call(
        _mla_up_kernel,
        grid=(n // tm,),
        in_specs=[pl.BlockSpec((tm, MLA_LORA), lambda i: (i, 8)),
                  pl.BlockSpec((tm, MLA_LORA), lambda i: (i, 9)),
                  pl.BlockSpec((tm, LANES), lambda i: (i, 40)),
                  full(gq), full(gkv), full(wq), full(wkv), tab_spec, tab_spec, tab_spec],
        out_specs=[pl.BlockSpec((tm, 2 * GROUP_WIDTH), lambda i: (i, 0)),
                   pl.BlockSpec((tm, 2 * GROUP_WIDTH), lambda i: (i, 0)),
                   pl.BlockSpec((tm, GROUP_WIDTH), lambda i: (i, 0))],
        out_shape=[jax.ShapeDtypeStruct((n, 2 * GROUP_WIDTH), BF16),
                   jax.ShapeDtypeStruct((n, 2 * GROUP_WIDTH), BF16),
                   jax.ShapeDtypeStruct((n, GROUP_WIDTH), BF16)],
        compiler_params=_cparams("parallel"),
        name="mla_up",
    )(z, z, z, gq, gkv, wq, wkv, tabs["c64p"], tabs["sa64p"], tabs["sb64p"])


def _qk(q, k):
    return lax.dot_general(q, k, (((1,), (1,)), ((), ())), preferred_element_type=F32)


def _flash_core(q_ref, k_ref, v_ref, vt_ref, s_ref, m_ref, l_ref, acc_ref, tk):
    nt = k_ref.shape[0] // tk
    assert nt % 2 == 0

    @pl.when(pl.program_id(2) == 0)
    def _():
        vt_ref[...] = v_ref[...].T

    m_ref[...] = jnp.full(m_ref.shape, MASKED, F32)
    l_ref[...] = jnp.zeros(l_ref.shape, F32)
    acc_ref[...] = jnp.zeros(acc_ref.shape, F32)

    def scores(t, slot):
        off = pl.multiple_of(jnp.minimum(t, nt - 1) * tk, tk)
        s_ref[slot] = _qk(k_ref[pl.ds(off, tk), :], q_ref[...])

    def consume(t, slot):
        off = pl.multiple_of(t * tk, tk)
        s = s_ref[slot]
        m_prev = m_ref[...]
        m_new = jnp.maximum(m_prev, jnp.max(s, axis=0, keepdims=True))
        alpha = jnp.exp2(m_prev - m_new)
        p = jnp.exp2(s - m_new)
        l_ref[...] = alpha * l_ref[...] + jnp.sum(p, axis=0, keepdims=True)
        acc_ref[...] = alpha * acc_ref[...] + jnp.dot(vt_ref[:, pl.ds(off, tk)], p.astype(vt_ref.dtype),
                                                      preferred_element_type=F32)
        m_ref[...] = m_new

    scores(0, 0)

    def body(j, carry):
        scores(2 * j + 1, 1)
        consume(2 * j, 0)
        scores(2 * j + 2, 0)
        consume(2 * j + 1, 1)
        return carry

    lax.fori_loop(0, nt // 2, body, 0, unroll=2)


def _flash_scratch(seq, tk, queries):
    return [pltpu.VMEM((LANES, seq), BF16), pltpu.VMEM((2, tk, queries), F32),
            pltpu.VMEM((1, queries), F32), pltpu.VMEM((1, queries), F32), pltpu.VMEM((LANES, queries), F32)]


def _ride_specs(rides, steps, step_of):
    specs = []
    for ride in rides:
        rows = ride.shape[0] // steps
        assert rows * steps == ride.shape[0] and rows % 16 == 0, (ride.shape, steps)
        specs.append(pl.BlockSpec((rows, ride.shape[1]), lambda *g: (step_of(*g), 0)))
    return specs, [jax.ShapeDtypeStruct(ride.shape, BF16) for ride in rides]


def _ride_cast(w_refs, wb_refs):
    for w_ref, wb_ref in zip(w_refs, wb_refs):
        wb_ref[...] = w_ref[...].astype(wb_ref.dtype)


def _flash_kernel(*refs, tk, n_ride):
    (q_ref, k_ref, v_ref), w_refs = refs[:3], refs[3:3 + n_ride]
    o_ref, wb_refs = refs[3 + n_ride], refs[4 + n_ride:4 + 2 * n_ride]
    vt_ref, s_ref, m_ref, l_ref, acc_ref = refs[4 + 2 * n_ride:]
    _ride_cast(w_refs, wb_refs)
    _flash_core(q_ref, k_ref, v_ref, vt_ref, s_ref, m_ref, l_ref, acc_ref, tk)
    o_ref[...] = (acc_ref[...] / l_ref[...]).T.astype(o_ref.dtype)


def _mla_attention(q, k, v, rides, batch, seq, tq=1024, tk=512):
    n = q.shape[0]
    tq, tk = _tile(seq, tq), _tile(seq, tk)
    nq = seq // tq
    dq = 2 * LANES
    ride_specs, ride_shapes = _ride_specs(rides, batch * GROUP_HEADS * nq,
                                          lambda b, h, i: (b * GROUP_HEADS + h) * nq + i)
    return pl.pallas_call(
        functools.partial(_flash_kernel, tk=tk, n_ride=len(rides)),
        grid=(batch, GROUP_HEADS, nq),
        in_specs=[pl.BlockSpec((tq, dq), lambda b, h, i: (b * nq + i, h)),
                  pl.BlockSpec((seq, dq), lambda b, h, i: (b, h)),
                  pl.BlockSpec((seq, LANES), lambda b, h, i: (b, h))] + ride_specs,
        out_specs=[pl.BlockSpec((tq, LANES), lambda b, h, i: (b * nq + i, h))] + ride_specs,
        out_shape=[jax.ShapeDtypeStruct((n, GROUP_WIDTH), BF16)] + ride_shapes,
        scratch_shapes=_flash_scratch(seq, tk, tq),
        compiler_params=_cparams("parallel", "parallel", "arbitrary"),
        name="mla_attention",
    )(q, k, v, *rides)


def _diff_kernel(*refs, tk, lambda_init, n_ride):
    (lq1_ref, lk1_ref, lq2_ref, lk2_ref, g_ref, q_ref, k_ref, v_ref), w_refs = refs[:8], refs[8:8 + n_ride]
    o_ref, wb_refs = refs[8 + n_ride], refs[9 + n_ride:9 + 2 * n_ride]
    qq_ref, vt_ref, s_ref, m_ref, l_ref, acc_ref = refs[9 + 2 * n_ride:]
    _ride_cast(w_refs, wb_refs)
    tq = q_ref.shape[0]
    q = q_ref[...]
    lane = lax.broadcasted_iota(jnp.int32, q.shape, 1)
    qq_ref[:tq, :] = jnp.where(lane < DIFF_QK_DIM, q, jnp.zeros_like(q))
    qq_ref[tq:, :] = jnp.where(lane >= DIFF_QK_DIM, q, jnp.zeros_like(q))
    _flash_core(qq_ref, k_ref, v_ref, vt_ref, s_ref, m_ref, l_ref, acc_ref, tk)
    lam = (jnp.exp(jnp.sum(lq1_ref[...] * lk1_ref[...], axis=-1, keepdims=True))
           - jnp.exp(jnp.sum(lq2_ref[...] * lk2_ref[...], axis=-1, keepdims=True)) + lambda_init)
    o = (acc_ref[:, :tq] / l_ref[:, :tq] - lam * (acc_ref[:, tq:] / l_ref[:, tq:])).T
    o_ref[...] = (_rms(o, g_ref[...]) * (1.0 - lambda_init)).astype(o_ref.dtype)


def _diff_attention(z, lq1, lk1, lq2, lk2, subln, lambda_init, rides, batch, seq, tq=512, tk=512):
    n = z.shape[0]
    tq, tk = _tile(seq, tq), _tile(seq, tk)
    nq = seq // tq
    vec = lambda a: a.reshape(1, -1)
    small = lambda a: pl.BlockSpec(a.shape, lambda b, h, i: (0, 0))
    params = [vec(a) for a in (lq1, lk1, lq2, lk2, subln)]
    ride_specs, ride_shapes = _ride_specs(rides, batch * GROUP_HEADS * nq,
                                          lambda b, h, i: (b * GROUP_HEADS + h) * nq + i)
    return pl.pallas_call(
        functools.partial(_diff_kernel, tk=tk, lambda_init=lambda_init, n_ride=len(rides)),
        grid=(batch, GROUP_HEADS, nq),
        in_specs=[small(a) for a in params] + [
            pl.BlockSpec((tq, LANES), lambda b, h, i: (b * nq + i, h)),
            pl.BlockSpec((seq, LANES), lambda b, h, i: (b, 4 + h)),
            pl.BlockSpec((seq, LANES), lambda b, h, i: (b, 8 + h))] + ride_specs,
        out_specs=[pl.BlockSpec((tq, LANES), lambda b, h, i: (b * nq + i, h))] + ride_specs,
        out_shape=[jax.ShapeDtypeStruct((n, GROUP_WIDTH), BF16)] + ride_shapes,
        scratch_shapes=[pltpu.VMEM((2 * tq, LANES), BF16)] + _flash_scratch(seq, tk, 2 * tq),
        compiler_params=_cparams("parallel", "parallel", "arbitrary"),
        name="diff_attention",
    )(*params, z, z, z, *rides)


def _na_bias_tables(rpb, rows):
    groups = rows // NA_GROUP_ROWS
    assert groups >= 3
    qc = np.arange(GRID_W)
    kc = np.arange(GRID_W)
    ws = np.clip(qc - NA_WIN_COLS // 2, 0, GRID_W - NA_WIN_COLS)
    col_ok = (kc[None, :] >= ws[:, None]) & (kc[None, :] < ws[:, None] + NA_WIN_COLS)
    edge = GRID_W - NA_WIN_COLS
    rp = jnp.pad(rpb.astype(F32), ((0, 0), (0, 0), (edge, edge)))
    toep = jnp.stack([rp[:, :, GRID_W - 1 - c:2 * GRID_W - 1 - c] for c in range(GRID_W)], axis=2)
    toep = jnp.where(jnp.asarray(col_ok)[None, None], toep, MASKED)
    masked = jnp.full((rpb.shape[0], GRID_W, GRID_W), MASKED, F32)
    tabs = []
    for m in (0, 1, groups - 1):
        kstart = np.clip(NA_GROUP_ROWS * m - NA_WIN_ROWS // 2, 0, rows - NA_KEY_ROWS)
        qr = NA_GROUP_ROWS * m + np.arange(NA_GROUP_ROWS)
        kr = kstart + np.arange(NA_KEY_ROWS)
        rs = np.clip(qr - NA_WIN_ROWS // 2, 0, rows - NA_WIN_ROWS)
        row_ok = (kr[None, :] >= rs[:, None]) & (kr[None, :] < rs[:, None] + NA_WIN_ROWS)
        ridx = kr[None, :] - qr[:, None] + NA_WIN_ROWS - 1
        tabs.append(jnp.concatenate([
            jnp.concatenate([toep[:, ridx[i, a]] if row_ok[i, a] else masked
                             for a in range(NA_KEY_ROWS)], axis=-1)
            for i in range(NA_GROUP_ROWS)], axis=1))
    return jnp.stack(tabs)


def _na_kernel(bias_ref, q_ref, k_ref, v_ref, o_ref, *, rows):
    groups = rows // NA_GROUP_ROWS
    tq = NA_GROUP_ROWS * GRID_W
    span = NA_KEY_ROWS * GRID_W

    def body(m, carry):
        cls = jnp.where(m == 0, 0, jnp.where(m == groups - 1, 2, 1))
        krow = jnp.clip(NA_GROUP_ROWS * m - NA_WIN_ROWS // 2, 0, rows - NA_KEY_ROWS)
        koff = pl.multiple_of(krow * GRID_W, GRID_W)
        qoff = pl.multiple_of(m * tq, tq)
        s = _qk(q_ref[pl.ds(qoff, tq), :], k_ref[pl.ds(koff, span), :]) + bias_ref[cls, 0]
        p = jnp.exp(s - jnp.max(s, axis=-1, keepdims=True))
        l = jnp.sum(p, axis=-1, keepdims=True)
        v = v_ref[pl.ds(koff, span), :]
        o = jnp.dot(p.astype(v.dtype), v, preferred_element_type=F32) / l
        o_ref[pl.ds(qoff, tq), :] = o.astype(o_ref.dtype)
        return carry

    lax.fori_loop(0, groups, body, 0, unroll=NA_UNROLL)


def _na_attention(z, rpb, batch, seq):
    n = z.shape[0]
    rows = seq // GRID_W
    bias = _na_bias_tables(rpb, rows)
    tq, span = NA_GROUP_ROWS * GRID_W, NA_KEY_ROWS * GRID_W
    return pl.pallas_call(
        functools.partial(_na_kernel, rows=rows),
        grid=(batch, GROUP_HEADS),
        in_specs=[pl.BlockSpec((3, 1, tq, span), lambda b, h: (0, h, 0, 0)),
                  pl.BlockSpec((seq, LANES), lambda b, h: (b, 12 + h)),
                  pl.BlockSpec((seq, LANES), lambda b, h: (b, 16 + h)),
                  pl.BlockSpec((seq, LANES), lambda b, h: (b, 20 + h))],
        out_specs=pl.BlockSpec((seq, LANES), lambda b, h: (b, h)),
        out_shape=jax.ShapeDtypeStruct((n, GROUP_WIDTH), BF16),
        compiler_params=_cparams("parallel", "parallel"),
        name="na_attention",
    )(bias, z, z, z)


def _swa_kernel(sink_ref, q_ref, k_ref, v_ref, o_ref, *, seq):
    sink = sink_ref[pl.program_id(1)]
    for c in range(SWA_TQ // SWA_SUB):
        rows = slice(c * SWA_SUB, (c + 1) * SWA_SUB)
        q0 = pl.program_id(2) * SWA_TQ + c * SWA_SUB
        koff = pl.multiple_of(jnp.clip(q0 - SWA_WINDOW, 0, seq - SWA_SPAN), SWA_WINDOW)
        s = _qk(q_ref[rows, :], k_ref[pl.ds(koff, SWA_SPAN), :])
        qpos = q0 + lax.broadcasted_iota(jnp.int32, s.shape, 0)
        kpos = koff + lax.broadcasted_iota(jnp.int32, s.shape, 1)
        s = jnp.where(jnp.abs(qpos - kpos) <= SWA_WINDOW, s, MASKED)
        m = jnp.maximum(jnp.max(s, axis=-1, keepdims=True), sink)
        p = jnp.exp(s - m)
        l = jnp.sum(p, axis=-1, keepdims=True) + jnp.exp(sink - m)
        v = v_ref[pl.ds(koff, SWA_SPAN), :]
        o_ref[rows, :] = (jnp.dot(p.astype(v.dtype), v, preferred_element_type=F32) / l).astype(o_ref.dtype)


def _swa_attention(z, sinks, batch, seq):
    n = z.shape[0]
    assert seq % SWA_TQ == 0 and seq >= SWA_SPAN
    nq = seq // SWA_TQ
    return pl.pallas_call(
        functools.partial(_swa_kernel, seq=seq),
        grid=(batch, GROUP_HEADS, nq),
        in_specs=[pl.BlockSpec(memory_space=pltpu.SMEM),
                  pl.BlockSpec((SWA_TQ, LANES), lambda b, h, i: (b * nq + i, 24 + h)),
                  pl.BlockSpec((seq, LANES), lambda b, h, i: (b, 28 + h // 2)),
                  pl.BlockSpec((seq, LANES), lambda b, h, i: (b, 30 + h // 2))],
        out_specs=pl.BlockSpec((SWA_TQ, LANES), lambda b, h, i: (b * nq + i, h)),
        out_shape=jax.ShapeDtypeStruct((n, GROUP_WIDTH), BF16),
        compiler_params=_cparams("parallel", "parallel", "arbitrary"),
        name="swa_attention",
    )(sinks, z, z, z)


def _dot_f32(x, w):
    xh = x.astype(BF16)
    xl = (x - xh.astype(F32)).astype(BF16)
    wh = w.astype(BF16)
    wl = (w - wh.astype(F32)).astype(BF16)
    d = lambda a, b: jnp.dot(a, b, preferred_element_type=F32)
    return d(xh, wh) + (d(xh, wl) + d(xl, wh))


def _top2(logits):
    lane = lax.broadcasted_iota(jnp.int32, logits.shape, 1)
    l1 = jnp.where(lane < N_EXPERTS, logits, MASKED)
    m1 = jnp.max(l1, axis=-1, keepdims=True)
    i1 = jnp.min(jnp.where(l1 == m1, lane, LANES), axis=-1, keepdims=True)
    l2 = jnp.where(lane == i1, MASKED, l1)
    m2 = jnp.max(l2, axis=-1, keepdims=True)
    i2 = jnp.min(jnp.where(l2 == m2, lane, LANES), axis=-1, keepdims=True)
    e2 = jnp.exp(m2 - m1)
    den = 1.0 + e2
    out = jnp.where(lane == 0, i1.astype(F32), 0.0)
    out = jnp.where(lane == 1, i2.astype(F32), out)
    out = jnp.where(lane == 2, 1.0 / den, out)
    return jnp.where(lane == 3, e2 / den, out)


def _outproj_kernel(*refs, routed):
    if routed:
        md, mn, ms, ml, w_ref, h_ref, g_ref, wr_ref, h1_ref, hn_ref, route_ref = refs
    else:
        md, mn, ms, ml, w_ref, h_ref, g_ref, h1_ref, hn_ref = refs
    acc = h_ref[...]
    for k, m in enumerate((md, mn, ms, ml)):
        acc = acc + jnp.dot(m[...], w_ref[k * GROUP_WIDTH:(k + 1) * GROUP_WIDTH, :],
                            preferred_element_type=F32)
    h1_ref[...] = acc
    hn = _rms(acc, g_ref[...])
    if routed:
        _slab_store(hn_ref, hn)
        route_ref[...] = _top2(_dot_f32(hn, wr_ref[...]))
    else:
        hn_ref[...] = hn.astype(hn_ref.dtype)


def _outproj(mixed, w, h, g, w_router=None, tm=256):
    n, d = h.shape
    tm = _tile(n, tm)
    routed = w_router is not None
    row = lambda width: pl.BlockSpec((tm, width), lambda i: (i, 0))
    full = lambda a: pl.BlockSpec(a.shape, lambda i: (0, 0))
    g = g.reshape(1, d)
    args = list(mixed) + [w, h, g]
    in_specs = [row(GROUP_WIDTH)] * 4 + [full(w), row(d), full(g)]
    if routed:
        args.append(w_router)
        in_specs.append(full(w_router))
        out_specs = [row(d), _slab_spec(tm, d, lambda i: (i, 0)), row(LANES)]
        out_shape = [jax.ShapeDtypeStruct((n, d), F32), _slab_shape(n, d),
                     jax.ShapeDtypeStruct((n, LANES), F32)]
    else:
        out_specs = [row(d), row(d)]
        out_shape = [jax.ShapeDtypeStruct((n, d), F32), jax.ShapeDtypeStruct((n, d), BF16)]
    return pl.pallas_call(
        functools.partial(_outproj_kernel, routed=routed),
        grid=(n // tm,),
        in_specs=in_specs, out_specs=out_specs, out_shape=out_shape,
        compiler_params=_cparams("parallel"),
        name="outproj_routed" if routed else "outproj",
    )(*args)


def _swiglu_step(x, wg_ref, wu_ref, wd_ref, o_ref, f):
    @pl.when(f == 0)
    def _():
        o_ref[...] = jnp.zeros(o_ref.shape, o_ref.dtype)

    g = jnp.dot(x, wg_ref[...], preferred_element_type=F32)
    u = jnp.dot(x, wu_ref[...], preferred_element_type=F32)
    a = (g * jax.nn.sigmoid(g) * u).astype(BF16)
    o_ref[...] += jnp.dot(a, wd_ref[...], preferred_element_type=F32)


def _ffn_kernel(x_ref, wg_ref, wu_ref, wd_ref, o_ref):
    _swiglu_step(x_ref[...], wg_ref, wu_ref, wd_ref, o_ref, pl.program_id(1))


def _dense_ffn(x, wg, wu, wd, tm=512, tf=1024):
    n, d = x.shape
    ff = wg.shape[1]
    tm, tf = _tile(n, tm), _tile(ff, tf)
    return pl.pallas_call(
        _ffn_kernel,
        grid=(n // tm, ff // tf),
        in_specs=[pl.BlockSpec((tm, d), lambda i, f: (i, 0)),
                  pl.BlockSpec((d, tf), lambda i, f: (0, f)),
                  pl.BlockSpec((d, tf), lambda i, f: (0, f)),
                  pl.BlockSpec((tf, d), lambda i, f: (f, 0))],
        out_specs=pl.BlockSpec((tm, d), lambda i, f: (i, 0)),
        out_shape=jax.ShapeDtypeStruct((n, d), F32),
        compiler_params=_cparams("parallel", "arbitrary"),
        name="dense_ffn",
    )(x, wg, wu, wd)


def _moe_kernel(blk_exp_ref, n_used_ref, tok_ref, tok_next_ref, hn_ref, wg_ref, wu_ref, wd_ref, o_ref,
                stage_ref, sem, xb_ref, acc_ref):
    b, f = pl.program_id(0), pl.program_id(1)
    tm, d = xb_ref.shape
    chunks = d // LANES
    n_used = n_used_ref[0]
    used = b < n_used
    last = f == pl.num_programs(1) - 1

    def row_copy(tok, r, slot):
        return pltpu.make_async_copy(hn_ref.at[pl.ds(tok[0, 0, r] * chunks, chunks)],
                                     stage_ref.at[slot, pl.ds(r * chunks, chunks)], sem.at[slot])

    def start_block(tok, slot):
        def go(r, carry):
            row_copy(tok, r, slot).start()
            return carry
        lax.fori_loop(0, tm, go, 0, unroll=8)

    def wait_block(slot):
        pltpu.make_async_copy(hn_ref.at[pl.ds(0, tm * chunks)], stage_ref.at[slot], sem.at[slot]).wait()

    @pl.when(used & (b == 0) & (f == 0))
    def _():
        start_block(tok_ref, 0)

    @pl.when(used & (f == 0))
    def _():
        slot = b % 2
        wait_block(slot)
        for c in range(chunks):
            xb_ref[:, c * LANES:(c + 1) * LANES] = _slab_chunk(stage_ref.at[slot], c, tm, d).astype(BF16)

    @pl.when((b + 1 < n_used) & (f == 1))
    def _():
        start_block(tok_next_ref, (b + 1) % 2)

    @pl.when(used)
    def _():
        _swiglu_step(xb_ref[...], wg_ref, wu_ref, wd_ref, acc_ref, f)

    @pl.when(used & last)
    def _():
        _slab_store(o_ref, acc_ref[...])

    @pl.when(jnp.logical_not(used) & last)
    def _():
        o_ref[...] = jnp.zeros(o_ref.shape, o_ref.dtype)


def _moe_ffn(hn, slot_tok, blk_exp, n_used, wg, wu, wd, tf=1024):
    d, ff = wg.shape[1], wg.shape[2]
    n_slots = slot_tok.shape[0]
    tf = _tile(ff, tf)
    nf = ff // tf
    assert nf >= 2
    n_blk = n_slots // MOE_TM
    tok = slot_tok.reshape(n_blk, 1, MOE_TM)
    tok_spec = lambda nxt: pl.BlockSpec((1, 1, MOE_TM), lambda b, f, be, nu: (jnp.minimum(b + nxt, n_blk - 1), 0, 0),
                                        memory_space=pltpu.SMEM)

    def blk(b, nu):
        return jnp.minimum(b, nu[0] - 1)

    def col(b, f, nu):
        return jnp.where(b < nu[0], f, nf - 1)

    grid_spec = pltpu.PrefetchScalarGridSpec(
        num_scalar_prefetch=2,
        grid=(n_blk, nf),
        in_specs=[tok_spec(0), tok_spec(1), pl.BlockSpec(memory_space=pl.ANY),
                  pl.BlockSpec((None, d, tf), lambda b, f, be, nu: (be[blk(b, nu)], 0, col(b, f, nu))),
                  pl.BlockSpec((None, d, tf), lambda b, f, be, nu: (be[blk(b, nu)], 0, col(b, f, nu))),
                  pl.BlockSpec((None, tf, d), lambda b, f, be, nu: (be[blk(b, nu)], col(b, f, nu), 0))],
        out_specs=_slab_spec(MOE_TM, d, lambda b, f, be, nu: (b, 0)),
        scratch_shapes=[pltpu.VMEM((2, MOE_TM * (d // LANES), LANES), F32), pltpu.SemaphoreType.DMA((2,)),
                        pltpu.VMEM((MOE_TM, d), BF16), pltpu.VMEM((MOE_TM, d), F32)],
    )
    return pl.pallas_call(
        _moe_kernel,
        grid_spec=grid_spec,
        out_shape=_slab_shape(n_slots, d),
        compiler_params=_cparams("arbitrary", "arbitrary"),
        name="moe_ffn",
    )(blk_exp, n_used, tok, tok, hn, wg, wu, wd)


def _ple_kernel(*refs, routed, last):
    if routed:
        h1_ref, pos_ref, pos_next_ref, ys_ref, route_ref, p_ref, gp_ref, wg_ref, wp_ref, gn_ref = refs[:10]
        outs, (stage_ref, sem, mix_ref) = refs[10:-3], refs[-3:]
        tm, d = h1_ref.shape
        chunks = d // LANES
        i = pl.program_id(0)

        def row_copy(pos, r, slot):
            return pltpu.make_async_copy(ys_ref.at[pl.ds(pos[0, 0, r] * chunks, chunks)],
                                         stage_ref.at[slot, pl.ds(r * chunks, chunks)], sem.at[slot])

        def start_rows(pos, slot):
            def go(r, carry):
                row_copy(pos, 2 * r, slot).start(priority=0)
                row_copy(pos, 2 * r + 1, slot).start(priority=1)
                return carry
            lax.fori_loop(0, tm, go, 0, unroll=4)

        @pl.when(i == 0)
        def _():
            start_rows(pos_ref, 0)

        @pl.when(i + 1 < pl.num_programs(0))
        def _():
            start_rows(pos_next_ref, (i + 1) % 2)

        slot = i % 2

        pltpu.make_async_copy(ys_ref.at[pl.ds(0, 2 * tm * chunks)], stage_ref.at[slot], sem.at[slot]).wait()

        route = route_ref[...]
        g0, g1 = route[:, 2:3], route[:, 3:4]
        y_ref = stage_ref.at[slot]
        for c in range(chunks):
            mix_ref[:, c * LANES:(c + 1) * LANES] = (
                g0 * y_ref[pl.ds(c, tm, stride=chunks), :]
                + g1 * y_ref[pl.ds(tm * chunks + c, tm, stride=chunks), :])
        h2 = h1_ref[...] + mix_ref[...]
    else:
        h1_ref, y_ref, p_ref, gp_ref, wg_ref, wp_ref, gn_ref = refs[:7]
        outs = refs[7:]
        h2 = h1_ref[...] + y_ref[...]
    hp = _rms(h2, gp_ref[...]).astype(BF16)
    gate = jax.nn.sigmoid(jnp.dot(hp, wg_ref[...], preferred_element_type=F32))
    pp = jnp.dot(p_ref[...].astype(BF16), wp_ref[...], preferred_element_type=F32)
    h3 = h2 + gate * pp
    outs[0][...] = _rms(h3, gn_ref[...]) if last else h3


def _ple(h1, y, routing, p, g_ple, w_gate, w_proj, g_final, last, tm=256):
    n, d = h1.shape
    tm = _tile(n, tm)
    nt = n // tm
    routed = routing is not None
    row = lambda width: pl.BlockSpec((tm, width), lambda i: (i, 0))
    full = lambda a: pl.BlockSpec(a.shape, lambda i: (0, 0))
    g_ple, g_final = g_ple.reshape(1, d), g_final.reshape(1, d)
    if routed:
        route, pos = routing
        pos = pos.reshape(nt, tm, 2).transpose(0, 2, 1).reshape(nt, 1, 2 * tm)
        pos_spec = lambda nxt: pl.BlockSpec((1, 1, 2 * tm), lambda i: (jnp.minimum(i + nxt, nt - 1), 0, 0),
                                            memory_space=pltpu.SMEM)
        args = [h1, pos, pos, y, route]
        in_specs = [row(d), pos_spec(0), pos_spec(1), pl.BlockSpec(memory_space=pl.ANY), row(LANES)]
        scratch = [pltpu.VMEM((2, 2 * tm * (d // LANES), LANES), F32), pltpu.SemaphoreType.DMA((2,)),
                   pltpu.VMEM((tm, d), F32)]
    else:
        args = [h1, y]
        in_specs = [row(d), row(d)]
        scratch = []
    args += [p, g_ple, w_gate, w_proj, g_final]
    in_specs += [row(p.shape[1]), full(g_ple), full(w_gate), full(w_proj), full(g_final)]
    return pl.pallas_call(
        functools.partial(_ple_kernel, routed=routed, last=last),
        grid=(nt,),
        in_specs=in_specs, out_specs=[row(d)], out_shape=[jax.ShapeDtypeStruct((n, d), F32)],
        scratch_shapes=scratch,
        compiler_params=_cparams("arbitrary"),
        name="ple_routed" if routed else "ple",
    )(*args)


def _dispatch(route, n):
    idx = route[:, :2].astype(jnp.int32)
    flat_e = idx.reshape(-1)
    onehot = (flat_e[:, None] == jnp.arange(N_EXPERTS, dtype=jnp.int32)[None, :]).astype(jnp.int32)
    csum = jnp.cumsum(onehot, axis=0)
    rank = jnp.sum(onehot * csum, axis=1) - 1
    counts = csum[-1]
    padded = (counts + MOE_TM - 1) // MOE_TM * MOE_TM
    pad_end = jnp.cumsum(padded)
    pad_start = pad_end - padded
    dest = pad_start[flat_e] + rank
    n_blk = (2 * n) // MOE_TM + N_EXPERTS
    flat_tok = jnp.arange(2 * n, dtype=jnp.int32) // 2
    slot_tok = jnp.zeros((n_blk * MOE_TM,), jnp.int32).at[dest].set(flat_tok)
    blk_exp = jnp.minimum(jnp.searchsorted(pad_end, jnp.arange(n_blk, dtype=jnp.int32) * MOE_TM, side="right"),
                          N_EXPERTS - 1).astype(jnp.int32)
    n_used = (pad_end[-1] // MOE_TM).astype(jnp.int32).reshape(1)
    return slot_tok, blk_exp, n_used, dest.reshape(n, 2).astype(jnp.int32)


def _mla_weights(w_uq, w_ukv):
    lora = w_uq.shape[0]
    wq = w_uq.reshape(lora, GROUP_HEADS, MLA_NOPE + MLA_ROPE)
    wq = jnp.pad(wq, ((0, 0), (0, 0), (0, 2 * LANES - MLA_NOPE - MLA_ROPE)))
    wkv = w_ukv.reshape(lora, GROUP_HEADS, 2, HEAD_DIM).transpose(0, 2, 1, 3)
    return wq.reshape(lora, -1).astype(BF16), wkv.reshape(lora, -1).astype(BF16)


def kernel(x, p, attn_norm, w_in, diff_lq1, diff_lk1, diff_lq2, diff_lk2, diff_subln, na_rpb, swa_sinks,
           mla_q_norm, mla_kv_norm, mla_w_uq, mla_w_ukv, w_out, ffn_norm, dense_w_gate, dense_w_up,
           dense_w_down, moe_router, moe_w_gate, moe_w_up, moe_w_down, ple_norm, ple_gate, ple_proj,
           final_norm):
    batch, seq, d = x.shape
    n = batch * seq
    depth = w_in.shape[0]
    tabs = _rope_tables(seq)
    h = x.reshape(n, d)
    bf16 = {}

    def weight(name, w):
        return bf16.pop(name).reshape(w.shape) if name in bf16 else w.astype(BF16)

    def attend(fn, *args, rides):
        out, *copies = fn(*args, [w.reshape(-1, w.shape[-1]) for w in rides.values()], batch, seq)
        bf16.update(zip(rides, copies))
        return out

    for i in range(depth):
        lambda_init = 0.8 - 0.6 * math.exp(-0.3 * i)
        j = i // 2
        routed = i % 2 == 1
        last = i == depth - 1
        w_in_i = jnp.pad(weight(("w_in", i), w_in[i]), ((0, 0), (0, IN_COLS_PAD - IN_COLS)))
        z = _inproj(h, attn_norm[i], w_in_i, tabs, seq)
        wq, wkv = _mla_weights(mla_w_uq[i], mla_w_ukv[i])
        q_l, k_l, v_l = _mla_up(z, mla_q_norm[i], mla_kv_norm[i], wq, wkv, tabs, seq)
        diff_rides = {("w_out", i): w_out[i]}
        mla_rides = {("ple_gate", i): ple_gate[i]}
        if routed:
            diff_rides[("moe_up", j)] = moe_w_up[j]
        else:
            diff_rides.update({("dense_gate", j): dense_w_gate[j], ("dense_up", j): dense_w_up[j]})
            mla_rides[("dense_down", j)] = dense_w_down[j]
            if not last:
                diff_rides[("moe_gate", j)] = moe_w_gate[j]
                mla_rides[("moe_down", j)] = moe_w_down[j]
        if not last:
            mla_rides[("w_in", i + 1)] = w_in[i + 1]
        mixed = (
            attend(_diff_attention, z, diff_lq1[i], diff_lk1[i], diff_lq2[i], diff_lk2[i], diff_subln[i],
                   lambda_init, rides=diff_rides),
            _na_attention(z, na_rpb[i], batch, seq),
            _swa_attention(z, swa_sinks[i], batch, seq),
            attend(_mla_attention, q_l, k_l, v_l, rides=mla_rides),
        )
        w_out_i = weight(("w_out", i), w_out[i])
        if not routed:
            h1, hn = _outproj(mixed, w_out_i, h, ffn_norm[i])
            y = _dense_ffn(hn, weight(("dense_gate", j), dense_w_gate[j]), weight(("dense_up", j), dense_w_up[j]),
                           weight(("dense_down", j), dense_w_down[j]))
            routing = None
        else:
            w_router = jnp.pad(moe_router[j], ((0, 0), (0, LANES - N_EXPERTS)))
            h1, hn, route = _outproj(mixed, w_out_i, h, ffn_norm[i], w_router)
            slot_tok, blk_exp, n_used, pos = _dispatch(route, n)
            y = _moe_ffn(hn, slot_tok, blk_exp, n_used, weight(("moe_gate", j), moe_w_gate[j]),
                         weight(("moe_up", j), moe_w_up[j]), weight(("moe_down", j), moe_w_down[j]))
            routing = (route, pos)
        (h,) = _ple(h1, y, routing, p[i].reshape(n, -1), ple_norm[i], weight(("ple_gate", i), ple_gate[i]),
                    ple_proj[i].astype(BF16), final_norm, last)
    return h.reshape(batch, seq, d)
```

```python
import functools
import math

import numpy as np
import jax
import jax.numpy as jnp
from jax import lax
from jax.experimental import pallas as pl
from jax.experimental.pallas import tpu as pltpu

F32 = jnp.float32
BF16 = jnp.bfloat16

HEAD_DIM = 128
GROUP_HEADS = 4
GROUP_WIDTH = GROUP_HEADS * HEAD_DIM
ROPE_THETA = 10000.0
NORM_EPS = 1e-6
DIFF_QK_DIM = 64
GRID_W = 64
NA_WIN_ROWS = 8
NA_WIN_COLS = 16
SWA_WINDOW = 128
MLA_LORA = 512
MLA_NOPE = 128
MLA_ROPE = 64
IN_COLS = 5184
N_EXPERTS = 8

LANES = 128
VMEM_LIMIT = 56 * 1024 * 1024
MASKED = -1e30
LOG2E = math.log2(math.e)

IN_TN = 512
IN_COLS_PAD = -(-IN_COLS // LANES) * LANES
NA_GROUP_ROWS = 4
NA_KEY_ROWS = NA_GROUP_ROWS + NA_WIN_ROWS
NA_UNROLL = 4
SWA_TQ = 1024
SWA_SUB = 256
SWA_SPAN = SWA_SUB + 2 * SWA_WINDOW
MOE_TM = 512


def _tile(n, pref):
    t = min(n, pref)
    assert n % t == 0, (n, t)
    return t


def _cparams(*sem):
    return pltpu.CompilerParams(dimension_semantics=sem, vmem_limit_bytes=VMEM_LIMIT)


def _rms(x, g):
    return x * lax.rsqrt(jnp.mean(x * x, axis=-1, keepdims=True) + NORM_EPS) * g


def _slab_shape(rows, width):
    return jax.ShapeDtypeStruct((rows * (width // LANES), LANES), F32)


def _slab_spec(tm, width, index_map):
    return pl.BlockSpec((tm * (width // LANES), LANES), index_map)


def _slab_chunk(ref, c, tm, width):
    return ref[pl.ds(c, tm, stride=width // LANES), :]


def _slab_store(ref, x):
    tm, width = x.shape
    for c in range(width // LANES):
        ref[pl.ds(c, tm, stride=width // LANES), :] = x[:, c * LANES:(c + 1) * LANES]


def _rope_tables(seq):
    def base(dim):
        inv = ROPE_THETA ** (-jnp.arange(0, dim, 2, dtype=F32) / dim)
        ang = jnp.arange(seq, dtype=F32)[:, None] * inv[None, :]
        ang = jnp.concatenate([ang, ang], axis=-1)
        return jnp.cos(ang), jnp.sin(ang)

    lane = np.arange(LANES)
    cos, sin = base(DIFF_QK_DIM)
    c64 = jnp.concatenate([cos, cos], axis=-1)
    s64 = jnp.concatenate([sin, sin], axis=-1)
    lo = jnp.asarray((lane % 64) < 32)
    sa64 = jnp.where(lo, -s64, 0.0)
    sb64 = jnp.where(lo, 0.0, s64)
    first = jnp.asarray(lane < 64)
    c64p, sa64p, sb64p = (jnp.where(first, t, 0.0) for t in (c64, sa64, sb64))
    cos, sin = base(HEAD_DIM)
    s128 = jnp.where(jnp.asarray(lane < 64), -sin, sin)
    return dict(c64=c64, sa64=sa64, sb64=sb64, c64p=c64p, sa64p=sa64p, sb64p=sb64p,
                c128=cos, s128=s128)


def _rope64(x, c, sa, sb):
    return x * c + pltpu.roll(x, 96, 1) * sa + pltpu.roll(x, 32, 1) * sb


def _rope128(x, c, s):
    return x * c + pltpu.roll(x, 64, 1) * s


def _inproj_kernel(h_ref, g_ref, w_ref, c64, sa64, sb64, c128, s128, o_ref):
    x = _rms(h_ref[...], g_ref[...]).astype(BF16)
    for j in range(-(-IN_COLS_PAD // IN_TN)):
        width = min(IN_TN, IN_COLS_PAD - j * IN_TN)
        acc = jnp.dot(x, w_ref[:, j * IN_TN:j * IN_TN + width], preferred_element_type=F32)
        for g in range(width // LANES):
            a = acc[:, g * LANES:(g + 1) * LANES]
            if j == 0:
                a = _rope64(a, c64[...], sa64[...], sb64[...]) * (DIFF_QK_DIM ** -0.5 * LOG2E)
            elif j == 1:
                a = _rope64(a, c64[...], sa64[...], sb64[...])
            elif j == 3:
                a = a * HEAD_DIM ** -0.5
            elif j == 6:
                a = _rope128(a, c128[...], s128[...]) * HEAD_DIM ** -0.5
            elif j == 7 and g < 2:
                a = _rope128(a, c128[...], s128[...])
            o_ref[:, j * IN_TN + g * LANES:j * IN_TN + (g + 1) * LANES] = a.astype(o_ref.dtype)


def _inproj(h, g, w, tabs, seq, tm=512):
    n, d = h.shape
    tm = _tile(seq, tm)
    per_seq = seq // tm
    tab_spec = pl.BlockSpec((tm, LANES), lambda i: (i % per_seq, 0))
    g = g.reshape(1, d)
    return pl.pallas_call(
        _inproj_kernel,
        grid=(n // tm,),
        in_specs=[pl.BlockSpec((tm, d), lambda i: (i, 0)),
                  pl.BlockSpec(g.shape, lambda i: (0, 0)),
                  pl.BlockSpec(w.shape, lambda i: (0, 0), pipeline_mode=pl.Buffered(1))] + [tab_spec] * 5,
        out_specs=pl.BlockSpec((tm, IN_COLS_PAD), lambda i: (i, 0)),
        out_shape=jax.ShapeDtypeStruct((n, IN_COLS_PAD), BF16),
        compiler_params=_cparams("parallel"),
        name="inproj",
    )(h, g, w, tabs["c64"], tabs["sa64"], tabs["sb64"], tabs["c128"], tabs["s128"])


def _mla_up_kernel(cq_ref, ckv_ref, kr_ref, gq_ref, gkv_ref, wq_ref, wkv_ref, c_ref, sa_ref, sb_ref,
                   q_ref, k_ref, v_ref):
    scale = (MLA_NOPE + MLA_ROPE) ** -0.5 * LOG2E
    c, sa, sb = c_ref[...], sa_ref[...], sb_ref[...]
    cq = _rms(cq_ref[...].astype(F32), gq_ref[...]).astype(BF16)
    qa = jnp.dot(cq, wq_ref[...], preferred_element_type=F32)
    for h in range(GROUP_HEADS):
        base = h * 2 * LANES
        q_ref[:, base:base + LANES] = (qa[:, base:base + LANES] * scale).astype(q_ref.dtype)
        q_ref[:, base + LANES:base + 2 * LANES] = (
            _rope64(qa[:, base + LANES:base + 2 * LANES], c, sa, sb) * scale).astype(q_ref.dtype)
    ckv = _rms(ckv_ref[...].astype(F32), gkv_ref[...]).astype(BF16)
    kva = jnp.dot(ckv, wkv_ref[...], preferred_element_type=F32)
    kr = _rope64(kr_ref[...].astype(F32), c, sa, sb).astype(k_ref.dtype)
    for h in range(GROUP_HEADS):
        base = h * 2 * LANES
        k_ref[:, base:base + LANES] = kva[:, h * LANES:(h + 1) * LANES].astype(k_ref.dtype)
        k_ref[:, base + LANES:base + 2 * LANES] = kr
    v_ref[...] = kva[:, GROUP_WIDTH:].astype(v_ref.dtype)


def _mla_up(z, gq, gkv, wq, wkv, tabs, seq, tm=512):
    n = z.shape[0]
    tm = _tile(seq, tm)
    per_seq = seq // tm
    tab_spec = pl.BlockSpec((tm, LANES), lambda i: (i % per_seq, 0))
    full = lambda a: pl.BlockSpec(a.shape, lambda i: (0,) * a.ndim)
    gq = gq.reshape(1, -1)
    gkv = gkv.reshape(1, -1)
    return pl.pallas_call(
        _mla_up_kernel,
        grid=(n // tm,),
        in_specs=[pl.BlockSpec((tm, MLA_LORA), lambda i: (i, 8)),
                  pl.BlockSpec((tm, MLA_LORA), lambda i: (i, 9)),
                  pl.BlockSpec((tm, LANES), lambda i: (i, 40)),
                  full(gq), full(gkv), full(wq), full(wkv), tab_spec, tab_spec, tab_spec],
        out_specs=[pl.BlockSpec((tm, 2 * GROUP_WIDTH), lambda i: (i, 0)),
                   pl.BlockSpec((tm, 2 * GROUP_WIDTH), lambda i: (i, 0)),
                   pl.BlockSpec((tm, GROUP_WIDTH), lambda i: (i, 0))],
        out_shape=[jax.ShapeDtypeStruct((n, 2 * GROUP_WIDTH), BF16),
                   jax.ShapeDtypeStruct((n, 2 * GROUP_WIDTH), BF16),
                   jax.ShapeDtypeStruct((n, GROUP_WIDTH), BF16)],
        compiler_params=_cparams("parallel"),
        name="mla_up",
    )(z, z, z, gq, gkv, wq, wkv, tabs["c64p"], tabs["sa64p"], tabs["sb64p"])


def _qk(q, k):
    return lax.dot_general(q, k, (((1,), (1,)), ((), ())), preferred_element_type=F32)


def _flash_core(q_ref, k_ref, v_ref, vt_ref, s_ref, m_ref, l_ref, acc_ref, tk):
    nt = k_ref.shape[0] // tk
    assert nt % 2 == 0

    @pl.when(pl.program_id(2) == 0)
    def _():
        vt_ref[...] = v_ref[...].T

    m_ref[...] = jnp.full(m_ref.shape, MASKED, F32)
    l_ref[...] = jnp.zeros(l_ref.shape, F32)
    acc_ref[...] = jnp.zeros(acc_ref.shape, F32)

    def scores(t, slot):
        off = pl.multiple_of(jnp.minimum(t, nt - 1) * tk, tk)
        s_ref[slot] = _qk(k_ref[pl.ds(off, tk), :], q_ref[...])

    def consume(t, slot):
        off = pl.multiple_of(t * tk, tk)
        s = s_ref[slot]
        m_prev = m_ref[...]
        m_new = jnp.maximum(m_prev, jnp.max(s, axis=0, keepdims=True))
        alpha = jnp.exp2(m_prev - m_new)
        p = jnp.exp2(s - m_new)
        l_ref[...] = alpha * l_ref[...] + jnp.sum(p, axis=0, keepdims=True)
        acc_ref[...] = alpha * acc_ref[...] + jnp.dot(vt_ref[:, pl.ds(off, tk)], p.astype(vt_ref.dtype),
                                                      preferred_element_type=F32)
        m_ref[...] = m_new

    scores(0, 0)

    def body(j, carry):
        scores(2 * j + 1, 1)
        consume(2 * j, 0)
        scores(2 * j + 2, 0)
        consume(2 * j + 1, 1)
        return carry

    lax.fori_loop(0, nt // 2, body, 0, unroll=2)


def _flash_scratch(seq, tk, queries):
    return [pltpu.VMEM((LANES, seq), BF16), pltpu.VMEM((2, tk, queries), F32),
            pltpu.VMEM((1, queries), F32), pltpu.VMEM((1, queries), F32), pltpu.VMEM((LANES, queries), F32)]


def _ride_specs(rides, steps, step_of):
    in_specs, out_specs, shapes = [], [], []
    for stacked, layer in rides:
        _, total, cols = stacked.shape
        rows = total // steps
        assert rows * steps == total and rows % 16 == 0, (stacked.shape, steps)
        in_specs.append(pl.BlockSpec((None, rows, cols), lambda *g, layer=layer: (layer, step_of(*g), 0)))
        out_specs.append(pl.BlockSpec((rows, cols), lambda *g: (step_of(*g), 0)))
        shapes.append(jax.ShapeDtypeStruct((total, cols), BF16))
    return in_specs, out_specs, shapes


def _ride_cast(w_refs, wb_refs):
    for w_ref, wb_ref in zip(w_refs, wb_refs):
        wb_ref[...] = w_ref[...].astype(wb_ref.dtype)


def _flash_kernel(*refs, tk, n_ride):
    (q_ref, k_ref, v_ref), w_refs = refs[:3], refs[3:3 + n_ride]
    o_ref, wb_refs = refs[3 + n_ride], refs[4 + n_ride:4 + 2 * n_ride]
    vt_ref, s_ref, m_ref, l_ref, acc_ref = refs[4 + 2 * n_ride:]
    _ride_cast(w_refs, wb_refs)
    _flash_core(q_ref, k_ref, v_ref, vt_ref, s_ref, m_ref, l_ref, acc_ref, tk)
    o_ref[...] = (acc_ref[...] / l_ref[...]).T.astype(o_ref.dtype)


def _mla_attention(q, k, v, rides, batch, seq, tq=1024, tk=512):
    n = q.shape[0]
    tq, tk = _tile(seq, tq), _tile(seq, tk)
    nq = seq // tq
    dq = 2 * LANES
    ride_in, ride_out, ride_shapes = _ride_specs(rides, batch * GROUP_HEADS * nq,
                                                 lambda b, h, i: (b * GROUP_HEADS + h) * nq + i)
    return pl.pallas_call(
        functools.partial(_flash_kernel, tk=tk, n_ride=len(rides)),
        grid=(batch, GROUP_HEADS, nq),
        in_specs=[pl.BlockSpec((tq, dq), lambda b, h, i: (b * nq + i, h)),
                  pl.BlockSpec((seq, dq), lambda b, h, i: (b, h)),
                  pl.BlockSpec((seq, LANES), lambda b, h, i: (b, h))] + ride_in,
        out_specs=[pl.BlockSpec((tq, LANES), lambda b, h, i: (b * nq + i, h))] + ride_out,
        out_shape=[jax.ShapeDtypeStruct((n, GROUP_WIDTH), BF16)] + ride_shapes,
        scratch_shapes=_flash_scratch(seq, tk, tq),
        compiler_params=_cparams("parallel", "parallel", "arbitrary"),
        name="mla_attention",
    )(q, k, v, *[stacked for stacked, _ in rides])


def _diff_kernel(*refs, tk, lambda_init, n_ride):
    (lq1_ref, lk1_ref, lq2_ref, lk2_ref, g_ref, q_ref, k_ref, v_ref), w_refs = refs[:8], refs[8:8 + n_ride]
    o_ref, wb_refs = refs[8 + n_ride], refs[9 + n_ride:9 + 2 * n_ride]
    qq_ref, vt_ref, s_ref, m_ref, l_ref, acc_ref = refs[9 + 2 * n_ride:]
    _ride_cast(w_refs, wb_refs)
    tq = q_ref.shape[0]
    q = q_ref[...]
    lane = lax.broadcasted_iota(jnp.int32, q.shape, 1)
    qq_ref[:tq, :] = jnp.where(lane < DIFF_QK_DIM, q, jnp.zeros_like(q))
    qq_ref[tq:, :] = jnp.where(lane >= DIFF_QK_DIM, q, jnp.zeros_like(q))
    _flash_core(qq_ref, k_ref, v_ref, vt_ref, s_ref, m_ref, l_ref, acc_ref, tk)
    lam = (jnp.exp(jnp.sum(lq1_ref[...] * lk1_ref[...], axis=-1, keepdims=True))
           - jnp.exp(jnp.sum(lq2_ref[...] * lk2_ref[...], axis=-1, keepdims=True)) + lambda_init)
    o = (acc_ref[:, :tq] / l_ref[:, :tq] - lam * (acc_ref[:, tq:] / l_ref[:, tq:])).T
    o_ref[...] = (_rms(o, g_ref[...]) * (1.0 - lambda_init)).astype(o_ref.dtype)


def _diff_attention(z, lq1, lk1, lq2, lk2, subln, lambda_init, rides, batch, seq, tq=512, tk=512):
    n = z.shape[0]
    tq, tk = _tile(seq, tq), _tile(seq, tk)
    nq = seq // tq
    vec = lambda a: a.reshape(1, -1)
    small = lambda a: pl.BlockSpec(a.shape, lambda b, h, i: (0, 0))
    params = [vec(a) for a in (lq1, lk1, lq2, lk2, subln)]
    ride_in, ride_out, ride_shapes = _ride_specs(rides, batch * GROUP_HEADS * nq,
                                                 lambda b, h, i: (b * GROUP_HEADS + h) * nq + i)
    return pl.pallas_call(
        functools.partial(_diff_kernel, tk=tk, lambda_init=lambda_init, n_ride=len(rides)),
        grid=(batch, GROUP_HEADS, nq),
        in_specs=[small(a) for a in params] + [
            pl.BlockSpec((tq, LANES), lambda b, h, i: (b * nq + i, h)),
            pl.BlockSpec((seq, LANES), lambda b, h, i: (b, 4 + h)),
            pl.BlockSpec((seq, LANES), lambda b, h, i: (b, 8 + h))] + ride_in,
        out_specs=[pl.BlockSpec((tq, LANES), lambda b, h, i: (b * nq + i, h))] + ride_out,
        out_shape=[jax.ShapeDtypeStruct((n, GROUP_WIDTH), BF16)] + ride_shapes,
        scratch_shapes=[pltpu.VMEM((2 * tq, LANES), BF16)] + _flash_scratch(seq, tk, 2 * tq),
        compiler_params=_cparams("parallel", "parallel", "arbitrary"),
        name="diff_attention",
    )(*params, z, z, z, *[stacked for stacked, _ in rides])


def _na_bias_tables(rpb, rows):
    groups = rows // NA_GROUP_ROWS
    assert groups >= 3
    qc = np.arange(GRID_W)
    kc = np.arange(GRID_W)
    ws = np.clip(qc - NA_WIN_COLS // 2, 0, GRID_W - NA_WIN_COLS)
    col_ok = (kc[None, :] >= ws[:, None]) & (kc[None, :] < ws[:, None] + NA_WIN_COLS)
    edge = GRID_W - NA_WIN_COLS
    rp = jnp.pad(rpb.astype(F32), ((0, 0), (0, 0), (edge, edge)))
    toep = jnp.stack([rp[:, :, GRID_W - 1 - c:2 * GRID_W - 1 - c] for c in range(GRID_W)], axis=2)
    toep = jnp.where(jnp.asarray(col_ok)[None, None], toep, MASKED)
    masked = jnp.full((rpb.shape[0], GRID_W, GRID_W), MASKED, F32)
    tabs = []
    for m in (0, 1, groups - 1):
        kstart = np.clip(NA_GROUP_ROWS * m - NA_WIN_ROWS // 2, 0, rows - NA_KEY_ROWS)
        qr = NA_GROUP_ROWS * m + np.arange(NA_GROUP_ROWS)
        kr = kstart + np.arange(NA_KEY_ROWS)
        rs = np.clip(qr - NA_WIN_ROWS // 2, 0, rows - NA_WIN_ROWS)
        row_ok = (kr[None, :] >= rs[:, None]) & (kr[None, :] < rs[:, None] + NA_WIN_ROWS)
        ridx = kr[None, :] - qr[:, None] + NA_WIN_ROWS - 1
        tabs.append(jnp.concatenate([
            jnp.concatenate([toep[:, ridx[i, a]] if row_ok[i, a] else masked
                             for a in range(NA_KEY_ROWS)], axis=-1)
            for i in range(NA_GROUP_ROWS)], axis=1))
    return jnp.stack(tabs)


def _na_kernel(bias_ref, q_ref, k_ref, v_ref, o_ref, *, rows):
    groups = rows // NA_GROUP_ROWS
    tq = NA_GROUP_ROWS * GRID_W
    span = NA_KEY_ROWS * GRID_W

    def body(m, carry):
        cls = jnp.where(m == 0, 0, jnp.where(m == groups - 1, 2, 1))
        krow = jnp.clip(NA_GROUP_ROWS * m - NA_WIN_ROWS // 2, 0, rows - NA_KEY_ROWS)
        koff = pl.multiple_of(krow * GRID_W, GRID_W)
        qoff = pl.multiple_of(m * tq, tq)
        s = _qk(q_ref[pl.ds(qoff, tq), :], k_ref[pl.ds(koff, span), :]) + bias_ref[cls, 0]
        p = jnp.exp(s - jnp.max(s, axis=-1, keepdims=True))
        l = jnp.sum(p, axis=-1, keepdims=True)
        v = v_ref[pl.ds(koff, span), :]
        o = jnp.dot(p.astype(v.dtype), v, preferred_element_type=F32) / l
        o_ref[pl.ds(qoff, tq), :] = o.astype(o_ref.dtype)
        return carry

    lax.fori_loop(0, groups, body, 0, unroll=NA_UNROLL)


def _na_attention(z, rpb, batch, seq):
    n = z.shape[0]
    rows = seq // GRID_W
    bias = _na_bias_tables(rpb, rows)
    tq, span = NA_GROUP_ROWS * GRID_W, NA_KEY_ROWS * GRID_W
    return pl.pallas_call(
        functools.partial(_na_kernel, rows=rows),
        grid=(batch, GROUP_HEADS),
        in_specs=[pl.BlockSpec((3, 1, tq, span), lambda b, h: (0, h, 0, 0)),
                  pl.BlockSpec((seq, LANES), lambda b, h: (b, 12 + h)),
                  pl.BlockSpec((seq, LANES), lambda b, h: (b, 16 + h)),
                  pl.BlockSpec((seq, LANES), lambda b, h: (b, 20 + h))],
        out_specs=pl.BlockSpec((seq, LANES), lambda b, h: (b, h)),
        out_shape=jax.ShapeDtypeStruct((n, GROUP_WIDTH), BF16),
        compiler_params=_cparams("parallel", "parallel"),
        name="na_attention",
    )(bias, z, z, z)


def _swa_kernel(sink_ref, q_ref, k_ref, v_ref, o_ref, *, seq):
    sink = sink_ref[pl.program_id(1)]
    for c in range(SWA_TQ // SWA_SUB):
        rows = slice(c * SWA_SUB, (c + 1) * SWA_SUB)
        q0 = pl.program_id(2) * SWA_TQ + c * SWA_SUB
        koff = pl.multiple_of(jnp.clip(q0 - SWA_WINDOW, 0, seq - SWA_SPAN), SWA_WINDOW)
        s = _qk(q_ref[rows, :], k_ref[pl.ds(koff, SWA_SPAN), :])
        qpos = q0 + lax.broadcasted_iota(jnp.int32, s.shape, 0)
        kpos = koff + lax.broadcasted_iota(jnp.int32, s.shape, 1)
        s = jnp.where(jnp.abs(qpos - kpos) <= SWA_WINDOW, s, MASKED)
        m = jnp.maximum(jnp.max(s, axis=-1, keepdims=True), sink)
        p = jnp.exp(s - m)
        l = jnp.sum(p, axis=-1, keepdims=True) + jnp.exp(sink - m)
        v = v_ref[pl.ds(koff, SWA_SPAN), :]
        o_ref[rows, :] = (jnp.dot(p.astype(v.dtype), v, preferred_element_type=F32) / l).astype(o_ref.dtype)


def _swa_attention(z, sinks, batch, seq):
    n = z.shape[0]
    assert seq % SWA_TQ == 0 and seq >= SWA_SPAN
    nq = seq // SWA_TQ
    return pl.pallas_call(
        functools.partial(_swa_kernel, seq=seq),
        grid=(batch, GROUP_HEADS, nq),
        in_specs=[pl.BlockSpec(memory_space=pltpu.SMEM),
                  pl.BlockSpec((SWA_TQ, LANES), lambda b, h, i: (b * nq + i, 24 + h)),
                  pl.BlockSpec((seq, LANES), lambda b, h, i: (b, 28 + h // 2)),
                  pl.BlockSpec((seq, LANES), lambda b, h, i: (b, 30 + h // 2))],
        out_specs=pl.BlockSpec((SWA_TQ, LANES), lambda b, h, i: (b * nq + i, h)),
        out_shape=jax.ShapeDtypeStruct((n, GROUP_WIDTH), BF16),
        compiler_params=_cparams("parallel", "parallel", "arbitrary"),
        name="swa_attention",
    )(sinks, z, z, z)


def _dot_f32(x, w):
    xh = x.astype(BF16)
    xl = (x - xh.astype(F32)).astype(BF16)
    wh = w.astype(BF16)
    wl = (w - wh.astype(F32)).astype(BF16)
    d = lambda a, b: jnp.dot(a, b, preferred_element_type=F32)
    return d(xh, wh) + (d(xh, wl) + d(xl, wh))


def _top2(logits):
    lane = lax.broadcasted_iota(jnp.int32, logits.shape, 1)
    l1 = jnp.where(lane < N_EXPERTS, logits, MASKED)
    m1 = jnp.max(l1, axis=-1, keepdims=True)
    i1 = jnp.min(jnp.where(l1 == m1, lane, LANES), axis=-1, keepdims=True)
    l2 = jnp.where(lane == i1, MASKED, l1)
    m2 = jnp.max(l2, axis=-1, keepdims=True)
    i2 = jnp.min(jnp.where(l2 == m2, lane, LANES), axis=-1, keepdims=True)
    e2 = jnp.exp(m2 - m1)
    den = 1.0 + e2
    out = jnp.where(lane == 0, i1.astype(F32), 0.0)
    out = jnp.where(lane == 1, i2.astype(F32), out)
    out = jnp.where(lane == 2, 1.0 / den, out)
    return jnp.where(lane == 3, e2 / den, out)


def _outproj_kernel(*refs, routed):
    if routed:
        md, mn, ms, ml, w_ref, h_ref, g_ref, wr_ref, h1_ref, hn_ref, route_ref = refs
    else:
        md, mn, ms, ml, w_ref, h_ref, g_ref, h1_ref, hn_ref = refs
    acc = h_ref[...]
    for k, m in enumerate((md, mn, ms, ml)):
        acc = acc + jnp.dot(m[...], w_ref[k * GROUP_WIDTH:(k + 1) * GROUP_WIDTH, :],
                            preferred_element_type=F32)
    h1_ref[...] = acc
    hn = _rms(acc, g_ref[...])
    if routed:
        _slab_store(hn_ref, hn)
        route_ref[...] = _top2(_dot_f32(hn, wr_ref[...]))
    else:
        hn_ref[...] = hn.astype(hn_ref.dtype)


def _outproj(mixed, w, h, g, w_router=None, tm=256):
    n, d = h.shape
    tm = _tile(n, tm)
    routed = w_router is not None
    row = lambda width: pl.BlockSpec((tm, width), lambda i: (i, 0))
    full = lambda a: pl.BlockSpec(a.shape, lambda i: (0, 0))
    g = g.reshape(1, d)
    args = list(mixed) + [w, h, g]
    in_specs = [row(GROUP_WIDTH)] * 4 + [full(w), row(d), full(g)]
    if routed:
        args.append(w_router)
        in_specs.append(full(w_router))
        out_specs = [row(d), _slab_spec(tm, d, lambda i: (i, 0)), row(LANES)]
        out_shape = [jax.ShapeDtypeStruct((n, d), F32), _slab_shape(n, d),
                     jax.ShapeDtypeStruct((n, LANES), F32)]
    else:
        out_specs = [row(d), row(d)]
        out_shape = [jax.ShapeDtypeStruct((n, d), F32), jax.ShapeDtypeStruct((n, d), BF16)]
    return pl.pallas_call(
        functools.partial(_outproj_kernel, routed=routed),
        grid=(n // tm,),
        in_specs=in_specs, out_specs=out_specs, out_shape=out_shape,
        compiler_params=_cparams("parallel"),
        name="outproj_routed" if routed else "outproj",
    )(*args)


def _swiglu_step(x, wg_ref, wu_ref, wd_ref, o_ref, f):
    @pl.when(f == 0)
    def _():
        o_ref[...] = jnp.zeros(o_ref.shape, o_ref.dtype)

    g = jnp.dot(x, wg_ref[...], preferred_element_type=F32)
    u = jnp.dot(x, wu_ref[...], preferred_element_type=F32)
    a = (g * jax.nn.sigmoid(g) * u).astype(BF16)
    o_ref[...] += jnp.dot(a, wd_ref[...], preferred_element_type=F32)


def _ffn_kernel(x_ref, wg_ref, wu_ref, wd_ref, o_ref):
    _swiglu_step(x_ref[...], wg_ref, wu_ref, wd_ref, o_ref, pl.program_id(1))


def _dense_ffn(x, wg, wu, wd, tm=512, tf=1024):
    n, d = x.shape
    ff = wg.shape[1]
    tm, tf = _tile(n, tm), _tile(ff, tf)
    return pl.pallas_call(
        _ffn_kernel,
        grid=(n // tm, ff // tf),
        in_specs=[pl.BlockSpec((tm, d), lambda i, f: (i, 0)),
                  pl.BlockSpec((d, tf), lambda i, f: (0, f)),
                  pl.BlockSpec((d, tf), lambda i, f: (0, f)),
                  pl.BlockSpec((tf, d), lambda i, f: (f, 0))],
        out_specs=pl.BlockSpec((tm, d), lambda i, f: (i, 0)),
        out_shape=jax.ShapeDtypeStruct((n, d), F32),
        compiler_params=_cparams("parallel", "arbitrary"),
        name="dense_ffn",
    )(x, wg, wu, wd)


def _moe_kernel(blk_exp_ref, n_used_ref, tok_ref, tok_next_ref, hn_ref, wg_ref, wu_ref, wd_ref, o_ref,
                stage_ref, sem, xb_ref, acc_ref):
    b, f = pl.program_id(0), pl.program_id(1)
    tm, d = xb_ref.shape
    chunks = d // LANES
    n_used = n_used_ref[0]
    used = b < n_used
    last = f == pl.num_programs(1) - 1

    def row_copy(tok, r, slot):
        return pltpu.make_async_copy(hn_ref.at[pl.ds(tok[0, 0, r] * chunks, chunks)],
                                     stage_ref.at[slot, pl.ds(r * chunks, chunks)], sem.at[slot])

    def start_block(tok, slot):
        def go(r, carry):
            row_copy(tok, r, slot).start()
            return carry
        lax.fori_loop(0, tm, go, 0, unroll=8)

    def wait_block(slot):
        pltpu.make_async_copy(hn_ref.at[pl.ds(0, tm * chunks)], stage_ref.at[slot], sem.at[slot]).wait()

    @pl.when(used & (b == 0) & (f == 0))
    def _():
        start_block(tok_ref, 0)

    @pl.when(used & (f == 0))
    def _():
        slot = b % 2
        wait_block(slot)
        for c in range(chunks):
            xb_ref[:, c * LANES:(c + 1) * LANES] = _slab_chunk(stage_ref.at[slot], c, tm, d).astype(BF16)

    @pl.when((b + 1 < n_used) & (f == 1))
    def _():
        start_block(tok_next_ref, (b + 1) % 2)

    @pl.when(used)
    def _():
        _swiglu_step(xb_ref[...], wg_ref, wu_ref, wd_ref, acc_ref, f)

    @pl.when(used & last)
    def _():
        _slab_store(o_ref, acc_ref[...])

    @pl.when(jnp.logical_not(used) & last)
    def _():
        o_ref[...] = jnp.zeros(o_ref.shape, o_ref.dtype)


def _moe_ffn(hn, slot_tok, blk_exp, n_used, wg, wu, wd, tf=1024):
    d, ff = wg.shape[1], wg.shape[2]
    n_slots = slot_tok.shape[0]
    tf = _tile(ff, tf)
    nf = ff // tf
    assert nf >= 2
    n_blk = n_slots // MOE_TM
    tok = slot_tok.reshape(n_blk, 1, MOE_TM)
    tok_spec = lambda nxt: pl.BlockSpec((1, 1, MOE_TM), lambda b, f, be, nu: (jnp.minimum(b + nxt, n_blk - 1), 0, 0),
                                        memory_space=pltpu.SMEM)

    def blk(b, nu):
        return jnp.minimum(b, nu[0] - 1)

    def col(b, f, nu):
        return jnp.where(b < nu[0], f, nf - 1)

    grid_spec = pltpu.PrefetchScalarGridSpec(
        num_scalar_prefetch=2,
        grid=(n_blk, nf),
        in_specs=[tok_spec(0), tok_spec(1), pl.BlockSpec(memory_space=pl.ANY),
                  pl.BlockSpec((None, d, tf), lambda b, f, be, nu: (be[blk(b, nu)], 0, col(b, f, nu))),
                  pl.BlockSpec((None, d, tf), lambda b, f, be, nu: (be[blk(b, nu)], 0, col(b, f, nu))),
                  pl.BlockSpec((None, tf, d), lambda b, f, be, nu: (be[blk(b, nu)], col(b, f, nu), 0))],
        out_specs=_slab_spec(MOE_TM, d, lambda b, f, be, nu: (b, 0)),
        scratch_shapes=[pltpu.VMEM((2, MOE_TM * (d // LANES), LANES), F32), pltpu.SemaphoreType.DMA((2,)),
                        pltpu.VMEM((MOE_TM, d), BF16), pltpu.VMEM((MOE_TM, d), F32)],
    )
    return pl.pallas_call(
        _moe_kernel,
        grid_spec=grid_spec,
        out_shape=_slab_shape(n_slots, d),
        compiler_params=_cparams("arbitrary", "arbitrary"),
        name="moe_ffn",
    )(blk_exp, n_used, tok, tok, hn, wg, wu, wd)


def _ple_kernel(*refs, routed, last):
    if routed:
        h1_ref, pos_ref, pos_next_ref, ys_ref, route_ref, p_ref, gp_ref, wg_ref, wp_ref, gn_ref = refs[:10]
        outs, (stage_ref, sem, mix_ref) = refs[10:-3], refs[-3:]
        tm, d = h1_ref.shape
        chunks = d // LANES
        i = pl.program_id(0)

        def row_copy(pos, r, slot):
            return pltpu.make_async_copy(ys_ref.at[pl.ds(pos[0, 0, r] * chunks, chunks)],
                                         stage_ref.at[slot, pl.ds(r * chunks, chunks)], sem.at[slot])

        def start_rows(pos, slot):
            def go(r, carry):
                row_copy(pos, 2 * r, slot).start(priority=0)
                row_copy(pos, 2 * r + 1, slot).start(priority=1)
                return carry
            lax.fori_loop(0, tm, go, 0, unroll=4)

        @pl.when(i == 0)
        def _():
            start_rows(pos_ref, 0)

        @pl.when(i + 1 < pl.num_programs(0))
        def _():
            start_rows(pos_next_ref, (i + 1) % 2)

        slot = i % 2

        pltpu.make_async_copy(ys_ref.at[pl.ds(0, 2 * tm * chunks)], stage_ref.at[slot], sem.at[slot]).wait()

        route = route_ref[...]
        g0, g1 = route[:, 2:3], route[:, 3:4]
        y_ref = stage_ref.at[slot]
        for c in range(chunks):
            mix_ref[:, c * LANES:(c + 1) * LANES] = (
                g0 * y_ref[pl.ds(c, tm, stride=chunks), :]
                + g1 * y_ref[pl.ds(tm * chunks + c, tm, stride=chunks), :])
        h2 = h1_ref[...] + mix_ref[...]
    else:
        h1_ref, y_ref, p_ref, gp_ref, wg_ref, wp_ref, gn_ref = refs[:7]
        outs = refs[7:]
        h2 = h1_ref[...] + y_ref[...]
    hp = _rms(h2, gp_ref[...]).astype(BF16)
    gate = jax.nn.sigmoid(jnp.dot(hp, wg_ref[...], preferred_element_type=F32))
    pp = jnp.dot(p_ref[...].astype(BF16), wp_ref[...], preferred_element_type=F32)
    h3 = h2 + gate * pp
    outs[0][...] = _rms(h3, gn_ref[...]) if last else h3


def _ple(h1, y, routing, p, g_ple, w_gate, w_proj, g_final, last, tm=256):
    n, d = h1.shape
    tm = _tile(n, tm)
    nt = n // tm
    routed = routing is not None
    row = lambda width: pl.BlockSpec((tm, width), lambda i: (i, 0))
    full = lambda a: pl.BlockSpec(a.shape, lambda i: (0, 0))
    g_ple, g_final = g_ple.reshape(1, d), g_final.reshape(1, d)
    if routed:
        route, pos = routing
        pos = pos.reshape(nt, tm, 2).transpose(0, 2, 1).reshape(nt, 1, 2 * tm)
        pos_spec = lambda nxt: pl.BlockSpec((1, 1, 2 * tm), lambda i: (jnp.minimum(i + nxt, nt - 1), 0, 0),
                                            memory_space=pltpu.SMEM)
        args = [h1, pos, pos, y, route]
        in_specs = [row(d), pos_spec(0), pos_spec(1), pl.BlockSpec(memory_space=pl.ANY), row(LANES)]
        scratch = [pltpu.VMEM((2, 2 * tm * (d // LANES), LANES), F32), pltpu.SemaphoreType.DMA((2,)),
                   pltpu.VMEM((tm, d), F32)]
    else:
        args = [h1, y]
        in_specs = [row(d), row(d)]
        scratch = []
    args += [p, g_ple, w_gate, w_proj, g_final]
    in_specs += [row(p.shape[1]), full(g_ple), full(w_gate), full(w_proj), full(g_final)]
    return pl.pallas_call(
        functools.partial(_ple_kernel, routed=routed, last=last),
        grid=(nt,),
        in_specs=in_specs, out_specs=[row(d)], out_shape=[jax.ShapeDtypeStruct((n, d), F32)],
        scratch_shapes=scratch,
        compiler_params=_cparams("arbitrary"),
        name="ple_routed" if routed else "ple",
    )(*args)


def _dispatch(route, n):
    idx = route[:, :2].astype(jnp.int32)
    flat_e = idx.reshape(-1)
    onehot = (flat_e[:, None] == jnp.arange(N_EXPERTS, dtype=jnp.int32)[None, :]).astype(jnp.int32)
    csum = jnp.cumsum(onehot, axis=0)
    rank = jnp.sum(onehot * csum, axis=1) - 1
    counts = csum[-1]
    padded = (counts + MOE_TM - 1) // MOE_TM * MOE_TM
    pad_end = jnp.cumsum(padded)
    pad_start = pad_end - padded
    dest = pad_start[flat_e] + rank
    n_blk = (2 * n) // MOE_TM + N_EXPERTS
    flat_tok = jnp.arange(2 * n, dtype=jnp.int32) // 2
    slot_tok = jnp.zeros((n_blk * MOE_TM,), jnp.int32).at[dest].set(flat_tok)
    blk_exp = jnp.minimum(jnp.searchsorted(pad_end, jnp.arange(n_blk, dtype=jnp.int32) * MOE_TM, side="right"),
                          N_EXPERTS - 1).astype(jnp.int32)
    n_used = (pad_end[-1] // MOE_TM).astype(jnp.int32).reshape(1)
    return slot_tok, blk_exp, n_used, dest.reshape(n, 2).astype(jnp.int32)


def _mla_weights(w_uq, w_ukv):
    lora = w_uq.shape[0]
    wq = w_uq.reshape(lora, GROUP_HEADS, MLA_NOPE + MLA_ROPE)
    wq = jnp.pad(wq, ((0, 0), (0, 0), (0, 2 * LANES - MLA_NOPE - MLA_ROPE)))
    wkv = w_ukv.reshape(lora, GROUP_HEADS, 2, HEAD_DIM).transpose(0, 2, 1, 3)
    return wq.reshape(lora, -1).astype(BF16), wkv.reshape(lora, -1).astype(BF16)


def kernel(x, p, attn_norm, w_in, diff_lq1, diff_lk1, diff_lq2, diff_lk2, diff_subln, na_rpb, swa_sinks,
           mla_q_norm, mla_kv_norm, mla_w_uq, mla_w_ukv, w_out, ffn_norm, dense_w_gate, dense_w_up,
           dense_w_down, moe_router, moe_w_gate, moe_w_up, moe_w_down, ple_norm, ple_gate, ple_proj,
           final_norm):
    batch, seq, d = x.shape
    n = batch * seq
    depth = w_in.shape[0]
    tabs = _rope_tables(seq)
    h = x.reshape(n, d)
    bf16 = {}

    def weight(name, stacked, layer):
        return bf16.pop(name).reshape(stacked.shape[1:]) if name in bf16 else stacked[layer].astype(BF16)

    def attend(fn, *args, rides):
        pairs = [(w.reshape(w.shape[0], -1, w.shape[-1]), layer) for w, layer in rides.values()]
        out, *copies = fn(*args, pairs, batch, seq)
        bf16.update(zip(rides, copies))
        return out

    for i in range(depth):
        lambda_init = 0.8 - 0.6 * math.exp(-0.3 * i)
        j = i // 2
        routed = i % 2 == 1
        last = i == depth - 1
        w_in_i = jnp.pad(weight(("w_in", i), w_in, i), ((0, 0), (0, IN_COLS_PAD - IN_COLS)))
        z = _inproj(h, attn_norm[i], w_in_i, tabs, seq)
        wq, wkv = _mla_weights(mla_w_uq[i], mla_w_ukv[i])
        q_l, k_l, v_l = _mla_up(z, mla_q_norm[i], mla_kv_norm[i], wq, wkv, tabs, seq)
        diff_rides = {("w_out", i): (w_out, i)}
        mla_rides = {("ple_gate", i): (ple_gate, i)}
        if routed:
            diff_rides[("moe_up", j)] = (moe_w_up, j)
        else:
            diff_rides.update({("dense_gate", j): (dense_w_gate, j), ("dense_up", j): (dense_w_up, j)})
            mla_rides[("dense_down", j)] = (dense_w_down, j)
            if not last:
                diff_rides[("moe_gate", j)] = (moe_w_gate, j)
                mla_rides[("moe_down", j)] = (moe_w_down, j)
        if not last:
            mla_rides[("w_in", i + 1)] = (w_in, i + 1)
        mixed = (
            attend(_diff_attention, z, diff_lq1[i], diff_lk1[i], diff_lq2[i], diff_lk2[i], diff_subln[i],
                   lambda_init, rides=diff_rides),
            _na_attention(z, na_rpb[i], batch, seq),
            _swa_attention(z, swa_sinks[i], batch, seq),
            attend(_mla_attention, q_l, k_l, v_l, rides=mla_rides),
        )
        w_out_i = weight(("w_out", i), w_out, i)
        if not routed:
            h1, hn = _outproj(mixed, w_out_i, h, ffn_norm[i])
            y = _dense_ffn(hn, weight(("dense_gate", j), dense_w_gate, j), weight(("dense_up", j), dense_w_up, j),
                           weight(("dense_down", j), dense_w_down, j))
            routing = None
        else:
            w_router = jnp.pad(moe_router[j], ((0, 0), (0, LANES - N_EXPERTS)))
            h1, hn, route = _outproj(mixed, w_out_i, h, ffn_norm[i], w_router)
            slot_tok, blk_exp, n_used, pos = _dispatch(route, n)
            y = _moe_ffn(hn, slot_tok, blk_exp, n_used, weight(("moe_gate", j), moe_w_gate, j),
                         weight(("moe_up", j), moe_w_up, j), weight(("moe_down", j), moe_w_down, j))
            routing = (route, pos)
        (h,) = _ple(h1, y, routing, p[i].reshape(n, -1), ple_norm[i], weight(("ple_gate", i), ple_gate, i),
                    ple_proj[i].astype(BF16), final_norm, last)
    return h.reshape(batch, seq, d)
```

```python
import functools
import math

import numpy as np
import jax
import jax.numpy as jnp
from jax import lax
from jax.experimental import pallas as pl
from jax.experimental.pallas import tpu as pltpu

F32 = jnp.float32
BF16 = jnp.bfloat16

HEAD_DIM = 128
GROUP_HEADS = 4
GROUP_WIDTH = GROUP_HEADS * HEAD_DIM
ROPE_THETA = 10000.0
NORM_EPS = 1e-6
DIFF_QK_DIM = 64
GRID_W = 64
NA_WIN_ROWS = 8
NA_WIN_COLS = 16
SWA_WINDOW = 128
MLA_LORA = 512
MLA_NOPE = 128
MLA_ROPE = 64
IN_COLS = 5184
N_EXPERTS = 8

LANES = 128
VMEM_LIMIT = 56 * 1024 * 1024
MASKED = -1e30
LOG2E = math.log2(math.e)

IN_TN = 512
IN_COLS_PAD = -(-IN_COLS // LANES) * LANES
NA_GROUP_ROWS = 4
NA_KEY_ROWS = NA_GROUP_ROWS + NA_WIN_ROWS
NA_UNROLL = 4
SWA_TQ = 1024
SWA_SUB = 256
SWA_SPAN = SWA_SUB + 2 * SWA_WINDOW
MOE_TM = 512


def _tile(n, pref):
    t = min(n, pref)
    assert n % t == 0, (n, t)
    return t


def _cparams(*sem):
    return pltpu.CompilerParams(dimension_semantics=sem, vmem_limit_bytes=VMEM_LIMIT)


def _rms(x, g):
    return x * lax.rsqrt(jnp.mean(x * x, axis=-1, keepdims=True) + NORM_EPS) * g


def _slab_shape(rows, width):
    return jax.ShapeDtypeStruct((rows * (width // LANES), LANES), F32)


def _slab_spec(tm, width, index_map):
    return pl.BlockSpec((tm * (width // LANES), LANES), index_map)


def _slab_chunk(ref, c, tm, width):
    return ref[pl.ds(c, tm, stride=width // LANES), :]


def _slab_store(ref, x):
    tm, width = x.shape
    for c in range(width // LANES):
        ref[pl.ds(c, tm, stride=width // LANES), :] = x[:, c * LANES:(c + 1) * LANES]


def _rope_tables(seq):
    def base(dim):
        inv = ROPE_THETA ** (-jnp.arange(0, dim, 2, dtype=F32) / dim)
        ang = jnp.arange(seq, dtype=F32)[:, None] * inv[None, :]
        ang = jnp.concatenate([ang, ang], axis=-1)
        return jnp.cos(ang), jnp.sin(ang)

    lane = np.arange(LANES)
    cos, sin = base(DIFF_QK_DIM)
    c64 = jnp.concatenate([cos, cos], axis=-1)
    s64 = jnp.concatenate([sin, sin], axis=-1)
    lo = jnp.asarray((lane % 64) < 32)
    sa64 = jnp.where(lo, -s64, 0.0)
    sb64 = jnp.where(lo, 0.0, s64)
    first = jnp.asarray(lane < 64)
    c64p, sa64p, sb64p = (jnp.where(first, t, 0.0) for t in (c64, sa64, sb64))
    cos, sin = base(HEAD_DIM)
    s128 = jnp.where(jnp.asarray(lane < 64), -sin, sin)
    return dict(c64=c64, sa64=sa64, sb64=sb64, c64p=c64p, sa64p=sa64p, sb64p=sb64p,
                c128=cos, s128=s128)


def _rope64(x, c, sa, sb):
    return x * c + pltpu.roll(x, 96, 1) * sa + pltpu.roll(x, 32, 1) * sb


def _rope128(x, c, s):
    return x * c + pltpu.roll(x, 64, 1) * s


def _inproj_kernel(h_ref, g_ref, w_ref, c64, sa64, sb64, c128, s128, o_ref):
    x = _rms(h_ref[...], g_ref[...]).astype(BF16)
    for j in range(-(-IN_COLS_PAD // IN_TN)):
        width = min(IN_TN, IN_COLS_PAD - j * IN_TN)
        acc = jnp.dot(x, w_ref[:, j * IN_TN:j * IN_TN + width], preferred_element_type=F32)
        for g in range(width // LANES):
            a = acc[:, g * LANES:(g + 1) * LANES]
            if j == 0:
                a = _rope64(a, c64[...], sa64[...], sb64[...]) * (DIFF_QK_DIM ** -0.5 * LOG2E)
            elif j == 1:
                a = _rope64(a, c64[...], sa64[...], sb64[...])
            elif j == 3:
                a = a * HEAD_DIM ** -0.5
            elif j == 6:
                a = _rope128(a, c128[...], s128[...]) * HEAD_DIM ** -0.5
            elif j == 7 and g < 2:
                a = _rope128(a, c128[...], s128[...])
            o_ref[:, j * IN_TN + g * LANES:j * IN_TN + (g + 1) * LANES] = a.astype(o_ref.dtype)


def _inproj(h, g, w, tabs, seq, tm=512):
    n, d = h.shape
    tm = _tile(seq, tm)
    per_seq = seq // tm
    tab_spec = pl.BlockSpec((tm, LANES), lambda i: (i % per_seq, 0))
    g = g.reshape(1, d)
    return pl.pallas_call(
        _inproj_kernel,
        grid=(n // tm,),
        in_specs=[pl.BlockSpec((tm, d), lambda i: (i, 0)),
                  pl.BlockSpec(g.shape, lambda i: (0, 0)),
                  pl.BlockSpec(w.shape, lambda i: (0, 0), pipeline_mode=pl.Buffered(1))] + [tab_spec] * 5,
        out_specs=pl.BlockSpec((tm, IN_COLS_PAD), lambda i: (i, 0)),
        out_shape=jax.ShapeDtypeStruct((n, IN_COLS_PAD), BF16),
        compiler_params=_cparams("parallel"),
        name="inproj",
    )(h, g, w, tabs["c64"], tabs["sa64"], tabs["sb64"], tabs["c128"], tabs["s128"])


def _mla_up_kernel(cq_ref, ckv_ref, kr_ref, gq_ref, gkv_ref, wq_ref, wkv_ref, c_ref, sa_ref, sb_ref,
                   q_ref, k_ref, v_ref):
    scale = (MLA_NOPE + MLA_ROPE) ** -0.5 * LOG2E
    c, sa, sb = c_ref[...], sa_ref[...], sb_ref[...]
    cq = _rms(cq_ref[...].astype(F32), gq_ref[...]).astype(BF16)
    qa = jnp.dot(cq, wq_ref[...], preferred_element_type=F32)
    for h in range(GROUP_HEADS):
        base = h * 2 * LANES
        q_ref[:, base:base + LANES] = (qa[:, base:base + LANES] * scale).astype(q_ref.dtype)
        q_ref[:, base + LANES:base + 2 * LANES] = (
            _rope64(qa[:, base + LANES:base + 2 * LANES], c, sa, sb) * scale).astype(q_ref.dtype)
    ckv = _rms(ckv_ref[...].astype(F32), gkv_ref[...]).astype(BF16)
    kva = jnp.dot(ckv, wkv_ref[...], preferred_element_type=F32)
    kr = _rope64(kr_ref[...].astype(F32), c, sa, sb).astype(k_ref.dtype)
    for h in range(GROUP_HEADS):
        base = h * 2 * LANES
        k_ref[:, base:base + LANES] = kva[:, h * LANES:(h + 1) * LANES].astype(k_ref.dtype)
        k_ref[:, base + LANES:base + 2 * LANES] = kr
    v_ref[...] = kva[:, GROUP_WIDTH:].astype(v_ref.dtype)


def _mla_up(z, gq, gkv, wq, wkv, tabs, seq, tm=512):
    n = z.shape[0]
    tm = _tile(seq, tm)
    per_seq = seq // tm
    tab_spec = pl.BlockSpec((tm, LANES), lambda i: (i % per_seq, 0))
    full = lambda a: pl.BlockSpec(a.shape, lambda i: (0,) * a.ndim)
    gq = gq.reshape(1, -1)
    gkv = gkv.reshape(1, -1)
    return pl.pallas_call(
        _mla_up_kernel,
        grid=(n // tm,),
        in_specs=[pl.BlockSpec((tm, MLA_LORA), lambda i: (i, 8)),
                  pl.BlockSpec((tm, MLA_LORA), lambda i: (i, 9)),
                  pl.BlockSpec((tm, LANES), lambda i: (i, 40)),
                  full(gq), full(gkv), full(wq), full(wkv), tab_spec, tab_spec, tab_spec],
        out_specs=[pl.BlockSpec((tm, 2 * GROUP_WIDTH), lambda i: (i, 0)),
                   pl.BlockSpec((tm, 2 * GROUP_WIDTH), lambda i: (i, 0)),
                   pl.BlockSpec((tm, GROUP_WIDTH), lambda i: (i, 0))],
        out_shape=[jax.ShapeDtypeStruct((n, 2 * GROUP_WIDTH), BF16),
                   jax.ShapeDtypeStruct((n, 2 * GROUP_WIDTH), BF16),
                   jax.ShapeDtypeStruct((n, GROUP_WIDTH), BF16)],
        compiler_params=_cparams("parallel"),
        name="mla_up",
    )(z, z, z, gq, gkv, wq, wkv, tabs["c64p"], tabs["sa64p"], tabs["sb64p"])


def _qk(q, k):
    return lax.dot_general(q, k, (((1,), (1,)), ((), ())), preferred_element_type=F32)


def _flash_core(q_ref, k_ref, v_ref, vt_ref, s_ref, m_ref, l_ref, acc_ref, tk):
    nt = k_ref.shape[0] // tk
    assert nt % 2 == 0

    @pl.when(pl.program_id(2) == 0)
    def _():
        vt_ref[...] = v_ref[...].T

    m_ref[...] = jnp.full(m_ref.shape, MASKED, F32)
    l_ref[...] = jnp.zeros(l_ref.shape, F32)
    acc_ref[...] = jnp.zeros(acc_ref.shape, F32)

    def scores(t, slot):
        off = pl.multiple_of(jnp.minimum(t, nt - 1) * tk, tk)
        s_ref[slot] = _qk(k_ref[pl.ds(off, tk), :], q_ref[...])

    def consume(t, slot):
        off = pl.multiple_of(t * tk, tk)
        s = s_ref[slot]
        m_prev = m_ref[...]
        m_new = jnp.maximum(m_prev, jnp.max(s, axis=0, keepdims=True))
        alpha = jnp.exp2(m_prev - m_new)
        p = jnp.exp2(s - m_new)
        l_ref[...] = alpha * l_ref[...] + jnp.sum(p, axis=0, keepdims=True)
        acc_ref[...] = alpha * acc_ref[...] + jnp.dot(vt_ref[:, pl.ds(off, tk)], p.astype(vt_ref.dtype),
                                                      preferred_element_type=F32)
        m_ref[...] = m_new

    scores(0, 0)

    def body(j, carry):
        scores(2 * j + 1, 1)
        consume(2 * j, 0)
        scores(2 * j + 2, 0)
        consume(2 * j + 1, 1)
        return carry

    lax.fori_loop(0, nt // 2, body, 0, unroll=2)


def _flash_scratch(seq, tk, queries):
    return [pltpu.VMEM((LANES, seq), BF16), pltpu.VMEM((2, tk, queries), F32),
            pltpu.VMEM((1, queries), F32), pltpu.VMEM((1, queries), F32), pltpu.VMEM((LANES, queries), F32)]


def _ride_specs(rides, steps, step_of):
    specs = []
    for ride in rides:
        rows = ride.shape[0] // steps
        assert rows * steps == ride.shape[0] and rows % 16 == 0, (ride.shape, steps)
        specs.append(pl.BlockSpec((rows, ride.shape[1]), lambda *g: (step_of(*g), 0)))
    return specs, [jax.ShapeDtypeStruct(ride.shape, BF16) for ride in rides]


def _ride_cast(w_refs, wb_refs):
    for w_ref, wb_ref in zip(w_refs, wb_refs):
        wb_ref[...] = w_ref[...].astype(wb_ref.dtype)


def _flash_kernel(*refs, tk, n_ride):
    (q_ref, k_ref, v_ref), w_refs = refs[:3], refs[3:3 + n_ride]
    o_ref, wb_refs = refs[3 + n_ride], refs[4 + n_ride:4 + 2 * n_ride]
    vt_ref, s_ref, m_ref, l_ref, acc_ref = refs[4 + 2 * n_ride:]
    _ride_cast(w_refs, wb_refs)
    _flash_core(q_ref, k_ref, v_ref, vt_ref, s_ref, m_ref, l_ref, acc_ref, tk)
    o_ref[...] = (acc_ref[...] / l_ref[...]).T.astype(o_ref.dtype)


def _mla_attention(q, k, v, rides, batch, seq, tq=1024, tk=512):
    n = q.shape[0]
    tq, tk = _tile(seq, tq), _tile(seq, tk)
    nq = seq // tq
    dq = 2 * LANES
    ride_specs, ride_shapes = _ride_specs(rides, batch * GROUP_HEADS * nq,
                                          lambda b, h, i: (b * GROUP_HEADS + h) * nq + i)
    return pl.pallas_call(
        functools.partial(_flash_kernel, tk=tk, n_ride=len(rides)),
        grid=(batch, GROUP_HEADS, nq),
        in_specs=[pl.BlockSpec((tq, dq), lambda b, h, i: (b * nq + i, h)),
                  pl.BlockSpec((seq, dq), lambda b, h, i: (b, h)),
                  pl.BlockSpec((seq, LANES), lambda b, h, i: (b, h))] + ride_specs,
        out_specs=[pl.BlockSpec((tq, LANES), lambda b, h, i: (b * nq + i, h))] + ride_specs,
        out_shape=[jax.ShapeDtypeStruct((n, GROUP_WIDTH), BF16)] + ride_shapes,
        scratch_shapes=_flash_scratch(seq, tk, tq),
        compiler_params=_cparams("parallel", "parallel", "arbitrary"),
        name="mla_attention",
    )(q, k, v, *rides)


def _diff_kernel(*refs, tk, lambda_init, n_ride):
    (lq1_ref, lk1_ref, lq2_ref, lk2_ref, g_ref, q_ref, k_ref, v_ref), w_refs = refs[:8], refs[8:8 + n_ride]
    o_ref, wb_refs = refs[8 + n_ride], refs[9 + n_ride:9 + 2 * n_ride]
    qq_ref, vt_ref, s_ref, m_ref, l_ref, acc_ref = refs[9 + 2 * n_ride:]
    _ride_cast(w_refs, wb_refs)
    tq = q_ref.shape[0]
    q = q_ref[...]
    lane = lax.broadcasted_iota(jnp.int32, q.shape, 1)
    qq_ref[:tq, :] = jnp.where(lane < DIFF_QK_DIM, q, jnp.zeros_like(q))
    qq_ref[tq:, :] = jnp.where(lane >= DIFF_QK_DIM, q, jnp.zeros_like(q))
    _flash_core(qq_ref, k_ref, v_ref, vt_ref, s_ref, m_ref, l_ref, acc_ref, tk)
    lam = (jnp.exp(jnp.sum(lq1_ref[...] * lk1_ref[...], axis=-1, keepdims=True))
           - jnp.exp(jnp.sum(lq2_ref[...] * lk2_ref[...], axis=-1, keepdims=True)) + lambda_init)
    o = (acc_ref[:, :tq] / l_ref[:, :tq] - lam * (acc_ref[:, tq:] / l_ref[:, tq:])).T
    o_ref[...] = (_rms(o, g_ref[...]) * (1.0 - lambda_init)).astype(o_ref.dtype)


def _diff_attention(z, lq1, lk1, lq2, lk2, subln, lambda_init, rides, batch, seq, tq=512, tk=512):
    n = z.shape[0]
    tq, tk = _tile(seq, tq), _tile(seq, tk)
    nq = seq // tq
    vec = lambda a: a.reshape(1, -1)
    small = lambda a: pl.BlockSpec(a.shape, lambda b, h, i: (0, 0))
    params = [vec(a) for a in (lq1, lk1, lq2, lk2, subln)]
    ride_specs, ride_shapes = _ride_specs(rides, batch * GROUP_HEADS * nq,
                                          lambda b, h, i: (b * GROUP_HEADS + h) * nq + i)
    return pl.pallas_call(
        functools.partial(_diff_kernel, tk=tk, lambda_init=lambda_init, n_ride=len(rides)),
        grid=(batch, GROUP_HEADS, nq),
        in_specs=[small(a) for a in params] + [
            pl.BlockSpec((tq, LANES), lambda b, h, i: (b * nq + i, h)),
            pl.BlockSpec((seq, LANES), lambda b, h, i: (b, 4 + h)),
            pl.BlockSpec((seq, LANES), lambda b, h, i: (b, 8 + h))] + ride_specs,
        out_specs=[pl.BlockSpec((tq, LANES), lambda b, h, i: (b * nq + i, h))] + ride_specs,
        out_shape=[jax.ShapeDtypeStruct((n, GROUP_WIDTH), BF16)] + ride_shapes,
        scratch_shapes=[pltpu.VMEM((2 * tq, LANES), BF16)] + _flash_scratch(seq, tk, 2 * tq),
        compiler_params=_cparams("parallel", "parallel", "arbitrary"),
        name="diff_attention",
    )(*params, z, z, z, *rides)


def _na_bias_tables(rpb, rows):
    groups = rows // NA_GROUP_ROWS
    assert groups >= 3
    qc = np.arange(GRID_W)
    kc = np.arange(GRID_W)
    ws = np.clip(qc - NA_WIN_COLS // 2, 0, GRID_W - NA_WIN_COLS)
    col_ok = (kc[None, :] >= ws[:, None]) & (kc[None, :] < ws[:, None] + NA_WIN_COLS)
    edge = GRID_W - NA_WIN_COLS
    rp = jnp.pad(rpb.astype(F32), ((0, 0), (0, 0), (edge, edge)))
    toep = jnp.stack([rp[:, :, GRID_W - 1 - c:2 * GRID_W - 1 - c] for c in range(GRID_W)], axis=2)
    toep = jnp.where(jnp.asarray(col_ok)[None, None], toep, MASKED)
    masked = jnp.full((rpb.shape[0], GRID_W, GRID_W), MASKED, F32)
    tabs = []
    for m in (0, 1, groups - 1):
        kstart = np.clip(NA_GROUP_ROWS * m - NA_WIN_ROWS // 2, 0, rows - NA_KEY_ROWS)
        qr = NA_GROUP_ROWS * m + np.arange(NA_GROUP_ROWS)
        kr = kstart + np.arange(NA_KEY_ROWS)
        rs = np.clip(qr - NA_WIN_ROWS // 2, 0, rows - NA_WIN_ROWS)
        row_ok = (kr[None, :] >= rs[:, None]) & (kr[None, :] < rs[:, None] + NA_WIN_ROWS)
        ridx = kr[None, :] - qr[:, None] + NA_WIN_ROWS - 1
        tabs.append(jnp.concatenate([
            jnp.concatenate([toep[:, ridx[i, a]] if row_ok[i, a] else masked
                             for a in range(NA_KEY_ROWS)], axis=-1)
            for i in range(NA_GROUP_ROWS)], axis=1))
    return jnp.stack(tabs)


def _na_kernel(bias_ref, q_ref, k_ref, v_ref, o_ref, *, rows):
    groups = rows // NA_GROUP_ROWS
    tq = NA_GROUP_ROWS * GRID_W
    span = NA_KEY_ROWS * GRID_W

    def body(m, carry):
        cls = jnp.where(m == 0, 0, jnp.where(m == groups - 1, 2, 1))
        krow = jnp.clip(NA_GROUP_ROWS * m - NA_WIN_ROWS // 2, 0, rows - NA_KEY_ROWS)
        koff = pl.multiple_of(krow * GRID_W, GRID_W)
        qoff = pl.multiple_of(m * tq, tq)
        s = _qk(q_ref[pl.ds(qoff, tq), :], k_ref[pl.ds(koff, span), :]) + bias_ref[cls, 0]
        p = jnp.exp(s - jnp.max(s, axis=-1, keepdims=True))
        l = jnp.sum(p, axis=-1, keepdims=True)
        v = v_ref[pl.ds(koff, span), :]
        o = jnp.dot(p.astype(v.dtype), v, preferred_element_type=F32) / l
        o_ref[pl.ds(qoff, tq), :] = o.astype(o_ref.dtype)
        return carry

    lax.fori_loop(0, groups, body, 0, unroll=NA_UNROLL)


def _na_attention(z, rpb, batch, seq):
    n = z.shape[0]
    rows = seq // GRID_W
    bias = _na_bias_tables(rpb, rows)
    tq, span = NA_GROUP_ROWS * GRID_W, NA_KEY_ROWS * GRID_W
    return pl.pallas_call(
        functools.partial(_na_kernel, rows=rows),
        grid=(batch, GROUP_HEADS),
        in_specs=[pl.BlockSpec((3, 1, tq, span), lambda b, h: (0, h, 0, 0)),
                  pl.BlockSpec((seq, LANES), lambda b, h: (b, 12 + h)),
                  pl.BlockSpec((seq, LANES), lambda b, h: (b, 16 + h)),
                  pl.BlockSpec((seq, LANES), lambda b, h: (b, 20 + h))],
        out_specs=pl.BlockSpec((seq, LANES), lambda b, h: (b, h)),
        out_shape=jax.ShapeDtypeStruct((n, GROUP_WIDTH), BF16),
        compiler_params=_cparams("parallel", "parallel"),
        name="na_attention",
    )(bias, z, z, z)


def _swa_kernel(sink_ref, q_ref, k_ref, v_ref, o_ref, *, seq):
    sink = sink_ref[pl.program_id(1)]
    for c in range(SWA_TQ // SWA_SUB):
        rows = slice(c * SWA_SUB, (c + 1) * SWA_SUB)
        q0 = pl.program_id(2) * SWA_TQ + c * SWA_SUB
        koff = pl.multiple_of(jnp.clip(q0 - SWA_WINDOW, 0, seq - SWA_SPAN), SWA_WINDOW)
        s = _qk(q_ref[rows, :], k_ref[pl.ds(koff, SWA_SPAN), :])
        qpos = q0 + lax.broadcasted_iota(jnp.int32, s.shape, 0)
        kpos = koff + lax.broadcasted_iota(jnp.int32, s.shape, 1)
        s = jnp.where(jnp.abs(qpos - kpos) <= SWA_WINDOW, s, MASKED)
        m = jnp.maximum(jnp.max(s, axis=-1, keepdims=True), sink)
        p = jnp.exp(s - m)
        l = jnp.sum(p, axis=-1, keepdims=True) + jnp.exp(sink - m)
        v = v_ref[pl.ds(koff, SWA_SPAN), :]
        o_ref[rows, :] = (jnp.dot(p.astype(v.dtype), v, preferred_element_type=F32) / l).astype(o_ref.dtype)


def _swa_attention(z, sinks, batch, seq):
    n = z.shape[0]
    assert seq % SWA_TQ == 0 and seq >= SWA_SPAN
    nq = seq // SWA_TQ
    return pl.pallas_call(
        functools.partial(_swa_kernel, seq=seq),
        grid=(batch, GROUP_HEADS, nq),
        in_specs=[pl.BlockSpec(memory_space=pltpu.SMEM),
                  pl.BlockSpec((SWA_TQ, LANES), lambda b, h, i: (b * nq + i, 24 + h)),
                  pl.BlockSpec((seq, LANES), lambda b, h, i: (b, 28 + h // 2)),
                  pl.BlockSpec((seq, LANES), lambda b, h, i: (b, 30 + h // 2))],
        out_specs=pl.BlockSpec((SWA_TQ, LANES), lambda b, h, i: (b * nq + i, h)),
        out_shape=jax.ShapeDtypeStruct((n, GROUP_WIDTH), BF16),
        compiler_params=_cparams("parallel", "parallel", "arbitrary"),
        name="swa_attention",
    )(sinks, z, z, z)


def _dot_f32(x, w):
    xh = x.astype(BF16)
    xl = (x - xh.astype(F32)).astype(BF16)
    wh = w.astype(BF16)
    wl = (w - wh.astype(F32)).astype(BF16)
    d = lambda a, b: jnp.dot(a, b, preferred_element_type=F32)
    return d(xh, wh) + (d(xh, wl) + d(xl, wh))


def _top2(logits):
    lane = lax.broadcasted_iota(jnp.int32, logits.shape, 1)
    l1 = jnp.where(lane < N_EXPERTS, logits, MASKED)
    m1 = jnp.max(l1, axis=-1, keepdims=True)
    i1 = jnp.min(jnp.where(l1 == m1, lane, LANES), axis=-1, keepdims=True)
    l2 = jnp.where(lane == i1, MASKED, l1)
    m2 = jnp.max(l2, axis=-1, keepdims=True)
    i2 = jnp.min(jnp.where(l2 == m2, lane, LANES), axis=-1, keepdims=True)
    e2 = jnp.exp(m2 - m1)
    den = 1.0 + e2
    out = jnp.where(lane == 0, i1.astype(F32), 0.0)
    out = jnp.where(lane == 1, i2.astype(F32), out)
    out = jnp.where(lane == 2, 1.0 / den, out)
    return jnp.where(lane == 3, e2 / den, out)


def _outproj_kernel(*refs, routed):
    if routed:
        md, mn, ms, ml, w_ref, h_ref, g_ref, wr_ref, h1_ref, hn_ref, route_ref = refs
    else:
        md, mn, ms, ml, w_ref, h_ref, g_ref, h1_ref, hn_ref = refs
    acc = h_ref[...]
    for k, m in enumerate((md, mn, ms, ml)):
        acc = acc + jnp.dot(m[...], w_ref[k * GROUP_WIDTH:(k + 1) * GROUP_WIDTH, :],
                            preferred_element_type=F32)
    h1_ref[...] = acc
    hn = _rms(acc, g_ref[...])
    if routed:
        _slab_store(hn_ref, hn)
        route_ref[...] = _top2(_dot_f32(hn, wr_ref[...]))
    else:
        hn_ref[...] = hn.astype(hn_ref.dtype)


def _outproj(mixed, w, h, g, w_router=None, tm=256):
    n, d = h.shape
    tm = _tile(n, tm)
    routed = w_router is not None
    row = lambda width: pl.BlockSpec((tm, width), lambda i: (i, 0))
    full = lambda a: pl.BlockSpec(a.shape, lambda i: (0, 0))
    g = g.reshape(1, d)
    args = list(mixed) + [w, h, g]
    in_specs = [row(GROUP_WIDTH)] * 4 + [full(w), row(d), full(g)]
    if routed:
        args.append(w_router)
        in_specs.append(full(w_router))
        out_specs = [row(d), _slab_spec(tm, d, lambda i: (i, 0)), row(LANES)]
        out_shape = [jax.ShapeDtypeStruct((n, d), F32), _slab_shape(n, d),
                     jax.ShapeDtypeStruct((n, LANES), F32)]
    else:
        out_specs = [row(d), row(d)]
        out_shape = [jax.ShapeDtypeStruct((n, d), F32), jax.ShapeDtypeStruct((n, d), BF16)]
    return pl.pallas_call(
        functools.partial(_outproj_kernel, routed=routed),
        grid=(n // tm,),
        in_specs=in_specs, out_specs=out_specs, out_shape=out_shape,
        compiler_params=_cparams("parallel"),
        name="outproj_routed" if routed else "outproj",
    )(*args)


def _swiglu_step(x, wg_ref, wu_ref, wd_ref, o_ref, f):
    @pl.when(f == 0)
    def _():
        o_ref[...] = jnp.zeros(o_ref.shape, o_ref.dtype)

    g = jnp.dot(x, wg_ref[...], preferred_element_type=F32)
    u = jnp.dot(x, wu_ref[...], preferred_element_type=F32)
    a = (g * jax.nn.sigmoid(g) * u).astype(BF16)
    o_ref[...] += jnp.dot(a, wd_ref[...], preferred_element_type=F32)


def _ffn_kernel(x_ref, wg_ref, wu_ref, wd_ref, o_ref):
    _swiglu_step(x_ref[...], wg_ref, wu_ref, wd_ref, o_ref, pl.program_id(1))


def _dense_ffn(x, wg, wu, wd, tm=512, tf=1024):
    n, d = x.shape
    ff = wg.shape[1]
    tm, tf = _tile(n, tm), _tile(ff, tf)
    return pl.pallas_call(
        _ffn_kernel,
        grid=(n // tm, ff // tf),
        in_specs=[pl.BlockSpec((tm, d), lambda i, f: (i, 0)),
                  pl.BlockSpec((d, tf), lambda i, f: (0, f)),
                  pl.BlockSpec((d, tf), lambda i, f: (0, f)),
                  pl.BlockSpec((tf, d), lambda i, f: (f, 0))],
        out_specs=pl.BlockSpec((tm, d), lambda i, f: (i, 0)),
        out_shape=jax.ShapeDtypeStruct((n, d), F32),
        compiler_params=_cparams("parallel", "arbitrary"),
        name="dense_ffn",
    )(x, wg, wu, wd)


def _moe_kernel(blk_exp_ref, n_used_ref, tok_ref, tok_next_ref, hn_ref, wg_ref, wu_ref, wd_ref, o_ref,
                stage_ref, sem, xb_ref, acc_ref):
    b, f = pl.program_id(0), pl.program_id(1)
    tm, d = xb_ref.shape
    chunks = d // LANES
    n_used = n_used_ref[0]
    used = b < n_used
    last = f == pl.num_programs(1) - 1

    def row_copy(tok, r, slot):
        return pltpu.make_async_copy(hn_ref.at[pl.ds(tok[0, 0, r] * chunks, chunks)],
                                     stage_ref.at[slot, pl.ds(r * chunks, chunks)], sem.at[slot])

    def start_block(tok, slot):
        def go(r, carry):
            row_copy(tok, r, slot).start()
            return carry
        lax.fori_loop(0, tm, go, 0, unroll=8)

    def wait_block(slot):
        pltpu.make_async_copy(hn_ref.at[pl.ds(0, tm * chunks)], stage_ref.at[slot], sem.at[slot]).wait()

    @pl.when(used & (b == 0) & (f == 0))
    def _():
        start_block(tok_ref, 0)

    @pl.when(used & (f == 0))
    def _():
        slot = b % 2
        wait_block(slot)
        for c in range(chunks):
            xb_ref[:, c * LANES:(c + 1) * LANES] = _slab_chunk(stage_ref.at[slot], c, tm, d).astype(BF16)

    @pl.when((b + 1 < n_used) & (f == 1))
    def _():
        start_block(tok_next_ref, (b + 1) % 2)

    @pl.when(used)
    def _():
        _swiglu_step(xb_ref[...], wg_ref, wu_ref, wd_ref, acc_ref, f)

    @pl.when(used & last)
    def _():
        _slab_store(o_ref, acc_ref[...])

    @pl.when(jnp.logical_not(used) & last)
    def _():
        o_ref[...] = jnp.zeros(o_ref.shape, o_ref.dtype)


def _moe_ffn(hn, slot_tok, blk_exp, n_used, wg, wu, wd, tf=1024):
    d, ff = wg.shape[1], wg.shape[2]
    n_slots = slot_tok.shape[0]
    tf = _tile(ff, tf)
    nf = ff // tf
    assert nf >= 2
    n_blk = n_slots // MOE_TM
    tok = slot_tok.reshape(n_blk, 1, MOE_TM)
    tok_spec = lambda nxt: pl.BlockSpec((1, 1, MOE_TM), lambda b, f, be, nu: (jnp.minimum(b + nxt, n_blk - 1), 0, 0),
                                        memory_space=pltpu.SMEM)

    def blk(b, nu):
        return jnp.minimum(b, nu[0] - 1)

    def col(b, f, nu):
        return jnp.where(b < nu[0], f, nf - 1)

    grid_spec = pltpu.PrefetchScalarGridSpec(
        num_scalar_prefetch=2,
        grid=(n_blk, nf),
        in_specs=[tok_spec(0), tok_spec(1), pl.BlockSpec(memory_space=pl.ANY),
                  pl.BlockSpec((None, d, tf), lambda b, f, be, nu: (be[blk(b, nu)], 0, col(b, f, nu))),
                  pl.BlockSpec((None, d, tf), lambda b, f, be, nu: (be[blk(b, nu)], 0, col(b, f, nu))),
                  pl.BlockSpec((None, tf, d), lambda b, f, be, nu: (be[blk(b, nu)], col(b, f, nu), 0))],
        out_specs=_slab_spec(MOE_TM, d, lambda b, f, be, nu: (b, 0)),
        scratch_shapes=[pltpu.VMEM((2, MOE_TM * (d // LANES), LANES), F32), pltpu.SemaphoreType.DMA((2,)),
                        pltpu.VMEM((MOE_TM, d), BF16), pltpu.VMEM((MOE_TM, d), F32)],
    )
    return pl.pallas_call(
        _moe_kernel,
        grid_spec=grid_spec,
        out_shape=_slab_shape(n_slots, d),
        compiler_params=_cparams("arbitrary", "arbitrary"),
        name="moe_ffn",
    )(blk_exp, n_used, tok, tok, hn, wg, wu, wd)


def _ple_kernel(*refs, routed, last):
    if routed:
        h1_ref, pos_ref, pos_next_ref, ys_ref, route_ref, p_ref, gp_ref, wg_ref, wp_ref, gn_ref = refs[:10]
        outs, (stage_ref, sem, mix_ref) = refs[10:-3], refs[-3:]
        tm, d = h1_ref.shape
        chunks = d // LANES
        i = pl.program_id(0)

        def row_copy(pos, r, slot):
            return pltpu.make_async_copy(ys_ref.at[pl.ds(pos[0, 0, r] * chunks, chunks)],
                                         stage_ref.at[slot, pl.ds(r * chunks, chunks)], sem.at[slot])

        def start_rows(pos, slot):
            def go(r, carry):
                row_copy(pos, 2 * r, slot).start(priority=0)
                row_copy(pos, 2 * r + 1, slot).start(priority=1)
                return carry
            lax.fori_loop(0, tm, go, 0, unroll=4)

        @pl.when(i == 0)
        def _():
            start_rows(pos_ref, 0)

        @pl.when(i + 1 < pl.num_programs(0))
        def _():
            start_rows(pos_next_ref, (i + 1) % 2)

        slot = i % 2

        pltpu.make_async_copy(ys_ref.at[pl.ds(0, 2 * tm * chunks)], stage_ref.at[slot], sem.at[slot]).wait()

        route = route_ref[...]
        g0, g1 = route[:, 2:3], route[:, 3:4]
        y_ref = stage_ref.at[slot]
        for c in range(chunks):
            mix_ref[:, c * LANES:(c + 1) * LANES] = (
                g0 * y_ref[pl.ds(c, tm, stride=chunks), :]
                + g1 * y_ref[pl.ds(tm * chunks + c, tm, stride=chunks), :])
        h2 = h1_ref[...] + mix_ref[...]
    else:
        h1_ref, y_ref, p_ref, gp_ref, wg_ref, wp_ref, gn_ref = refs[:7]
        outs = refs[7:]
        h2 = h1_ref[...] + y_ref[...]
    hp = _rms(h2, gp_ref[...]).astype(BF16)
    gate = jax.nn.sigmoid(jnp.dot(hp, wg_ref[...], preferred_element_type=F32))
    pp = jnp.dot(p_ref[...].astype(BF16), wp_ref[...], preferred_element_type=F32)
    h3 = h2 + gate * pp
    outs[0][...] = _rms(h3, gn_ref[...]) if last else h3


def _ple(h1, y, routing, p, g_ple, w_gate, w_proj, g_final, last, tm=256):
    n, d = h1.shape
    tm = _tile(n, tm)
    nt = n // tm
    routed = routing is not None
    row = lambda width: pl.BlockSpec((tm, width), lambda i: (i, 0))
    full = lambda a: pl.BlockSpec(a.shape, lambda i: (0, 0))
    g_ple, g_final = g_ple.reshape(1, d), g_final.reshape(1, d)
    if routed:
        route, pos = routing
        pos = pos.reshape(nt, tm, 2).transpose(0, 2, 1).reshape(nt, 1, 2 * tm)
        pos_spec = lambda nxt: pl.BlockSpec((1, 1, 2 * tm), lambda i: (jnp.minimum(i + nxt, nt - 1), 0, 0),
                                            memory_space=pltpu.SMEM)
        args = [h1, pos, pos, y, route]
        in_specs = [row(d), pos_spec(0), pos_spec(1), pl.BlockSpec(memory_space=pl.ANY), row(LANES)]
        scratch = [pltpu.VMEM((2, 2 * tm * (d // LANES), LANES), F32), pltpu.SemaphoreType.DMA((2,)),
                   pltpu.VMEM((tm, d), F32)]
    else:
        args = [h1, y]
        in_specs = [row(d), row(d)]
        scratch = []
    args += [p, g_ple, w_gate, w_proj, g_final]
    in_specs += [row(p.shape[1]), full(g_ple), full(w_gate), full(w_proj), full(g_final)]
    return pl.pallas_call(
        functools.partial(_ple_kernel, routed=routed, last=last),
        grid=(nt,),
        in_specs=in_specs, out_specs=[row(d)], out_shape=[jax.ShapeDtypeStruct((n, d), F32)],
        scratch_shapes=scratch,
        compiler_params=_cparams("arbitrary"),
        name="ple_routed" if routed else "ple",
    )(*args)


def _dispatch(route, n):
    idx = route[:, :2].astype(jnp.int32)
    flat_e = idx.reshape(-1)
    onehot = (flat_e[:, None] == jnp.arange(N_EXPERTS, dtype=jnp.int32)[None, :]).astype(jnp.int32)
    csum = jnp.cumsum(onehot, axis=0)
    rank = jnp.sum(onehot * csum, axis=1) - 1
    counts = csum[-1]
    padded = (counts + MOE_TM - 1) // MOE_TM * MOE_TM
    pad_end = jnp.cumsum(padded)
    pad_start = pad_end - padded
    dest = pad_start[flat_e] + rank
    n_blk = (2 * n) // MOE_TM + N_EXPERTS
    flat_tok = jnp.arange(2 * n, dtype=jnp.int32) // 2
    slot_tok = jnp.zeros((n_blk * MOE_TM,), jnp.int32).at[dest].set(
        flat_tok, unique_indices=True, mode="promise_in_bounds")
    blk_exp = jnp.minimum(jnp.searchsorted(pad_end, jnp.arange(n_blk, dtype=jnp.int32) * MOE_TM, side="right"),
                          N_EXPERTS - 1).astype(jnp.int32)
    n_used = (pad_end[-1] // MOE_TM).astype(jnp.int32).reshape(1)
    return slot_tok, blk_exp, n_used, dest.reshape(n, 2).astype(jnp.int32)


def _mla_weights(w_uq, w_ukv):
    lora = w_uq.shape[0]
    wq = w_uq.reshape(lora, GROUP_HEADS, MLA_NOPE + MLA_ROPE)
    wq = jnp.pad(wq, ((0, 0), (0, 0), (0, 2 * LANES - MLA_NOPE - MLA_ROPE)))
    wkv = w_ukv.reshape(lora, GROUP_HEADS, 2, HEAD_DIM).transpose(0, 2, 1, 3)
    return wq.reshape(lora, -1).astype(BF16), wkv.reshape(lora, -1).astype(BF16)


def kernel(x, p, attn_norm, w_in, diff_lq1, diff_lk1, diff_lq2, diff_lk2, diff_subln, na_rpb, swa_sinks,
           mla_q_norm, mla_kv_norm, mla_w_uq, mla_w_ukv, w_out, ffn_norm, dense_w_gate, dense_w_up,
           dense_w_down, moe_router, moe_w_gate, moe_w_up, moe_w_down, ple_norm, ple_gate, ple_proj,
           final_norm):
    batch, seq, d = x.shape
    n = batch * seq
    depth = w_in.shape[0]
    tabs = _rope_tables(seq)
    h = x.reshape(n, d)
    bf16 = {}

    def weight(name, w):
        return bf16.pop(name).reshape(w.shape) if name in bf16 else w.astype(BF16)

    def attend(fn, *args, rides):
        out, *copies = fn(*args, [w.reshape(-1, w.shape[-1]) for w in rides.values()], batch, seq)
        bf16.update(zip(rides, copies))
        return out

    for i in range(depth):
        lambda_init = 0.8 - 0.6 * math.exp(-0.3 * i)
        j = i // 2
        routed = i % 2 == 1
        last = i == depth - 1
        w_in_i = jnp.pad(weight(("w_in", i), w_in[i]), ((0, 0), (0, IN_COLS_PAD - IN_COLS)))
        z = _inproj(h, attn_norm[i], w_in_i, tabs, seq)
        wq, wkv = _mla_weights(mla_w_uq[i], mla_w_ukv[i])
        q_l, k_l, v_l = _mla_up(z, mla_q_norm[i], mla_kv_norm[i], wq, wkv, tabs, seq)
        diff_rides = {("w_out", i): w_out[i]}
        mla_rides = {("ple_gate", i): ple_gate[i]}
        if routed:
            diff_rides[("moe_up", j)] = moe_w_up[j]
        else:
            diff_rides.update({("dense_gate", j): dense_w_gate[j], ("dense_up", j): dense_w_up[j]})
            mla_rides[("dense_down", j)] = dense_w_down[j]
            if not last:
                diff_rides[("moe_gate", j)] = moe_w_gate[j]
                mla_rides[("moe_down", j)] = moe_w_down[j]
        if not last:
            mla_rides[("w_in", i + 1)] = w_in[i + 1]
        mixed = (
            attend(_diff_attention, z, diff_lq1[i], diff_lk1[i], diff_lq2[i], diff_lk2[i], diff_subln[i],
                   lambda_init, rides=diff_rides),
            _na_attention(z, na_rpb[i], batch, seq),
            _swa_attention(z, swa_sinks[i], batch, seq),
            attend(_mla_attention, q_l, k_l, v_l, rides=mla_rides),
        )
        w_out_i = weight(("w_out", i), w_out[i])
        if not routed:
            h1, hn = _outproj(mixed, w_out_i, h, ffn_norm[i])
            y = _dense_ffn(hn, weight(("dense_gate", j), dense_w_gate[j]), weight(("dense_up", j), dense_w_up[j]),
                           weight(("dense_down", j), dense_w_down[j]))
            routing = None
        else:
            w_router = jnp.pad(moe_router[j], ((0, 0), (0, LANES - N_EXPERTS)))
            h1, hn, route = _outproj(mixed, w_out_i, h, ffn_norm[i], w_router)
            slot_tok, blk_exp, n_used, pos = _dispatch(route, n)
            y = _moe_ffn(hn, slot_tok, blk_exp, n_used, weight(("moe_gate", j), moe_w_gate[j]),
                         weight(("moe_up", j), moe_w_up[j]), weight(("moe_down", j), moe_w_down[j]))
            routing = (route, pos)
        (h,) = _ple(h1, y, routing, p[i].reshape(n, -1), ple_norm[i], weight(("ple_gate", i), ple_gate[i]),
                    ple_proj[i].astype(BF16), final_norm, last)
    return h.reshape(batch, seq, d)
```
